```python
import math
import jax, jax.numpy as jnp
from jax import lax
import numpy as np

D_MODEL = 1024
BATCH = 16
SEQ = 4096
DEPTH = 4
DEC_BATCH = 8
DEC_SEQ = 16
PAST_LEN = 1024

CHUNK = 64
N_EVEN = (DEPTH + 1) // 2
N_ODD = DEPTH // 2
HEAD_DIM = 64
A_WIDTH = D_MODEL // 2
A_GROUP = 16
A_GROUPS = A_WIDTH // A_GROUP
A_STATE = 64
B_HEADS = (D_MODEL // 2) // HEAD_DIM
B_KV_HEADS = 2
B_GQA = B_HEADS // B_KV_HEADS
WINDOW = 128
WIN_CHUNKS = WINDOW // CHUNK
WIN_CACHE = WINDOW
N_BUCKETS = 32
MAX_DISTANCE = 128
EVEN_IN = A_WIDTH + B_HEADS * HEAD_DIM + 2 * B_KV_HEADS * HEAD_DIM
C_WIDTH = 3 * D_MODEL // 4
C_HEADS = C_WIDTH // HEAD_DIM
W_LORA = 64
A_LORA = 64
G_LORA = 128
SHIFT_WIDTH = 3 * C_WIDTH + W_LORA + A_LORA + G_LORA
D_WIDTH = D_MODEL - C_WIDTH
POOL_WINDOWS = (2, 4, 8, 16)
POOL_GROUP = D_WIDTH // len(POOL_WINDOWS)
POOL_HIST = max(POOL_WINDOWS) - 1
ODD_IN = SHIFT_WIDTH + D_WIDTH
E_GROUPS = 4
E_PER_GROUP = 8
N_EXPERTS = E_GROUPS * E_PER_GROUP
TOP_K = 2
D_EXPERT = D_MODEL // 4
MOE_BLOCK = 128
ALPHA = (2 * DEPTH) ** 0.25
BETA = (8 * DEPTH) ** -0.25
LN_EPS = 1e-5
GN_EPS = 64e-5

kernel_name = "hybrid_streaming_encoder_step"


def layer_norm(x, g, b):
    xf = x.astype(jnp.float32)
    mu = xf.mean(-1, keepdims=True)
    var = jnp.square(xf - mu).mean(-1, keepdims=True)
    y = (xf - mu) * lax.rsqrt(var + LN_EPS) * g.astype(jnp.float32) + b.astype(jnp.float32)
    return y.astype(x.dtype)


def t5_bucket(rel):
    half = N_BUCKETS // 2
    max_exact = half // 2
    n = jnp.abs(rel)
    large = max_exact + (jnp.log(jnp.maximum(n, 1).astype(jnp.float32) / max_exact)
                         / math.log(MAX_DISTANCE / max_exact) * (half - max_exact)).astype(jnp.int32)
    large = jnp.minimum(large, half - 1)
    return jnp.where(rel > 0, half, 0) + jnp.where(n < max_exact, n, large)


def s5_mixer(u, h0, a_re, a_im, log_dt, b_re, b_im, c_re, c_im, d_skip, w_glu):
    bsz, L, _ = u.shape
    f32 = jnp.float32
    uf = u.astype(f32).reshape(bsz, L, A_GROUPS, A_GROUP)
    lam = lax.complex(a_re.astype(f32), a_im.astype(f32))
    dt = jnp.exp(log_dt.astype(f32))[:, None]
    a_bar = jnp.exp(lam * dt)
    b_bar = ((a_bar - 1.0) / lam)[..., None] * lax.complex(b_re.astype(f32), b_im.astype(f32))
    bu = lax.complex(jnp.einsum('blgc,gpc->blgp', uf, b_bar.real),
                     jnp.einsum('blgc,gpc->blgp', uf, b_bar.imag))
    h0c = lax.complex(h0[..., 0].astype(f32), h0[..., 1].astype(f32))
    bu = bu.at[:, 0].add(a_bar * h0c)

    def combine(left, right):
        a_l, b_l = left
        a_r, b_r = right
        return a_r * a_l, a_r * b_l + b_r

    a_seq = jnp.broadcast_to(a_bar, (1, L) + a_bar.shape)
    _, h = lax.associative_scan(combine, (a_seq, bu), axis=1)
    y = (jnp.einsum('blgp,gcp->blgc', h.real, c_re.astype(f32))
         - jnp.einsum('blgp,gcp->blgc', h.imag, c_im.astype(f32))
         + d_skip.astype(f32) * uf)
    g = jax.nn.gelu(y.reshape(bsz, L, A_WIDTH))
    out = g * jax.nn.sigmoid(g @ w_glu.astype(f32))
    h_last = jnp.stack([h[:, -1].real, h[:, -1].imag], axis=-1)
    return out.astype(u.dtype), h_last.astype(u.dtype)


def sink_attention(q, k, v, mask, rel, rel_bias, sinks):
    f32 = jnp.float32
    bsz, n, nq = q.shape[:3]
    qg = q.reshape(bsz, n, nq, B_KV_HEADS, B_GQA, HEAD_DIM).astype(f32)
    s = jnp.einsum('bnqhgd,bnkhd->bnhgqk', qg, k.astype(f32)) * (HEAD_DIM ** -0.5)
    bias = rel_bias.astype(f32)[t5_bucket(rel)]
    bias = jnp.transpose(bias, (0, 3, 1, 2)).reshape(n, B_KV_HEADS, B_GQA, nq, -1)
    s = jnp.where(mask[:, None, None], s + bias, -jnp.inf)
    sink = sinks.astype(f32).reshape(1, 1, B_KV_HEADS, B_GQA, 1, 1)
    m = jnp.maximum(s.max(-1, keepdims=True), sink)
    p = jnp.exp(s - m)
    denom = p.sum(-1, keepdims=True) + jnp.exp(sink - m)
    o = jnp.einsum('bnhgqk,bnkhd->bnqhgd', p / denom, v.astype(f32))
    return o.reshape(bsz, n, nq, B_HEADS * HEAD_DIM)


def swa_prompt(q, k, v, rel_bias, sinks):
    bsz, L = q.shape[:2]
    nc = L // CHUNK
    pad = WIN_CHUNKS * CHUNK

    def band(t):
        tp = jnp.pad(t, ((0, 0), (pad, 0), (0, 0), (0, 0))).reshape(bsz, nc + WIN_CHUNKS, CHUNK, B_KV_HEADS, HEAD_DIM)
        return jnp.concatenate([tp[:, s:s + nc] for s in range(WIN_CHUNKS + 1)], axis=2)

    qb = q.reshape(bsz, nc, CHUNK, B_HEADS, HEAD_DIM)
    c0 = jnp.arange(nc)[:, None] * CHUNK
    qpos = c0 + jnp.arange(CHUNK)[None]
    kpos = c0 - pad + jnp.arange((WIN_CHUNKS + 1) * CHUNK)[None]
    rel = kpos[:, None, :] - qpos[:, :, None]
    mask = jnp.broadcast_to((kpos >= 0)[:, None, :], rel.shape)
    out = sink_attention(qb, band(k), band(v), mask, rel, rel_bias, sinks)
    return out.reshape(bsz, L, B_HEADS * HEAD_DIM)


def swa_sample(q, k_all, v_all, rel_bias, sinks):
    bsz, L = q.shape[:2]
    qpos = PAST_LEN + jnp.arange(L)
    kpos = PAST_LEN - WIN_CACHE + jnp.arange(k_all.shape[1])
    rel = (kpos[None, :] - qpos[:, None])[None]
    mask = jnp.broadcast_to((kpos >= 0)[None, None, :], rel.shape)
    out = sink_attention(q[:, None], k_all[:, None], v_all[:, None], mask, rel, rel_bias, sinks)
    return out.reshape(bsz, L, B_HEADS * HEAD_DIM)


def even_mixer(x, h0, cache_k, cache_v, w_in, w_out, a_re, a_im, log_dt, b_re, b_im,
               c_re, c_im, d_skip, w_glu, sinks, rel_bias):
    bsz, L, _ = x.shape
    proj = x @ w_in
    q_end = A_WIDTH + B_HEADS * HEAD_DIM
    k_end = q_end + B_KV_HEADS * HEAD_DIM
    u = proj[..., :A_WIDTH]
    q = proj[..., A_WIDTH:q_end].reshape(bsz, L, B_HEADS, HEAD_DIM)
    k = proj[..., q_end:k_end].reshape(bsz, L, B_KV_HEADS, HEAD_DIM)
    v = proj[..., k_end:].reshape(bsz, L, B_KV_HEADS, HEAD_DIM)
    a_out, h_last = s5_mixer(u, h0, a_re, a_im, log_dt, b_re, b_im, c_re, c_im, d_skip, w_glu)
    if cache_k is None:
        att = swa_prompt(q, k, v, rel_bias, sinks)
        new_k, new_v = k[:, -WIN_CACHE:], v[:, -WIN_CACHE:]
    else:
        k_all = jnp.concatenate([cache_k.astype(x.dtype), k], axis=1)
        v_all = jnp.concatenate([cache_v.astype(x.dtype), v], axis=1)
        att = swa_sample(q, k_all, v_all, rel_bias, sinks)
        new_k, new_v = k_all[:, -WIN_CACHE:], v_all[:, -WIN_CACHE:]
    y = jnp.concatenate([a_out, att.astype(x.dtype)], axis=-1) @ w_out
    return y, h_last, new_k, new_v


def rwkv_recurrence(r, decay, k, v, kk, a, s0):
    def step(s, inp):
        r_t, w_t, k_t, v_t, kk_t, a_t = inp
        sa = jnp.einsum('bhvk,bhk->bhv', s, kk_t)
        s = (s * w_t[:, :, None, :] - sa[..., None] * (kk_t * a_t)[:, :, None, :]
             + v_t[..., None] * k_t[:, :, None, :])
        return s, jnp.einsum('bhvk,bhk->bhv', s, r_t)

    xs = tuple(jnp.moveaxis(t, 1, 0) for t in (r, decay, k, v, kk, a))
    s_last, y = lax.scan(step, s0, xs)
    return jnp.moveaxis(y, 0, 1), s_last


def pool_mixer(p, hist, pos0, pool_w, pool_scale):
    bsz, L, _ = p.shape
    f32 = jnp.float32
    pf = p.astype(f32)
    full = jnp.concatenate([hist.astype(f32), pf], axis=1)
    cs = jnp.pad(jnp.cumsum(full, axis=1), ((0, 0), (1, 0), (0, 0)))
    t = jnp.arange(L)
    outs = []
    for j, w in enumerate(POOL_WINDOWS):
        sl = slice(j * POOL_GROUP, (j + 1) * POOL_GROUP)
        win_sum = cs[:, POOL_HIST + 1:POOL_HIST + 1 + L, sl] - cs[:, POOL_HIST + 1 - w:POOL_HIST + 1 - w + L, sl]
        cnt = jnp.minimum(w, pos0 + t + 1).astype(f32)[None, :, None]
        outs.append(win_sum / cnt - pf[..., sl])
    pooled = jnp.stack(outs, axis=2)
    out = jnp.einsum('blgc,gcd->blgd', pooled, pool_w.astype(f32)).reshape(bsz, L, D_WIDTH)
    out = out * pool_scale.astype(f32)
    return out.astype(p.dtype), full[:, -POOL_HIST:].astype(p.dtype)


def odd_mixer(x, shift0, s0, pool0, pos0, w_in, w_out, mu, w0, w2, a0, a2, g2,
              k_k, k_a, r_k, ln_g, ln_b, pool_w, pool_scale):
    bsz, L, _ = x.shape
    f32 = jnp.float32
    proj = x @ w_in
    sh = proj[..., :SHIFT_WIDTH]
    p = proj[..., SHIFT_WIDTH:]
    prev = jnp.concatenate([shift0[:, None].astype(x.dtype), sh[:, :-1]], axis=1)
    shm = (sh + (prev - sh) * mu).astype(f32)
    o1, o2, o3 = C_WIDTH, 2 * C_WIDTH, 3 * C_WIDTH
    r, k, v = shm[..., :o1], shm[..., o1:o2], shm[..., o2:o3]
    xw = shm[..., o3:o3 + W_LORA]
    xa = shm[..., o3 + W_LORA:o3 + W_LORA + A_LORA]
    xg = shm[..., o3 + W_LORA + A_LORA:]
    w = -jax.nn.softplus(-(w0.astype(f32) + jnp.tanh(xw) @ w2.astype(f32))) - 0.5
    decay = jnp.exp(-jnp.exp(w))
    a = jax.nn.sigmoid(a0.astype(f32) + xa @ a2.astype(f32))
    g = jax.nn.sigmoid(xg) @ g2.astype(f32)
    heads = lambda t: t.reshape(bsz, L, C_HEADS, HEAD_DIM)
    kk = heads(k * k_k.astype(f32))
    kk = kk / jnp.maximum(jnp.linalg.norm(kk, axis=-1, keepdims=True), 1e-12)
    k = k * (1.0 + (a - 1.0) * k_a.astype(f32))
    rh, kh, vh = heads(r), heads(k), heads(v)
    y, s_last = rwkv_recurrence(rh, heads(decay), kh, vh, kk, heads(a), s0.astype(f32))
    mu_y = y.mean(-1, keepdims=True)
    var_y = jnp.square(y - mu_y).mean(-1, keepdims=True)
    y = ((y - mu_y) * lax.rsqrt(var_y + GN_EPS)).reshape(bsz, L, C_WIDTH) * ln_g.astype(f32) + ln_b.astype(f32)
    bonus = (rh * kh * r_k.astype(f32)).sum(-1, keepdims=True) * vh
    c_out = (y + bonus.reshape(bsz, L, C_WIDTH)) * g
    d_out, pool_new = pool_mixer(p, pool0, pos0, pool_w, pool_scale)
    out = jnp.concatenate([c_out.astype(x.dtype), d_out], axis=-1) @ w_out
    return out, sh[:, -1], s_last.astype(x.dtype), pool_new


def hier_moe(x, w_c, b_c, w_f, b_f, w_gate, w_up, w_down):
    bsz, L, d = x.shape
    f32 = jnp.float32
    t = bsz * L
    xf = x.reshape(t, d)
    xr = xf.astype(f32)
    lc = xr @ w_c.astype(f32) + b_c.astype(f32)
    pc = jax.nn.softmax(lc, axis=-1)
    _, grp = lax.top_k(lc, 1)
    p_grp = jnp.take_along_axis(pc, grp, axis=1)
    lf = (xr @ w_f.astype(f32) + b_f.astype(f32)).reshape(t, E_GROUPS, E_PER_GROUP)
    lf_g = jnp.take_along_axis(lf, grp[:, :, None], axis=1)[:, 0]
    top_v, top_i = lax.top_k(lf_g, TOP_K)
    gate = jax.nn.softmax(top_v, axis=-1) * p_grp
    expert = grp * E_PER_GROUP + top_i
    tk = t * TOP_K
    e_flat = expert.reshape(tk)
    w_flat = gate.reshape(tk)
    tok_flat = jnp.repeat(jnp.arange(t, dtype=jnp.int32), TOP_K)
    counts = jnp.zeros((N_EXPERTS,), jnp.int32).at[e_flat].add(1)
    padded = (counts + MOE_BLOCK - 1) // MOE_BLOCK * MOE_BLOCK
    off = jnp.cumsum(counts) - counts
    pend = jnp.cumsum(padded)
    poff = pend - padded
    order = jnp.argsort(e_flat)
    se = e_flat[order]
    dest = poff[se] + jnp.arange(tk, dtype=jnp.int32) - off[se]
    n_blocks = -(-tk // MOE_BLOCK) + N_EXPERTS
    buf_tok = jnp.full((n_blocks * MOE_BLOCK,), t, jnp.int32).at[dest].set(tok_flat[order])
    buf_w = jnp.zeros((n_blocks * MOE_BLOCK,), f32).at[dest].set(w_flat[order])
    block_e = jnp.minimum(jnp.searchsorted(pend, jnp.arange(n_blocks, dtype=jnp.int32) * MOE_BLOCK, side='right'),
                          N_EXPERTS - 1)
    x_pad = jnp.concatenate([xf, jnp.zeros((1, d), xf.dtype)], axis=0)
    xb = x_pad[buf_tok].reshape(n_blocks, MOE_BLOCK, d)

    def expert_block(args):
        xblk, e = args
        h = jax.nn.silu(xblk @ w_gate[e]) * (xblk @ w_up[e])
        return h @ w_down[e]

    yb = lax.map(expert_block, (xb, block_e)).reshape(n_blocks * MOE_BLOCK, d)
    y = jax.ops.segment_sum(yb.astype(f32) * buf_w[:, None], buf_tok, num_segments=t + 1)[:t]
    return y.reshape(bsz, L, d).astype(x.dtype)


def trunk(x, s5_0, k_0, v_0, rw_0, shift_0, pool_0, pos0, w):
    bsz = x.shape[0]
    prompt = s5_0 is None
    s5_new, k_new, v_new, rw_new, sh_new, pool_new = [], [], [], [], [], []
    for i in range(DEPTH):
        j = i // 2
        if i % 2 == 0:
            h0 = jnp.zeros((bsz, A_GROUPS, A_STATE, 2), jnp.float32) if prompt else s5_0[j]
            ck = None if prompt else k_0[j]
            cv = None if prompt else v_0[j]
            mix, h1, nk, nv = even_mixer(
                x, h0, ck, cv, w["ev_w_in"][j], w["ev_w_out"][j], w["s5_a_re"][j], w["s5_a_im"][j],
                w["s5_log_dt"][j], w["s5_b_re"][j], w["s5_b_im"][j], w["s5_c_re"][j], w["s5_c_im"][j],
                w["s5_d"][j], w["s5_w_glu"][j], w["swa_sinks"][j], w["rel_bias"])
            s5_new.append(h1)
            k_new.append(nk)
            v_new.append(nv)
        else:
            sh0 = jnp.zeros((bsz, SHIFT_WIDTH), x.dtype) if prompt else shift_0[j]
            s0 = jnp.zeros((bsz, C_HEADS, HEAD_DIM, HEAD_DIM), jnp.float32) if prompt else rw_0[j]
            pl0 = jnp.zeros((bsz, POOL_HIST, D_WIDTH), x.dtype) if prompt else pool_0[j]
            mix, sh1, s1, pl1 = odd_mixer(
                x, sh0, s0, pl0, pos0, w["od_w_in"][j], w["od_w_out"][j], w["rw_mu"][j], w["rw_w0"][j],
                w["rw_w2"][j], w["rw_a0"][j], w["rw_a2"][j], w["rw_g2"][j], w["rw_k_k"][j], w["rw_k_a"][j],
                w["rw_r_k"][j], w["rw_ln_g"][j], w["rw_ln_b"][j], w["pool_w"][j], w["pool_scale"][j])
            sh_new.append(sh1)
            rw_new.append(s1)
            pool_new.append(pl1)
        x = layer_norm(ALPHA * x + mix, w["ln1_g"][i], w["ln1_b"][i])
        ffn = hier_moe(x, w["moe_w_coarse"][i], w["moe_b_coarse"][i], w["moe_w_fine"][i], w["moe_b_fine"][i],
                       w["moe_w_gate"][i], w["moe_w_up"][i], w["moe_w_down"][i])
        x = layer_norm(ALPHA * x + ffn, w["ln2_g"][i], w["ln2_b"][i])
    return (x, jnp.stack(s5_new), jnp.stack(k_new), jnp.stack(v_new),
            jnp.stack(rw_new), jnp.stack(sh_new), jnp.stack(pool_new))


def setup_inputs(seed: int = 0) -> dict:
    key = jax.random.key(seed)
    ks = iter(jax.random.split(key, 64))
    f32 = jnp.float32

    def nrm(shape, scale=1.0):
        return jax.random.normal(next(ks), shape, f32) * scale

    def unif(shape, lo, hi):
        return jax.random.uniform(next(ks), shape, f32, minval=lo, maxval=hi)

    a_im_base = jnp.pi * jnp.arange(A_STATE, dtype=f32)
    return {
        "x_prompt": nrm((BATCH, SEQ, D_MODEL)),
        "x_sample": nrm((DEC_BATCH, DEC_SEQ, D_MODEL)),
        "state_s5": nrm((N_EVEN, DEC_BATCH, A_GROUPS, A_STATE, 2), 0.5),
        "cache_swa_k": nrm((N_EVEN, DEC_BATCH, WIN_CACHE, B_KV_HEADS, HEAD_DIM)),
        "cache_swa_v": nrm((N_EVEN, DEC_BATCH, WIN_CACHE, B_KV_HEADS, HEAD_DIM)),
        "state_rwkv": nrm((N_ODD, DEC_BATCH, C_HEADS, HEAD_DIM, HEAD_DIM), 0.2),
        "state_shift": nrm((N_ODD, DEC_BATCH, SHIFT_WIDTH)),
        "state_pool": nrm((N_ODD, DEC_BATCH, POOL_HIST, D_WIDTH)),
        "rel_bias": nrm((N_BUCKETS, B_HEADS), 0.1),
        "ev_w_in": nrm((N_EVEN, D_MODEL, EVEN_IN), D_MODEL ** -0.5),
        "ev_w_out": nrm((N_EVEN, D_MODEL, D_MODEL), BETA * D_MODEL ** -0.5),
        "s5_a_re": -0.5 + nrm((N_EVEN, A_GROUPS, A_STATE), 0.01),
        "s5_a_im": a_im_base + nrm((N_EVEN, A_GROUPS, A_STATE), 0.01),
        "s5_log_dt": unif((N_EVEN, A_GROUPS), math.log(1e-3), math.log(1e-1)),
        "s5_b_re": nrm((N_EVEN, A_GROUPS, A_STATE, A_GROUP), (2 * A_GROUP) ** -0.5),
        "s5_b_im": nrm((N_EVEN, A_GROUPS, A_STATE, A_GROUP), (2 * A_GROUP) ** -0.5),
        "s5_c_re": nrm((N_EVEN, A_GROUPS, A_GROUP, A_STATE), (2 * A_STATE) ** -0.5),
        "s5_c_im": nrm((N_EVEN, A_GROUPS, A_GROUP, A_STATE), (2 * A_STATE) ** -0.5),
        "s5_d": nrm((N_EVEN, A_GROUPS, A_GROUP)),
        "s5_w_glu": nrm((N_EVEN, A_WIDTH, A_WIDTH), A_WIDTH ** -0.5),
        "swa_sinks": nrm((N_EVEN, B_HEADS)),
        "od_w_in": nrm((N_ODD, D_MODEL, ODD_IN), D_MODEL ** -0.5),
        "od_w_out": nrm((N_ODD, D_MODEL, D_MODEL), BETA * D_MODEL ** -0.5),
        "rw_mu": unif((N_ODD, SHIFT_WIDTH), 0.0, 1.0),
        "rw_w0": unif((N_ODD, C_WIDTH), -6.0, -1.0),
        "rw_w2": nrm((N_ODD, W_LORA, C_WIDTH), 0.1 * W_LORA ** -0.5),
        "rw_a0": nrm((N_ODD, C_WIDTH), 0.1),
        "rw_a2": nrm((N_ODD, A_LORA, C_WIDTH), A_LORA ** -0.5),
        "rw_g2": nrm((N_ODD, G_LORA, C_WIDTH), G_LORA ** -0.5),
        "rw_k_k": 0.85 + nrm((N_ODD, C_WIDTH), 0.05),
        "rw_k_a": 1.0 + nrm((N_ODD, C_WIDTH), 0.05),
        "rw_r_k": nrm((N_ODD, C_HEADS, HEAD_DIM), 0.1),
        "rw_ln_g": 1.0 + nrm((N_ODD, C_WIDTH), 0.05),
        "rw_ln_b": nrm((N_ODD, C_WIDTH), 0.02),
        "pool_w": nrm((N_ODD, len(POOL_WINDOWS), POOL_GROUP, POOL_GROUP), POOL_GROUP ** -0.5),
        "pool_scale": 1.0 + nrm((N_ODD, D_WIDTH), 0.1),
        "ln1_g": 1.0 + nrm((DEPTH, D_MODEL), 0.05),
        "ln1_b": nrm((DEPTH, D_MODEL), 0.02),
        "ln2_g": 1.0 + nrm((DEPTH, D_MODEL), 0.05),
        "ln2_b": nrm((DEPTH, D_MODEL), 0.02),
        "moe_w_coarse": nrm((DEPTH, D_MODEL, E_GROUPS), D_MODEL ** -0.5),
        "moe_b_coarse": nrm((DEPTH, E_GROUPS), 0.01),
        "moe_w_fine": nrm((DEPTH, D_MODEL, N_EXPERTS), D_MODEL ** -0.5),
        "moe_b_fine": nrm((DEPTH, N_EXPERTS), 0.01),
        "moe_w_gate": nrm((DEPTH, N_EXPERTS, D_MODEL, D_EXPERT), D_MODEL ** -0.5),
        "moe_w_up": nrm((DEPTH, N_EXPERTS, D_MODEL, D_EXPERT), D_MODEL ** -0.5),
        "moe_w_down": nrm((DEPTH, N_EXPERTS, D_EXPERT, D_MODEL), BETA * D_EXPERT ** -0.5),
    }


def reference(x_prompt, x_sample, state_s5, cache_swa_k, cache_swa_v, state_rwkv, state_shift, state_pool,
              rel_bias, ev_w_in, ev_w_out, s5_a_re, s5_a_im, s5_log_dt, s5_b_re, s5_b_im, s5_c_re, s5_c_im,
              s5_d, s5_w_glu, swa_sinks, od_w_in, od_w_out, rw_mu, rw_w0, rw_w2, rw_a0, rw_a2, rw_g2,
              rw_k_k, rw_k_a, rw_r_k, rw_ln_g, rw_ln_b, pool_w, pool_scale, ln1_g, ln1_b, ln2_g, ln2_b,
              moe_w_coarse, moe_b_coarse, moe_w_fine, moe_b_fine, moe_w_gate, moe_w_up, moe_w_down):
    w = dict(rel_bias=rel_bias, ev_w_in=ev_w_in, ev_w_out=ev_w_out, s5_a_re=s5_a_re, s5_a_im=s5_a_im,
             s5_log_dt=s5_log_dt, s5_b_re=s5_b_re, s5_b_im=s5_b_im, s5_c_re=s5_c_re, s5_c_im=s5_c_im,
             s5_d=s5_d, s5_w_glu=s5_w_glu, swa_sinks=swa_sinks, od_w_in=od_w_in, od_w_out=od_w_out,
             rw_mu=rw_mu, rw_w0=rw_w0, rw_w2=rw_w2, rw_a0=rw_a0, rw_a2=rw_a2, rw_g2=rw_g2, rw_k_k=rw_k_k,
             rw_k_a=rw_k_a, rw_r_k=rw_r_k, rw_ln_g=rw_ln_g, rw_ln_b=rw_ln_b, pool_w=pool_w,
             pool_scale=pool_scale, ln1_g=ln1_g, ln1_b=ln1_b, ln2_g=ln2_g, ln2_b=ln2_b,
             moe_w_coarse=moe_w_coarse, moe_b_coarse=moe_b_coarse, moe_w_fine=moe_w_fine,
             moe_b_fine=moe_b_fine, moe_w_gate=moe_w_gate, moe_w_up=moe_w_up, moe_w_down=moe_w_down)
    y_prompt, s5_p, k_p, v_p, rw_p, sh_p, pool_p = trunk(x_prompt, None, None, None, None, None, None, 0, w)
    y_sample, s5_s, k_s, v_s, rw_s, sh_s, pool_s = trunk(
        x_sample, state_s5, cache_swa_k, cache_swa_v, state_rwkv, state_shift, state_pool, PAST_LEN, w)
    return (y_prompt, y_sample, s5_p, s5_s, k_p, v_p, k_s, v_s, rw_p, rw_s, sh_p, sh_s, pool_p, pool_s)
```

```python
import functools
import math

import jax
import jax.numpy as jnp
from jax import lax
from jax.experimental import pallas as pl
from jax.experimental.pallas import tpu as pltpu

F32 = jnp.float32
BF16 = jnp.bfloat16
I32 = jnp.int32

CHUNK = 64
WINDOW = 128
HEAD_DIM = 64
PAST_LEN = 1024
N_BUCKETS = 32
MAX_DISTANCE = 128
POOL_WINDOWS = (2, 4, 8, 16)
POOL_HIST = max(POOL_WINDOWS) - 1
E_GROUPS = 4
LN_EPS = 1e-5
GN_EPS = 64e-5
A_GROUP = 16

LANES = 128
SUBLANES = 8
VMEM_LIMIT_BYTES = 56 * 1024 * 1024

ROW_TILE = 512
MOE_BLOCK = 256


def _params(*sem):
    return pltpu.CompilerParams(dimension_semantics=sem, vmem_limit_bytes=VMEM_LIMIT_BYTES)


def _sigmoid(x):
    return 1.0 / (1.0 + jnp.exp(-x))


def _layer_norm(y, g, b):
    mu = jnp.mean(y, axis=-1, keepdims=True)
    d = y - mu
    var = jnp.mean(d * d, axis=-1, keepdims=True)
    return d * lax.rsqrt(var + LN_EPS) * g + b


def _bdot(a, b):
    return jnp.dot(a, b, preferred_element_type=F32)


def _mm_kernel(*refs, n_in, splits, epilogue, alpha):
    xs = refs[:n_in]
    ws = refs[n_in:2 * n_in]
    pos = 2 * n_in
    acc = None
    for x_ref, w_ref in zip(xs, ws):
        d = _bdot(x_ref[...].astype(BF16), w_ref[...])
        acc = d if acc is None else acc + d
    if epilogue == "ln":
        res_ref, g_ref, b_ref = refs[pos:pos + 3]
        pos += 3
        acc = _layer_norm(alpha * res_ref[...] + acc, g_ref[...], b_ref[...])
    elif epilogue == "glu":
        acc = xs[0][...] * _sigmoid(acc)
    off = 0
    for o_ref, n in zip(refs[pos:], splits):
        o_ref[...] = acc[:, off:off + n]
        off += n


def _matmul(xs, ws, splits=None, epilogue=None, res=None, g=None, b=None, alpha=None):
    t = xs[0].shape[0]
    n = ws[0].shape[1]
    splits = tuple(splits) if splits else (n,)
    tm = min(ROW_TILE, t)
    in_specs = [pl.BlockSpec((tm, x.shape[1]), lambda i: (i, 0)) for x in xs]
    in_specs += [pl.BlockSpec(w.shape, lambda i: (0, 0)) for w in ws]
    args = list(xs) + list(ws)
    if epilogue == "ln":
        in_specs += [pl.BlockSpec((tm, n), lambda i: (i, 0)),
                     pl.BlockSpec((1, n), lambda i: (0, 0)), pl.BlockSpec((1, n), lambda i: (0, 0))]
        args += [res, g.reshape(1, n), b.reshape(1, n)]
    outs = pl.pallas_call(
        functools.partial(_mm_kernel, n_in=len(xs), splits=splits, epilogue=epilogue, alpha=alpha),
        grid=(t // tm,),
        in_specs=in_specs,
        out_specs=[pl.BlockSpec((tm, s), lambda i: (i, 0)) for s in splits],
        out_shape=[jax.ShapeDtypeStruct((t, s), F32) for s in splits],
        compiler_params=_params("parallel"),
    )(*args)
    return outs if len(splits) > 1 else outs[0]


def _s5_kernel(u_ref, bre_ref, bim_ref, cre_ref, cim_ref, are_ref, aim_ref, d_ref, h0re_ref, h0im_ref,
               g_ref, hnre_ref, hnim_ref, hre, him, car_re, car_im, *, batch, steps):
    ti = pl.program_id(1)

    @pl.when(ti == 0)
    def _():
        car_re[...] = h0re_ref[...]
        car_im[...] = h0im_ref[...]

    u = u_ref[...]
    ub = u.astype(BF16)
    hre[...] = _bdot(ub, bre_ref[...])
    him[...] = _bdot(ub, bim_ref[...])
    a_re = jnp.broadcast_to(are_ref[...], car_re.shape)
    a_im = jnp.broadcast_to(aim_ref[...], car_re.shape)

    def step(t, carry):
        h_re, h_im = carry
        r0 = pl.multiple_of(t * batch, batch)
        n_re = a_re * h_re - a_im * h_im + hre[pl.ds(r0, batch), :]
        n_im = a_re * h_im + a_im * h_re + him[pl.ds(r0, batch), :]
        hre[pl.ds(r0, batch), :] = n_re
        him[pl.ds(r0, batch), :] = n_im
        return n_re, n_im

    h_re, h_im = lax.fori_loop(0, steps, step, (car_re[...], car_im[...]))
    car_re[...] = h_re
    car_im[...] = h_im
    hnre_ref[...] = h_re
    hnim_ref[...] = h_im
    y = (_bdot(hre[...].astype(BF16), cre_ref[...]) - _bdot(him[...].astype(BF16), cim_ref[...])
         + d_ref[...] * u)
    cdf = 0.5 * (1.0 + jnp.tanh(math.sqrt(2.0 / math.pi) * (y + 0.044715 * (y * y * y))))
    g_ref[...] = y * cdf


def _s5(u, batch, h0re, h0im, prm):
    t, width = u.shape
    nt = width // LANES
    ns = prm["bre"].shape[-1]
    rows = min(1024, t)
    steps = rows // batch
    tile = lambda j, i: (j, 0, 0)
    return pl.pallas_call(
        functools.partial(_s5_kernel, batch=batch, steps=steps),
        grid=(nt, t // rows),
        in_specs=[pl.BlockSpec((rows, LANES), lambda j, i: (i, j)),
                  pl.BlockSpec((None, LANES, ns), tile), pl.BlockSpec((None, LANES, ns), tile),
                  pl.BlockSpec((None, ns, LANES), tile), pl.BlockSpec((None, ns, LANES), tile),
                  pl.BlockSpec((None, 1, ns), tile), pl.BlockSpec((None, 1, ns), tile),
                  pl.BlockSpec((None, 1, LANES), tile),
                  pl.BlockSpec((None, batch, ns), tile), pl.BlockSpec((None, batch, ns), tile)],
        out_specs=[pl.BlockSpec((rows, LANES), lambda j, i: (i, j)),
                   pl.BlockSpec((None, batch, ns), tile), pl.BlockSpec((None, batch, ns), tile)],
        out_shape=[jax.ShapeDtypeStruct((t, width), F32),
                   jax.ShapeDtypeStruct((nt, batch, ns), F32), jax.ShapeDtypeStruct((nt, batch, ns), F32)],
        scratch_shapes=[pltpu.VMEM((rows, ns), F32), pltpu.VMEM((rows, ns), F32),
                        pltpu.VMEM((batch, ns), F32), pltpu.VMEM((batch, ns), F32)],
        compiler_params=_params("parallel", "arbitrary"),
    )(u, prm["bre"], prm["bim"], prm["cre"], prm["cim"], prm["are"], prm["aim"], prm["d"], h0re, h0im)


def _s5_params(a_re, a_im, log_dt, b_re, b_im, c_re, c_im, d_skip):
    groups, n_state = a_re.shape
    gpt = LANES // A_GROUP
    nt = groups // gpt
    lam = lax.complex(a_re.astype(F32), a_im.astype(F32))
    dt = jnp.exp(log_dt.astype(F32))[:, None]
    a_bar = jnp.exp(lam * dt)
    b_bar = ((a_bar - 1.0) / lam)[..., None] * lax.complex(b_re.astype(F32), b_im.astype(F32))
    eye = jnp.eye(gpt, dtype=F32)

    def in_proj(m):
        m = m.reshape(nt, gpt, n_state, A_GROUP).transpose(0, 1, 3, 2)
        return (m[:, :, :, None, :] * eye[None, :, None, :, None]).reshape(nt, gpt * A_GROUP, gpt * n_state)

    def out_proj(m):
        m = m.astype(F32).reshape(nt, gpt, A_GROUP, n_state).transpose(0, 1, 3, 2)
        return (m[:, :, :, None, :] * eye[None, :, None, :, None]).reshape(nt, gpt * n_state, gpt * A_GROUP)

    return dict(bre=in_proj(b_bar.real).astype(BF16), bim=in_proj(b_bar.imag).astype(BF16),
                cre=out_proj(c_re).astype(BF16), cim=out_proj(c_im).astype(BF16),
                are=a_bar.real.reshape(nt, 1, gpt * n_state), aim=a_bar.imag.reshape(nt, 1, gpt * n_state),
                d=d_skip.astype(F32).reshape(nt, 1, LANES))


def _attn_kernel(*refs, n_kv, masked, n_heads, gqa):
    q_ref = refs[0]
    k_refs = refs[1:1 + n_kv]
    v_refs = refs[1 + n_kv:1 + 2 * n_kv]
    bias_ref, sink_ref, o_ref = refs[1 + 2 * n_kv:]
    c = pl.program_id(1)
    qb = q_ref[...].astype(BF16)
    kb = jnp.concatenate([r[...] for r in k_refs], axis=0).astype(BF16)
    vb = jnp.concatenate([r[...] for r in v_refs], axis=0).astype(BF16)
    cq, nk = qb.shape[0], kb.shape[0]
    if masked:
        col = lax.broadcasted_iota(I32, (cq, nk), 1)
        valid = (c - (n_kv - 1)) * CHUNK + col >= 0
    outs = []
    for h in range(n_heads):
        kvh = h // gqa
        qh = qb[:, h * HEAD_DIM:(h + 1) * HEAD_DIM]
        kh = kb[:, kvh * HEAD_DIM:(kvh + 1) * HEAD_DIM]
        vh = vb[:, kvh * HEAD_DIM:(kvh + 1) * HEAD_DIM]
        s = lax.dot_general(qh, kh, (((1,), (1,)), ((), ())), preferred_element_type=F32)
        s = s * (HEAD_DIM ** -0.5) + bias_ref[h]
        if masked:
            s = jnp.where(valid, s, -jnp.inf)
        sink = sink_ref[h]
        m = jnp.maximum(jnp.max(s, axis=-1, keepdims=True), sink)
        p = jnp.exp(s - m)
        denom = jnp.sum(p, axis=-1, keepdims=True) + jnp.exp(sink - m)
        outs.append(_bdot((p / denom).astype(BF16), vh))
    o_ref[...] = jnp.concatenate(outs, axis=-1)


def _attention(q, k, v, bias, sinks, cq, ck, n_kv, masked):
    bsz, lq, qw = q.shape
    kw = k.shape[-1]
    n_heads = qw // HEAD_DIM
    gqa = n_heads // (kw // HEAD_DIM)

    def kv_spec(s):
        return pl.BlockSpec((None, ck, kw), lambda b, c: (b, jnp.maximum(c - (n_kv - 1) + s, 0), 0))

    return pl.pallas_call(
        functools.partial(_attn_kernel, n_kv=n_kv, masked=masked, n_heads=n_heads, gqa=gqa),
        grid=(bsz, lq // cq),
        in_specs=[pl.BlockSpec((None, cq, qw), lambda b, c: (b, c, 0))]
        + [kv_spec(s) for s in range(n_kv)] + [kv_spec(s) for s in range(n_kv)]
        + [pl.BlockSpec(bias.shape, lambda b, c: (0, 0, 0)), pl.BlockSpec(memory_space=pltpu.SMEM)],
        out_specs=pl.BlockSpec((None, cq, qw), lambda b, c: (b, c, 0)),
        out_shape=jax.ShapeDtypeStruct((bsz, lq, qw), F32),
        compiler_params=_params("parallel", "arbitrary"),
    )(q, *([k] * n_kv), *([v] * n_kv), bias, sinks)


def _t5_bucket(rel):
    half = N_BUCKETS // 2
    max_exact = half // 2
    n = jnp.abs(rel)
    large = max_exact + (jnp.log(jnp.maximum(n, 1).astype(F32) / max_exact)
                         / math.log(MAX_DISTANCE / max_exact) * (half - max_exact)).astype(I32)
    large = jnp.minimum(large, half - 1)
    return jnp.where(rel > 0, half, 0) + jnp.where(n < max_exact, n, large)


def _bias_table(rel_bias, nq, nk):
    rel = (jnp.arange(nk) - WINDOW)[None, :] - jnp.arange(nq)[:, None]
    return jnp.transpose(rel_bias.astype(F32)[_t5_bucket(rel)], (2, 0, 1))


def _router_kernel(x_ref, w_ref, b_ref, route_ref, cnt_ref, carry, *, n_fine):
    @pl.when(pl.program_id(0) == 0)
    def _():
        carry[...] = jnp.zeros_like(carry)

    logits = _bdot(x_ref[...].astype(BF16), w_ref[...]) + b_ref[...]
    tm = logits.shape[0]
    lane = lax.broadcasted_iota(I32, logits.shape, 1)
    lanef = lane.astype(F32)
    per_group = n_fine // E_GROUPS
    big = float(LANES)

    def first_max(mask):
        mx = jnp.max(jnp.where(mask, logits, -jnp.inf), axis=-1, keepdims=True)
        idx = jnp.min(jnp.where(mask & (logits == mx), lanef, big), axis=-1, keepdims=True)
        return mx, idx

    cmask = lane < E_GROUPS
    mc, grp = first_max(cmask)
    p_grp = 1.0 / jnp.sum(jnp.where(cmask, jnp.exp(logits - mc), 0.0), axis=-1, keepdims=True)
    lo = E_GROUPS + per_group * grp
    fmask = (lanef >= lo) & (lanef < lo + per_group)
    m1, i1 = first_max(fmask)
    sel0 = lanef == i1
    m2, i2 = first_max(fmask & jnp.logical_not(sel0))
    sel1 = lanef == i2
    e = jnp.exp(m2 - m1)
    g0 = (1.0 / (1.0 + e)) * p_grp
    g1 = (e / (1.0 + e)) * p_grp

    cmat = jnp.where(sel0 | sel1, 1.0, 0.0)
    row = lax.broadcasted_iota(I32, (tm, tm), 0)
    col = lax.broadcasted_iota(I32, (tm, tm), 1)
    tril = jnp.where(col < row, 1.0, 0.0).astype(BF16)
    before = carry[...] + _bdot(tril, cmat.astype(BF16))
    r0 = jnp.sum(jnp.where(sel0, before, 0.0), axis=-1, keepdims=True)
    r1 = jnp.sum(jnp.where(sel1, before, 0.0), axis=-1, keepdims=True)
    carry[...] = carry[...] + jnp.sum(cmat, axis=0, keepdims=True)
    cnt_ref[...] = carry[...]
    vals = (i1 - E_GROUPS, i2 - E_GROUPS, r0, r1, g0, g1)
    route = jnp.zeros_like(logits)
    for j, val in enumerate(vals):
        route = jnp.where(lane == j, val, route)
    route_ref[...] = route


def _router(x, w_route, b_route, n_fine):
    t, d = x.shape
    tm = min(ROW_TILE, t)
    return pl.pallas_call(
        functools.partial(_router_kernel, n_fine=n_fine),
        grid=(t // tm,),
        in_specs=[pl.BlockSpec((tm, d), lambda i: (i, 0)), pl.BlockSpec((d, LANES), lambda i: (0, 0)),
                  pl.BlockSpec((1, LANES), lambda i: (0, 0))],
        out_specs=[pl.BlockSpec((tm, LANES), lambda i: (i, 0)), pl.BlockSpec((1, LANES), lambda i: (0, 0))],
        out_shape=[jax.ShapeDtypeStruct((t, LANES), F32), jax.ShapeDtypeStruct((1, LANES), F32)],
        scratch_shapes=[pltpu.VMEM((1, LANES), F32)],
        compiler_params=_params("arbitrary"),
    )(x, w_route, b_route)


def _row_copy(src_ref, src_row, dst_ref, dst_row, sem):
    return pltpu.make_async_copy(src_ref.at[pl.ds(src_row, 1)], dst_ref.at[pl.ds(dst_row, 1)], sem)


def _dispatch_kernel(dest_ref, x_ref, xs_in_ref, xs_ref, sem, *, tm, top_k):
    del xs_in_ref

    def issue(r, carry):
        for s in range(top_k):
            _row_copy(x_ref, r, xs_ref, dest_ref[s, r], sem).start()
        return carry

    def drain(r, carry):
        for s in range(top_k):
            _row_copy(x_ref, r, xs_ref, dest_ref[s, r], sem).wait()
        return carry

    lax.fori_loop(0, tm, issue, 0)
    lax.fori_loop(0, tm, drain, 0)


def _dispatch(dest, x, n_rows):
    t, d = x.shape
    top_k = dest.shape[0]
    tm = min(ROW_TILE, t)
    return pl.pallas_call(
        functools.partial(_dispatch_kernel, tm=tm, top_k=top_k),
        grid=(t // tm,),
        in_specs=[pl.BlockSpec((top_k, tm), lambda i: (0, i), memory_space=pltpu.SMEM),
                  pl.BlockSpec((tm, d), lambda i: (i, 0)),
                  pl.BlockSpec(memory_space=pl.ANY)],
        out_specs=pl.BlockSpec(memory_space=pl.ANY),
        out_shape=jax.ShapeDtypeStruct((n_rows, d), F32),
        scratch_shapes=[pltpu.SemaphoreType.DMA(())],
        input_output_aliases={2: 0},
        compiler_params=_params("arbitrary"),
    )(dest, x, jnp.zeros((n_rows, d), F32))


def _expert_kernel(be_ref, nu_ref, xs_ref, wg_ref, wu_ref, wd_ref, ys_ref):
    del be_ref
    i = pl.program_id(0)

    @pl.when(i < nu_ref[0])
    def _():
        xb = xs_ref[...].astype(BF16)
        a = _bdot(xb, wg_ref[...])
        h = (a * _sigmoid(a)) * _bdot(xb, wu_ref[...])
        ys_ref[...] = _bdot(h.astype(BF16), wd_ref[...])

    @pl.when(i >= nu_ref[0])
    def _():
        ys_ref[...] = jnp.zeros_like(ys_ref)


def _experts(block_e, n_used, xs, wg, wu, wd):
    n_rows, d = xs.shape
    de = wg.shape[-1]
    grid_spec = pltpu.PrefetchScalarGridSpec(
        num_scalar_prefetch=2,
        grid=(n_rows // MOE_BLOCK,),
        in_specs=[pl.BlockSpec((MOE_BLOCK, d), lambda i, be, nu: (i, 0)),
                  pl.BlockSpec((None, d, de), lambda i, be, nu: (be[i], 0, 0)),
                  pl.BlockSpec((None, d, de), lambda i, be, nu: (be[i], 0, 0)),
                  pl.BlockSpec((None, de, d), lambda i, be, nu: (be[i], 0, 0))],
        out_specs=pl.BlockSpec((MOE_BLOCK, d), lambda i, be, nu: (i, 0)),
    )
    return pl.pallas_call(
        _expert_kernel, grid_spec=grid_spec,
        out_shape=jax.ShapeDtypeStruct((n_rows, d), F32),
        compiler_params=_params("arbitrary"),
    )(block_e, n_used, xs, wg, wu, wd)


def _combine_kernel(dest_ref, x_ref, gate_ref, g_ref, b_ref, ys_ref, o_ref, ybuf, sem, *, tm, top_k, alpha):
    def issue(r, carry):
        for s in range(top_k):
            _row_copy(ys_ref, dest_ref[s, r], ybuf.at[s], r, sem).start()
        return carry

    def drain(r, carry):
        for s in range(top_k):
            _row_copy(ys_ref, dest_ref[s, r], ybuf.at[s], r, sem).wait()
        return carry

    lax.fori_loop(0, tm, issue, 0)
    lax.fori_loop(0, tm, drain, 0)
    gate = gate_ref[...]
    y = alpha * x_ref[...]
    for s in range(top_k):
        y = y + gate[:, s:s + 1] * ybuf[s]
    o_ref[...] = _layer_norm(y, g_ref[...], b_ref[...])


def _combine(dest, x, gates, ys, ln_g, ln_b, alpha):
    t, d = x.shape
    top_k = dest.shape[0]
    tm = min(ROW_TILE, t)
    return pl.pallas_call(
        functools.partial(_combine_kernel, tm=tm, top_k=top_k, alpha=alpha),
        grid=(t // tm,),
        in_specs=[pl.BlockSpec((top_k, tm), lambda i: (0, i), memory_space=pltpu.SMEM),
                  pl.BlockSpec((tm, d), lambda i: (i, 0)),
                  pl.BlockSpec((tm, top_k), lambda i: (i, 0)),
                  pl.BlockSpec((1, d), lambda i: (0, 0)), pl.BlockSpec((1, d), lambda i: (0, 0)),
                  pl.BlockSpec(memory_space=pl.ANY)],
        out_specs=pl.BlockSpec((tm, d), lambda i: (i, 0)),
        out_shape=jax.ShapeDtypeStruct((t, d), F32),
        scratch_shapes=[pltpu.VMEM((top_k, tm, d), F32), pltpu.SemaphoreType.DMA(())],
        compiler_params=_params("arbitrary"),
    )(dest, x, gates, ln_g.reshape(1, d), ln_b.reshape(1, d), ys)


def _moe_ln(x, w_c, b_c, w_f, b_f, wg, wu, wd, ln_g, ln_b, alpha):
    t, d = x.shape
    n_exp = w_f.shape[1]
    pad = LANES - E_GROUPS - n_exp
    w_route = jnp.concatenate([w_c.astype(F32), w_f.astype(F32), jnp.zeros((d, pad), F32)], axis=1)
    b_route = jnp.concatenate([b_c.astype(F32), b_f.astype(F32), jnp.zeros((pad,), F32)]).reshape(1, LANES)
    route, cnt = _router(x, w_route.astype(BF16), b_route, n_exp)
    experts = route[:, 0:2].astype(I32)
    ranks = route[:, 2:4].astype(I32)
    gates = route[:, 4:6]
    top_k = experts.shape[1]
    counts = cnt[0, E_GROUPS:E_GROUPS + n_exp].astype(I32)
    padded = (counts + MOE_BLOCK - 1) // MOE_BLOCK * MOE_BLOCK
    pend = jnp.cumsum(padded)
    poff = pend - padded
    dest = (poff[experts] + ranks).T
    n_blocks = (t * top_k) // MOE_BLOCK + n_exp
    block_e = jnp.minimum(jnp.searchsorted(pend, jnp.arange(n_blocks, dtype=I32) * MOE_BLOCK, side="right"),
                          n_exp - 1).astype(I32)
    n_used = (pend[-1] // MOE_BLOCK).astype(I32).reshape(1)
    xs = _dispatch(dest, x, n_blocks * MOE_BLOCK)
    ys = _experts(block_e, n_used, xs, wg, wu, wd)
    return _combine(dest, x, gates, ys, ln_g, ln_b, alpha)


def _head_sum(x, ones):
    hi = x.astype(BF16)
    lo = (x - hi.astype(F32)).astype(BF16)
    parts = []
    for j in range(x.shape[1] // LANES):
        sl = slice(j * LANES, (j + 1) * LANES)
        parts.append(_bdot(hi[:, sl], ones) + _bdot(lo[:, sl], ones))
    return jnp.concatenate(parts, axis=-1)


def _rw_prep_kernel(sh_ref, halo_ref, sh0_ref, mu_ref, w0_ref, w2_ref, a0_ref, a2_ref, g2_ref,
                    kk_ref, ka_ref, rk_ref, ones_ref,
                    r_o, w_o, k_o, v_o, kk_o, b_o, bonus_o, g_o, *, batch, cw):
    sh = sh_ref[...]
    first = jnp.where(pl.program_id(0) == 0, sh0_ref[...], halo_ref[...])
    prev = jnp.concatenate([first, sh[:sh.shape[0] - batch]], axis=0)
    shm = sh + (prev - sh) * mu_ref[...]
    r, k, v = shm[:, :cw], shm[:, cw:2 * cw], shm[:, 2 * cw:3 * cw]
    xwa = shm[:, 3 * cw:3 * cw + LANES]
    xg = shm[:, 3 * cw + LANES:]
    ones = ones_ref[...]
    z = w0_ref[...] + _bdot(jnp.tanh(xwa).astype(BF16), w2_ref[...])
    softplus = jnp.maximum(-z, 0.0) + jnp.log1p(jnp.exp(-jnp.abs(z)))
    w_o[...] = jnp.exp(-jnp.exp(-softplus - 0.5))
    a = _sigmoid(a0_ref[...] + _bdot(xwa.astype(BF16), a2_ref[...]))
    g_o[...] = _bdot(_sigmoid(xg).astype(BF16), g2_ref[...])
    kk = k * kk_ref[...]
    kk = kk / jnp.maximum(jnp.sqrt(_head_sum(kk * kk, ones)), 1e-12)
    k2 = k * (1.0 + (a - 1.0) * ka_ref[...])
    r_o[...] = r
    k_o[...] = k2
    v_o[...] = v
    kk_o[...] = kk
    b_o[...] = kk * a
    bonus_o[...] = _head_sum(r * k2 * rk_ref[...], ones) * v


def _rw_prep(sh, shift0, batch, prm):
    t, sw = sh.shape
    cw = prm["w0"].shape[-1]
    tm = min(256, t)
    hb = tm // batch
    row = lambda i: (i, 0)
    fix = lambda i: (0, 0)
    vec = lambda n: pl.BlockSpec((1, n), fix)
    return pl.pallas_call(
        functools.partial(_rw_prep_kernel, batch=batch, cw=cw),
        grid=(t // tm,),
        in_specs=[pl.BlockSpec((tm, sw), row),
                  pl.BlockSpec((batch, sw), lambda i: (jnp.maximum(i * hb - 1, 0), 0)),
                  pl.BlockSpec((batch, sw), fix), vec(sw), vec(cw),
                  pl.BlockSpec((LANES, cw), fix), vec(cw), pl.BlockSpec((LANES, cw), fix),
                  pl.BlockSpec((LANES, cw), fix), vec(cw), vec(cw), vec(cw),
                  pl.BlockSpec((LANES, LANES), fix)],
        out_specs=[pl.BlockSpec((tm, cw), row)] * 8,
        out_shape=[jax.ShapeDtypeStruct((t, cw), F32)] * 8,
        compiler_params=_params("parallel"),
    )(sh, sh, shift0, prm["mu"], prm["w0"], prm["w2"], prm["a0"], prm["a2"], prm["g2"],
      prm["k_k"], prm["k_a"], prm["r_k"], prm["ones"])


def _rwkv_kernel(r_ref, w_ref, k_ref, v_ref, kk_ref, b_ref, s0_ref, y_ref, sn_ref, state, xt,
                 *, batch, steps, n_tiles):
    ti = pl.program_id(0)

    @pl.when(ti == 0)
    def _():
        state[...] = s0_ref[...]

    operands = (r_ref, w_ref, k_ref, v_ref, kk_ref, b_ref)
    chains = n_tiles * batch
    hd = HEAD_DIM

    def time_step(t, carry):
        r0 = pl.multiple_of(t * batch, batch)
        for o, ref in enumerate(operands):
            slab = ref[pl.ds(r0, batch), :]
            parts = [slab[:, j * LANES:(j + 1) * LANES] for j in range(n_tiles)]
            if chains < LANES:
                parts.append(jnp.zeros((LANES - chains, LANES), F32))
            xt[o] = jnp.concatenate(parts, axis=0).T
        ys = []
        for par in range(2):
            base = par * hd

            def sa_body(kidx, acc):
                return acc + state[par, kidx] * xt[4, pl.ds(base + kidx, 1), :]

            sa = lax.fori_loop(0, hd, sa_body, jnp.zeros((hd, LANES), F32))
            vv = xt[3, pl.ds(base, hd), :]

            def up_body(kidx, acc):
                row = base + kidx
                sn = (state[par, kidx] * xt[1, pl.ds(row, 1), :] - sa * xt[5, pl.ds(row, 1), :]
                      + vv * xt[2, pl.ds(row, 1), :])
                state[par, kidx] = sn
                return acc + sn * xt[0, pl.ds(row, 1), :]

            ys.append(lax.fori_loop(0, hd, up_body, jnp.zeros((hd, LANES), F32)))
        yt = jnp.concatenate(ys, axis=0).T
        for j in range(n_tiles):
            y_ref[pl.ds(r0, batch), j * LANES:(j + 1) * LANES] = yt[j * batch:(j + 1) * batch, :]
        return carry

    lax.fori_loop(0, steps, time_step, 0)

    @pl.when(ti == pl.num_programs(0) - 1)
    def _():
        sn_ref[...] = state[...]


def _rwkv(ops, s0c, batch):
    t, cw = ops[0].shape
    n_tiles = cw // LANES
    steps = min(16, t // batch)
    rows = steps * batch
    blk = pl.BlockSpec((rows, cw), lambda i: (i, 0))
    st = pl.BlockSpec(s0c.shape, lambda i: (0, 0, 0, 0))
    return pl.pallas_call(
        functools.partial(_rwkv_kernel, batch=batch, steps=steps, n_tiles=n_tiles),
        grid=(t // rows,),
        in_specs=[blk] * 6 + [st],
        out_specs=[blk, st],
        out_shape=[jax.ShapeDtypeStruct((t, cw), F32), jax.ShapeDtypeStruct(s0c.shape, F32)],
        scratch_shapes=[pltpu.VMEM(s0c.shape, F32), pltpu.VMEM((6, LANES, LANES), F32)],
        compiler_params=_params("arbitrary"),
    )(*ops, s0c)


def _rw_post_kernel(y_ref, bonus_ref, g_ref, lg_ref, lb_ref, ones_ref, o_ref):
    y = y_ref[...]
    ones = ones_ref[...]
    mu = _head_sum(y, ones) * (1.0 / HEAD_DIM)
    d = y - mu
    var = _head_sum(d * d, ones) * (1.0 / HEAD_DIM)
    yn = d * lax.rsqrt(var + GN_EPS) * lg_ref[...] + lb_ref[...]
    o_ref[...] = (yn + bonus_ref[...]) * g_ref[...]


def _rw_post(y, bonus, g, ln_g, ln_b, ones):
    t, cw = y.shape
    tm = min(ROW_TILE, t)
    row = pl.BlockSpec((tm, cw), lambda i: (i, 0))
    vec = pl.BlockSpec((1, cw), lambda i: (0, 0))
    return pl.pallas_call(
        _rw_post_kernel, grid=(t // tm,),
        in_specs=[row, row, row, vec, vec, pl.BlockSpec((LANES, LANES), lambda i: (0, 0))],
        out_specs=row, out_shape=jax.ShapeDtypeStruct((t, cw), F32),
        compiler_params=_params("parallel"),
    )(y, bonus, g, ln_g, ln_b, ones)


def _pool_kernel(main_ref, halo_ref, w_ref, scale_ref, o_ref, *, batch, pos0):
    tm = main_ref.shape[0]
    full = jnp.concatenate([main_ref[...], halo_ref[...]], axis=0)
    b = batch
    n = full.shape[0]
    a2 = full[b:] + full[:n - b]
    a4 = a2[2 * b:] + a2[:n - 3 * b]
    a8 = a4[4 * b:] + a4[:n - 7 * b]
    a16 = a8[8 * b:] + a8[:n - 15 * b]
    cur = full[16 * b:16 * b + tm]
    sums = (a2[15 * b:15 * b + tm], a4[13 * b:13 * b + tm], a8[9 * b:9 * b + tm], a16[b:b + tm])
    lane = lax.broadcasted_iota(I32, cur.shape, 1)
    grp = lane // (cur.shape[1] // len(POOL_WINDOWS))
    win_sum = sums[-1]
    width = jnp.full(cur.shape, float(POOL_WINDOWS[-1]), F32)
    for j in range(len(POOL_WINDOWS) - 2, -1, -1):
        win_sum = jnp.where(grp == j, sums[j], win_sum)
        width = jnp.where(grp == j, float(POOL_WINDOWS[j]), width)
    row = lax.broadcasted_iota(I32, cur.shape, 0)
    t_idx = pl.program_id(0) * (tm // b) + row // b
    cnt = jnp.minimum(width, (pos0 + t_idx + 1).astype(F32))
    pooled = win_sum / cnt - cur
    o_ref[...] = _bdot(pooled.astype(BF16), w_ref[...]) * scale_ref[...]


def _pool(fp, t, batch, w_bd, scale, pos0):
    dw = fp.shape[1]
    tm = min(ROW_TILE, t)
    halo = 16 * batch
    return pl.pallas_call(
        functools.partial(_pool_kernel, batch=batch, pos0=pos0),
        grid=(t // tm,),
        in_specs=[pl.BlockSpec((tm, dw), lambda i: (i, 0)),
                  pl.BlockSpec((halo, dw), lambda i: ((i + 1) * (tm // halo), 0)),
                  pl.BlockSpec((dw, dw), lambda i: (0, 0)), pl.BlockSpec((1, dw), lambda i: (0, 0))],
        out_specs=pl.BlockSpec((tm, dw), lambda i: (i, 0)),
        out_shape=jax.ShapeDtypeStruct((t, dw), F32),
        compiler_params=_params("parallel"),
    )(fp, fp, w_bd, scale)


def _to_batch_major(a, seq, batch):
    return a.reshape(seq, batch, a.shape[-1]).transpose(1, 0, 2)


def _to_time_major(a):
    bsz, seq, width = a.shape
    return a.transpose(1, 0, 2).reshape(seq * bsz, width)


def _even_mixer(x, batch, seq, h0, cache_k, cache_v, w, j, bias, alpha, ln_g, ln_b):
    a_width = w["s5_w_glu"].shape[-1]
    w_in = w["ev_w_in"][j]
    qw = a_width
    kvw = (w_in.shape[1] - 2 * a_width) // 2
    u, q, k, v = _matmul([x], [w_in.astype(BF16)], splits=(a_width, qw, kvw, kvw))
    prm = _s5_params(w["s5_a_re"][j], w["s5_a_im"][j], w["s5_log_dt"][j], w["s5_b_re"][j], w["s5_b_im"][j],
                     w["s5_c_re"][j], w["s5_c_im"][j], w["s5_d"][j])
    nt = a_width // LANES
    ns = prm["bre"].shape[-1]
    if h0 is None:
        h0re = jnp.zeros((nt, batch, ns), F32)
        h0im = h0re
    else:
        h0re = h0[..., 0].astype(F32).reshape(batch, nt, ns).transpose(1, 0, 2)
        h0im = h0[..., 1].astype(F32).reshape(batch, nt, ns).transpose(1, 0, 2)
    g, hn_re, hn_im = _s5(u, batch, h0re, h0im, prm)
    groups, n_state = w["s5_a_re"][j].shape
    h_last = jnp.stack([hn_re.transpose(1, 0, 2).reshape(batch, groups, n_state),
                        hn_im.transpose(1, 0, 2).reshape(batch, groups, n_state)], axis=-1)
    a_out = _matmul([g], [w["s5_w_glu"][j].astype(BF16)], epilogue="glu")

    qb, kb, vb = (_to_batch_major(a, seq, batch) for a in (q, k, v))
    sinks = w["swa_sinks"][j].astype(F32)
    if cache_k is None:
        att = _attention(qb, kb, vb, bias, sinks, CHUNK, CHUNK, WINDOW // CHUNK + 1, True)
        new_k, new_v = kb[:, seq - WINDOW:], vb[:, seq - WINDOW:]
    else:
        k_all = jnp.concatenate([cache_k.astype(F32).reshape(batch, WINDOW, kvw), kb], axis=1)
        v_all = jnp.concatenate([cache_v.astype(F32).reshape(batch, WINDOW, kvw), vb], axis=1)
        att = _attention(qb, k_all, v_all, bias, sinks, seq, WINDOW + seq, 1, False)
        new_k, new_v = k_all[:, seq:], v_all[:, seq:]
    att = _to_time_major(att)
    w_out = w["ev_w_out"][j].astype(BF16)
    xn = _matmul([a_out, att], [w_out[:a_width], w_out[a_width:]], epilogue="ln",
                 res=x, g=ln_g, b=ln_b, alpha=alpha)
    kv_shape = (batch, WINDOW, kvw // HEAD_DIM, HEAD_DIM)
    return xn, h_last, new_k.reshape(kv_shape), new_v.reshape(kv_shape)


def _odd_mixer(x, batch, seq, shift0, s0, pool0, pos0, w, j, alpha, ln_g, ln_b):
    t = x.shape[0]
    w_in = w["od_w_in"][j]
    cw = w["rw_w0"].shape[-1]
    sw = w["rw_mu"].shape[-1]
    dw = w_in.shape[1] - sw
    n_heads = cw // HEAD_DIM
    sh, p = _matmul([x], [w_in.astype(BF16)], splits=(sw, dw))
    zeros_half = jnp.zeros((LANES - w["rw_w2"].shape[1], cw), F32)
    head_lane = jnp.arange(LANES) // HEAD_DIM
    ones = (head_lane[:, None] == head_lane[None, :]).astype(BF16)
    prm = dict(
        mu=w["rw_mu"][j].astype(F32).reshape(1, sw), w0=w["rw_w0"][j].astype(F32).reshape(1, cw),
        w2=jnp.concatenate([w["rw_w2"][j].astype(F32), zeros_half], axis=0).astype(BF16),
        a0=w["rw_a0"][j].astype(F32).reshape(1, cw),
        a2=jnp.concatenate([zeros_half, w["rw_a2"][j].astype(F32)], axis=0).astype(BF16),
        g2=w["rw_g2"][j].astype(BF16),
        k_k=w["rw_k_k"][j].astype(F32).reshape(1, cw), k_a=w["rw_k_a"][j].astype(F32).reshape(1, cw),
        r_k=w["rw_r_k"][j].astype(F32).reshape(1, cw), ones=ones)
    if shift0 is None:
        shift0 = jnp.zeros((batch, sw), F32)
    r, dec, k2, v, kk, b, bonus, g = _rw_prep(sh, shift0.astype(F32), batch, prm)

    n_tiles = n_heads // 2
    chains = n_tiles * batch
    if s0 is None:
        s0c = jnp.zeros((2, HEAD_DIM, HEAD_DIM, LANES), F32)
    else:
        s0c = s0.astype(F32).reshape(batch, n_tiles, 2, HEAD_DIM, HEAD_DIM).transpose(2, 4, 3, 1, 0)
        s0c = jnp.pad(s0c.reshape(2, HEAD_DIM, HEAD_DIM, chains), ((0, 0), (0, 0), (0, 0), (0, LANES - chains)))
    y, snc = _rwkv((r, dec, k2, v, kk, b), s0c, batch)
    s_last = snc[..., :chains].reshape(2, HEAD_DIM, HEAD_DIM, n_tiles, batch).transpose(4, 3, 0, 2, 1)
    s_last = s_last.reshape(batch, n_heads, HEAD_DIM, HEAD_DIM)
    c_out = _rw_post(y, bonus, g, w["rw_ln_g"][j].astype(F32).reshape(1, cw),
                     w["rw_ln_b"][j].astype(F32).reshape(1, cw), ones)

    if pool0 is None:
        hist = jnp.zeros((POOL_HIST * batch, dw), F32)
    else:
        hist = _to_time_major(pool0.astype(F32))
    fp = jnp.concatenate([jnp.zeros((batch, dw), F32), hist, p], axis=0)
    n_grp = len(POOL_WINDOWS)
    pg = dw // n_grp
    eye = jnp.eye(n_grp, dtype=F32)
    w_bd = (w["pool_w"][j].astype(F32)[:, :, None, :] * eye[:, None, :, None]).reshape(dw, dw).astype(BF16)
    d_out = _pool(fp, t, batch, w_bd, w["pool_scale"][j].astype(F32).reshape(1, dw), pos0)
    pool_new = _to_batch_major(fp[fp.shape[0] - POOL_HIST * batch:], POOL_HIST, batch)

    w_out = w["od_w_out"][j].astype(BF16)
    xn = _matmul([c_out, d_out], [w_out[:cw], w_out[cw:]], epilogue="ln", res=x, g=ln_g, b=ln_b, alpha=alpha)
    sh_last = sh[t - batch:]
    return xn, sh_last, s_last, pool_new


def _trunk(x3, s5_0, k_0, v_0, rw_0, shift_0, pool_0, pos0, w):
    batch, seq, _ = x3.shape
    depth = w["ln1_g"].shape[0]
    alpha = (2 * depth) ** 0.25
    x = _to_time_major(x3.astype(F32))
    prompt = s5_0 is None
    if prompt:
        bias = _bias_table(w["rel_bias"], CHUNK, (WINDOW // CHUNK + 1) * CHUNK)
    else:
        bias = _bias_table(w["rel_bias"], seq, WINDOW + seq)
    s5_new, k_new, v_new, rw_new, sh_new, pool_new = [], [], [], [], [], []
    for i in range(depth):
        j = i // 2
        if i % 2 == 0:
            x, h1, nk, nv = _even_mixer(
                x, batch, seq, None if prompt else s5_0[j], None if prompt else k_0[j],
                None if prompt else v_0[j], w, j, bias, alpha, w["ln1_g"][i], w["ln1_b"][i])
            s5_new.append(h1)
            k_new.append(nk)
            v_new.append(nv)
        else:
            x, sh1, s1, pl1 = _odd_mixer(
                x, batch, seq, None if prompt else shift_0[j], None if prompt else rw_0[j],
                None if prompt else pool_0[j], pos0, w, j, alpha, w["ln1_g"][i], w["ln1_b"][i])
            sh_new.append(sh1)
            rw_new.append(s1)
            pool_new.append(pl1)
        x = _moe_ln(x, w["moe_w_coarse"][i], w["moe_b_coarse"][i], w["moe_w_fine"][i], w["moe_b_fine"][i],
                    w["moe_w_gate"][i].astype(BF16), w["moe_w_up"][i].astype(BF16),
                    w["moe_w_down"][i].astype(BF16), w["ln2_g"][i], w["ln2_b"][i], alpha)
    y = _to_batch_major(x, seq, batch)
    return (y, jnp.stack(s5_new), jnp.stack(k_new), jnp.stack(v_new),
            jnp.stack(rw_new), jnp.stack(sh_new), jnp.stack(pool_new))


def kernel(x_prompt, x_sample, state_s5, cache_swa_k, cache_swa_v, state_rwkv, state_shift, state_pool, rel_bias, ev_w_in, ev_w_out, s5_a_re, s5_a_im, s5_log_dt, s5_b_re, s5_b_im, s5_c_re, s5_c_im, s5_d, s5_w_glu, swa_sinks, od_w_in, od_w_out, rw_mu, rw_w0, rw_w2, rw_a0, rw_a2, rw_g2, rw_k_k, rw_k_a, rw_r_k, rw_ln_g, rw_ln_b, pool_w, pool_scale, ln1_g, ln1_b, ln2_g, ln2_b, moe_w_coarse, moe_b_coarse, moe_w_fine, moe_b_fine, moe_w_gate, moe_w_up, moe_w_down):
    w = dict(rel_bias=rel_bias, ev_w_in=ev_w_in, ev_w_out=ev_w_out, s5_a_re=s5_a_re, s5_a_im=s5_a_im,
             s5_log_dt=s5_log_dt, s5_b_re=s5_b_re, s5_b_im=s5_b_im, s5_c_re=s5_c_re, s5_c_im=s5_c_im,
             s5_d=s5_d, s5_w_glu=s5_w_glu, swa_sinks=swa_sinks, od_w_in=od_w_in, od_w_out=od_w_out,
             rw_mu=rw_mu, rw_w0=rw_w0, rw_w2=rw_w2, rw_a0=rw_a0, rw_a2=rw_a2, rw_g2=rw_g2, rw_k_k=rw_k_k,
             rw_k_a=rw_k_a, rw_r_k=rw_r_k, rw_ln_g=rw_ln_g, rw_ln_b=rw_ln_b, pool_w=pool_w,
             pool_scale=pool_scale, ln1_g=ln1_g, ln1_b=ln1_b, ln2_g=ln2_g, ln2_b=ln2_b,
             moe_w_coarse=moe_w_coarse, moe_b_coarse=moe_b_coarse, moe_w_fine=moe_w_fine,
             moe_b_fine=moe_b_fine, moe_w_gate=moe_w_gate, moe_w_up=moe_w_up, moe_w_down=moe_w_down)
    y_s, s5_s, k_s, v_s, rw_s, sh_s, pool_s = _trunk(
        x_sample, state_s5, cache_swa_k, cache_swa_v, state_rwkv, state_shift, state_pool, PAST_LEN, w)
    y_p, s5_p, k_p, v_p, rw_p, sh_p, pool_p = _trunk(x_prompt, None, None, None, None, None, None, 0, w)
    return (y_p, y_s, s5_p, s5_s, k_p, v_p, k_s, v_s, rw_p, rw_s, sh_p, sh_s, pool_p, pool_s)
```

```python
import functools
import math

import jax
import jax.numpy as jnp
from jax import lax
from jax.experimental import pallas as pl
from jax.experimental.pallas import tpu as pltpu

F32 = jnp.float32
BF16 = jnp.bfloat16
I32 = jnp.int32

CHUNK = 64
WINDOW = 128
HEAD_DIM = 64
PAST_LEN = 1024
N_BUCKETS = 32
MAX_DISTANCE = 128
POOL_WINDOWS = (2, 4, 8, 16)
POOL_HIST = max(POOL_WINDOWS) - 1
E_GROUPS = 4
LN_EPS = 1e-5
GN_EPS = 64e-5
A_GROUP = 16

LANES = 128
SUBLANES = 8
VMEM_LIMIT_BYTES = 56 * 1024 * 1024

ROW_TILE = 512
MOE_BLOCK = 256


def _params(*sem):
    return pltpu.CompilerParams(dimension_semantics=sem, vmem_limit_bytes=VMEM_LIMIT_BYTES)


def _sigmoid(x):
    return 1.0 / (1.0 + jnp.exp(-x))


def _layer_norm(y, g, b):
    mu = jnp.mean(y, axis=-1, keepdims=True)
    d = y - mu
    var = jnp.mean(d * d, axis=-1, keepdims=True)
    return d * lax.rsqrt(var + LN_EPS) * g + b


def _bdot(a, b):
    return jnp.dot(a, b, preferred_element_type=F32)


def _mm_kernel(*refs, n_in, splits, epilogue, alpha):
    xs = refs[:n_in]
    ws = refs[n_in:2 * n_in]
    pos = 2 * n_in
    acc = None
    for x_ref, w_ref in zip(xs, ws):
        d = _bdot(x_ref[...].astype(BF16), w_ref[...])
        acc = d if acc is None else acc + d
    if epilogue == "ln":
        res_ref, g_ref, b_ref = refs[pos:pos + 3]
        pos += 3
        acc = _layer_norm(alpha * res_ref[...] + acc, g_ref[...], b_ref[...])
    elif epilogue == "glu":
        acc = xs[0][...] * _sigmoid(acc)
    off = 0
    for o_ref, n in zip(refs[pos:], splits):
        o_ref[...] = acc[:, off:off + n]
        off += n


def _matmul(xs, ws, splits=None, epilogue=None, res=None, g=None, b=None, alpha=None):
    t = xs[0].shape[0]
    n = ws[0].shape[1]
    splits = tuple(splits) if splits else (n,)
    tm = min(ROW_TILE, t)
    in_specs = [pl.BlockSpec((tm, x.shape[1]), lambda i: (i, 0)) for x in xs]
    in_specs += [pl.BlockSpec(w.shape, lambda i: (0, 0)) for w in ws]
    args = list(xs) + list(ws)
    if epilogue == "ln":
        in_specs += [pl.BlockSpec((tm, n), lambda i: (i, 0)),
                     pl.BlockSpec((1, n), lambda i: (0, 0)), pl.BlockSpec((1, n), lambda i: (0, 0))]
        args += [res, g.reshape(1, n), b.reshape(1, n)]
    outs = pl.pallas_call(
        functools.partial(_mm_kernel, n_in=len(xs), splits=splits, epilogue=epilogue, alpha=alpha),
        grid=(t // tm,),
        in_specs=in_specs,
        out_specs=[pl.BlockSpec((tm, s), lambda i: (i, 0)) for s in splits],
        out_shape=[jax.ShapeDtypeStruct((t, s), F32) for s in splits],
        compiler_params=_params("parallel"),
        name="matmul_" + (epilogue or "plain"),
    )(*args)
    return outs if len(splits) > 1 else outs[0]


def _s5_kernel(u_ref, bre_ref, bim_ref, cre_ref, cim_ref, are_ref, aim_ref, d_ref, h0re_ref, h0im_ref,
               g_ref, hnre_ref, hnim_ref, hre, him, car_re, car_im, *, batch, steps):
    ti = pl.program_id(1)

    @pl.when(ti == 0)
    def _():
        car_re[...] = h0re_ref[...]
        car_im[...] = h0im_ref[...]

    u = u_ref[...]
    ub = u.astype(BF16)
    hre[...] = _bdot(ub, bre_ref[...])
    him[...] = _bdot(ub, bim_ref[...])
    a_re = jnp.broadcast_to(are_ref[...], car_re.shape)
    a_im = jnp.broadcast_to(aim_ref[...], car_re.shape)

    def step(t, carry):
        h_re, h_im = carry
        r0 = pl.multiple_of(t * batch, batch)
        n_re = a_re * h_re - a_im * h_im + hre[pl.ds(r0, batch), :]
        n_im = a_re * h_im + a_im * h_re + him[pl.ds(r0, batch), :]
        hre[pl.ds(r0, batch), :] = n_re
        him[pl.ds(r0, batch), :] = n_im
        return n_re, n_im

    h_re, h_im = lax.fori_loop(0, steps, step, (car_re[...], car_im[...]))
    car_re[...] = h_re
    car_im[...] = h_im
    hnre_ref[...] = h_re
    hnim_ref[...] = h_im
    y = (_bdot(hre[...].astype(BF16), cre_ref[...]) - _bdot(him[...].astype(BF16), cim_ref[...])
         + d_ref[...] * u)
    cdf = 0.5 * (1.0 + jnp.tanh(math.sqrt(2.0 / math.pi) * (y + 0.044715 * (y * y * y))))
    g_ref[...] = y * cdf


def _s5(u, batch, h0re, h0im, prm):
    t, width = u.shape
    nt = width // LANES
    ns = prm["bre"].shape[-1]
    rows = min(1024, t)
    steps = rows // batch
    tile = lambda j, i: (j, 0, 0)
    return pl.pallas_call(
        functools.partial(_s5_kernel, batch=batch, steps=steps),
        grid=(nt, t // rows),
        in_specs=[pl.BlockSpec((rows, LANES), lambda j, i: (i, j)),
                  pl.BlockSpec((None, LANES, ns), tile), pl.BlockSpec((None, LANES, ns), tile),
                  pl.BlockSpec((None, ns, LANES), tile), pl.BlockSpec((None, ns, LANES), tile),
                  pl.BlockSpec((None, 1, ns), tile), pl.BlockSpec((None, 1, ns), tile),
                  pl.BlockSpec((None, 1, LANES), tile),
                  pl.BlockSpec((None, batch, ns), tile), pl.BlockSpec((None, batch, ns), tile)],
        out_specs=[pl.BlockSpec((rows, LANES), lambda j, i: (i, j)),
                   pl.BlockSpec((None, batch, ns), tile), pl.BlockSpec((None, batch, ns), tile)],
        out_shape=[jax.ShapeDtypeStruct((t, width), F32),
                   jax.ShapeDtypeStruct((nt, batch, ns), F32), jax.ShapeDtypeStruct((nt, batch, ns), F32)],
        scratch_shapes=[pltpu.VMEM((rows, ns), F32), pltpu.VMEM((rows, ns), F32),
                        pltpu.VMEM((batch, ns), F32), pltpu.VMEM((batch, ns), F32)],
        compiler_params=_params("parallel", "arbitrary"),
        name="s5_scan",
    )(u, prm["bre"], prm["bim"], prm["cre"], prm["cim"], prm["are"], prm["aim"], prm["d"], h0re, h0im)


def _s5_params(a_re, a_im, log_dt, b_re, b_im, c_re, c_im, d_skip):
    groups, n_state = a_re.shape
    gpt = LANES // A_GROUP
    nt = groups // gpt
    lam = lax.complex(a_re.astype(F32), a_im.astype(F32))
    dt = jnp.exp(log_dt.astype(F32))[:, None]
    a_bar = jnp.exp(lam * dt)
    b_bar = ((a_bar - 1.0) / lam)[..., None] * lax.complex(b_re.astype(F32), b_im.astype(F32))
    eye = jnp.eye(gpt, dtype=F32)

    def in_proj(m):
        m = m.reshape(nt, gpt, n_state, A_GROUP).transpose(0, 1, 3, 2)
        return (m[:, :, :, None, :] * eye[None, :, None, :, None]).reshape(nt, gpt * A_GROUP, gpt * n_state)

    def out_proj(m):
        m = m.astype(F32).reshape(nt, gpt, A_GROUP, n_state).transpose(0, 1, 3, 2)
        return (m[:, :, :, None, :] * eye[None, :, None, :, None]).reshape(nt, gpt * n_state, gpt * A_GROUP)

    return dict(bre=in_proj(b_bar.real).astype(BF16), bim=in_proj(b_bar.imag).astype(BF16),
                cre=out_proj(c_re).astype(BF16), cim=out_proj(c_im).astype(BF16),
                are=a_bar.real.reshape(nt, 1, gpt * n_state), aim=a_bar.imag.reshape(nt, 1, gpt * n_state),
                d=d_skip.astype(F32).reshape(nt, 1, LANES))


def _attn_kernel(*refs, n_kv, masked, tiles_per_kv):
    q_ref = refs[0]
    k_refs = refs[1:1 + n_kv]
    v_refs = refs[1 + n_kv:1 + 2 * n_kv]
    bias_ref, sink_ref, o_ref = refs[1 + 2 * n_kv:]
    c = pl.program_id(1)
    q = q_ref[...]
    kt = jnp.concatenate([r[...] for r in k_refs], axis=0)
    vt = jnp.concatenate([r[...] for r in v_refs], axis=0)
    cq, nk = q.shape[0], kt.shape[0]
    low_half = lax.broadcasted_iota(I32, kt.shape, 1) < HEAD_DIM
    k_sw = pltpu.roll(kt, HEAD_DIM, 1)
    v_sw = pltpu.roll(vt, HEAD_DIM, 1)

    def placed(t, t_sw, kvh, half):
        src = t if kvh == half else t_sw
        keep = low_half if half == 0 else jnp.logical_not(low_half)
        return jnp.where(keep, src, 0.0).astype(BF16)

    if masked:
        col = lax.broadcasted_iota(I32, (tiles_per_kv * cq, nk), 1)
        valid = (c - (n_kv - 1)) * CHUNK + col >= 0
    scores = []
    for kvh in range(2):
        tiles = [q[:, (kvh * tiles_per_kv + i) * LANES:(kvh * tiles_per_kv + i + 1) * LANES]
                 for i in range(tiles_per_kv)]
        qs = jnp.concatenate(tiles, axis=0).astype(BF16)
        for half in range(2):
            s = lax.dot_general(qs, placed(kt, k_sw, kvh, half), (((1,), (1,)), ((), ())),
                                preferred_element_type=F32)
            s = s * (HEAD_DIM ** -0.5) + bias_ref[2 * kvh + half]
            if masked:
                s = jnp.where(valid, s, -jnp.inf)
            scores.append(s)
    probs = []
    for grp, s in enumerate(scores):
        sink = sink_ref[grp]
        m = jnp.maximum(jnp.max(s, axis=-1, keepdims=True), sink)
        p = jnp.exp(s - m)
        denom = jnp.sum(p, axis=-1, keepdims=True) + jnp.exp(sink - m)
        probs.append((p / denom).astype(BF16))
    for kvh in range(2):
        o = (_bdot(probs[2 * kvh], placed(vt, v_sw, kvh, 0))
             + _bdot(probs[2 * kvh + 1], placed(vt, v_sw, kvh, 1)))
        for i in range(tiles_per_kv):
            tile = kvh * tiles_per_kv + i
            o_ref[:, tile * LANES:(tile + 1) * LANES] = o[i * cq:(i + 1) * cq]


def _attention(q, k, v, bias, sinks, cq, ck, n_kv, masked):
    bsz, lq, qw = q.shape
    kw = k.shape[-1]
    n_heads = qw // HEAD_DIM
    assert kw == LANES and n_heads % 4 == 0, "two kv heads in one lane tile, an even number of query tiles each"
    tiles_per_kv = n_heads // 4
    heads = [[2 * (kvh * tiles_per_kv + i) + half for i in range(tiles_per_kv)]
             for kvh in range(2) for half in range(2)]
    bias_g = jnp.stack([jnp.concatenate([bias[h] for h in hs], axis=0) for hs in heads])
    sink_g = jnp.stack([jnp.concatenate([jnp.ones((cq, 1), F32) * sinks[h] for h in hs], axis=0)
                        for hs in heads])

    def kv_spec(s):
        return pl.BlockSpec((None, ck, kw), lambda b, c: (b, jnp.maximum(c - (n_kv - 1) + s, 0), 0))

    return pl.pallas_call(
        functools.partial(_attn_kernel, n_kv=n_kv, masked=masked, tiles_per_kv=tiles_per_kv),
        grid=(bsz, lq // cq),
        in_specs=[pl.BlockSpec((None, cq, qw), lambda b, c: (b, c, 0))]
        + [kv_spec(s) for s in range(n_kv)] + [kv_spec(s) for s in range(n_kv)]
        + [pl.BlockSpec(bias_g.shape, lambda b, c: (0, 0, 0)), pl.BlockSpec(sink_g.shape, lambda b, c: (0, 0, 0))],
        out_specs=pl.BlockSpec((None, cq, qw), lambda b, c: (b, c, 0)),
        out_shape=jax.ShapeDtypeStruct((bsz, lq, qw), F32),
        compiler_params=_params("parallel", "arbitrary"),
        name="swa_attention",
    )(q, *([k] * n_kv), *([v] * n_kv), bias_g, sink_g)


def _t5_bucket(rel):
    half = N_BUCKETS // 2
    max_exact = half // 2
    n = jnp.abs(rel)
    large = max_exact + (jnp.log(jnp.maximum(n, 1).astype(F32) / max_exact)
                         / math.log(MAX_DISTANCE / max_exact) * (half - max_exact)).astype(I32)
    large = jnp.minimum(large, half - 1)
    return jnp.where(rel > 0, half, 0) + jnp.where(n < max_exact, n, large)


def _bias_table(rel_bias, nq, nk):
    rel = (jnp.arange(nk) - WINDOW)[None, :] - jnp.arange(nq)[:, None]
    return jnp.transpose(rel_bias.astype(F32)[_t5_bucket(rel)], (2, 0, 1))


def _router_kernel(x_ref, w_ref, b_ref, route_ref, cnt_ref, carry, *, n_fine):
    @pl.when(pl.program_id(0) == 0)
    def _():
        carry[...] = jnp.zeros_like(carry)

    logits = _bdot(x_ref[...].astype(BF16), w_ref[...]) + b_ref[...]
    tm = logits.shape[0]
    lane = lax.broadcasted_iota(I32, logits.shape, 1)
    lanef = lane.astype(F32)
    per_group = n_fine // E_GROUPS
    big = float(LANES)

    def first_max(mask):
        mx = jnp.max(jnp.where(mask, logits, -jnp.inf), axis=-1, keepdims=True)
        idx = jnp.min(jnp.where(mask & (logits == mx), lanef, big), axis=-1, keepdims=True)
        return mx, idx

    cmask = lane < E_GROUPS
    mc, grp = first_max(cmask)
    p_grp = 1.0 / jnp.sum(jnp.where(cmask, jnp.exp(logits - mc), 0.0), axis=-1, keepdims=True)
    lo = E_GROUPS + per_group * grp
    fmask = (lanef >= lo) & (lanef < lo + per_group)
    m1, i1 = first_max(fmask)
    sel0 = lanef == i1
    m2, i2 = first_max(fmask & jnp.logical_not(sel0))
    sel1 = lanef == i2
    e = jnp.exp(m2 - m1)
    g0 = (1.0 / (1.0 + e)) * p_grp
    g1 = (e / (1.0 + e)) * p_grp

    cmat = jnp.where(sel0 | sel1, 1.0, 0.0)
    row = lax.broadcasted_iota(I32, (tm, tm), 0)
    col = lax.broadcasted_iota(I32, (tm, tm), 1)
    tril = jnp.where(col < row, 1.0, 0.0).astype(BF16)
    before = carry[...] + _bdot(tril, cmat.astype(BF16))
    r0 = jnp.sum(jnp.where(sel0, before, 0.0), axis=-1, keepdims=True)
    r1 = jnp.sum(jnp.where(sel1, before, 0.0), axis=-1, keepdims=True)
    carry[...] = carry[...] + jnp.sum(cmat, axis=0, keepdims=True)
    cnt_ref[...] = carry[...]
    vals = (i1 - E_GROUPS, i2 - E_GROUPS, r0, r1, g0, g1)
    route = jnp.zeros_like(logits)
    for j, val in enumerate(vals):
        route = jnp.where(lane == j, val, route)
    route_ref[...] = route


def _router(x, w_route, b_route, n_fine):
    t, d = x.shape
    tm = min(ROW_TILE, t)
    return pl.pallas_call(
        functools.partial(_router_kernel, n_fine=n_fine),
        grid=(t // tm,),
        in_specs=[pl.BlockSpec((tm, d), lambda i: (i, 0)), pl.BlockSpec((d, LANES), lambda i: (0, 0)),
                  pl.BlockSpec((1, LANES), lambda i: (0, 0))],
        out_specs=[pl.BlockSpec((tm, LANES), lambda i: (i, 0)), pl.BlockSpec((1, LANES), lambda i: (0, 0))],
        out_shape=[jax.ShapeDtypeStruct((t, LANES), F32), jax.ShapeDtypeStruct((1, LANES), F32)],
        scratch_shapes=[pltpu.VMEM((1, LANES), F32)],
        compiler_params=_params("arbitrary"),
        name="moe_router",
    )(x, w_route, b_route)


def _row_copy(src_ref, src_row, dst_ref, dst_row, sem):
    return pltpu.make_async_copy(src_ref.at[pl.ds(src_row, 1)], dst_ref.at[pl.ds(dst_row, 1)], sem)


def _dispatch_kernel(dest_ref, x_ref, xs_in_ref, xs_ref, sem, *, tm, top_k):
    del xs_in_ref

    def issue(r, carry):
        for s in range(top_k):
            _row_copy(x_ref, r, xs_ref, dest_ref[s, r], sem).start()
        return carry

    def drain(r, carry):
        for s in range(top_k):
            _row_copy(x_ref, r, xs_ref, dest_ref[s, r], sem).wait()
        return carry

    lax.fori_loop(0, tm, issue, 0)
    lax.fori_loop(0, tm, drain, 0)


def _dispatch(dest, x, n_rows):
    t, d = x.shape
    top_k = dest.shape[0]
    tm = min(ROW_TILE, t)
    return pl.pallas_call(
        functools.partial(_dispatch_kernel, tm=tm, top_k=top_k),
        grid=(t // tm,),
        in_specs=[pl.BlockSpec((top_k, tm), lambda i: (0, i), memory_space=pltpu.SMEM),
                  pl.BlockSpec((tm, d), lambda i: (i, 0)),
                  pl.BlockSpec(memory_space=pl.ANY)],
        out_specs=pl.BlockSpec(memory_space=pl.ANY),
        out_shape=jax.ShapeDtypeStruct((n_rows, d), F32),
        scratch_shapes=[pltpu.SemaphoreType.DMA(())],
        input_output_aliases={2: 0},
        compiler_params=_params("arbitrary"),
        name="moe_dispatch",
    )(dest, x, jnp.zeros((n_rows, d), F32))


def _expert_kernel(be_ref, nu_ref, xs_ref, wg_ref, wu_ref, wd_ref, ys_ref):
    del be_ref
    i = pl.program_id(0)

    @pl.when(i < nu_ref[0])
    def _():
        xb = xs_ref[...].astype(BF16)
        a = _bdot(xb, wg_ref[...])
        h = (a * _sigmoid(a)) * _bdot(xb, wu_ref[...])
        ys_ref[...] = _bdot(h.astype(BF16), wd_ref[...])

    @pl.when(i >= nu_ref[0])
    def _():
        ys_ref[...] = jnp.zeros_like(ys_ref)


def _experts(block_e, n_used, xs, wg, wu, wd):
    n_rows, d = xs.shape
    de = wg.shape[-1]
    grid_spec = pltpu.PrefetchScalarGridSpec(
        num_scalar_prefetch=2,
        grid=(n_rows // MOE_BLOCK,),
        in_specs=[pl.BlockSpec((MOE_BLOCK, d), lambda i, be, nu: (i, 0)),
                  pl.BlockSpec((None, d, de), lambda i, be, nu: (be[i], 0, 0)),
                  pl.BlockSpec((None, d, de), lambda i, be, nu: (be[i], 0, 0)),
                  pl.BlockSpec((None, de, d), lambda i, be, nu: (be[i], 0, 0))],
        out_specs=pl.BlockSpec((MOE_BLOCK, d), lambda i, be, nu: (i, 0)),
    )
    return pl.pallas_call(
        _expert_kernel, grid_spec=grid_spec,
        out_shape=jax.ShapeDtypeStruct((n_rows, d), F32),
        compiler_params=_params("arbitrary"),
        name="moe_experts",
    )(block_e, n_used, xs, wg, wu, wd)


def _combine_kernel(dest_ref, x_ref, gate_ref, g_ref, b_ref, ys_ref, o_ref, ybuf, sem, *, tm, top_k, alpha):
    def issue(r, carry):
        for s in range(top_k):
            _row_copy(ys_ref, dest_ref[s, r], ybuf.at[s], r, sem).start()
        return carry

    def drain(r, carry):
        for s in range(top_k):
            _row_copy(ys_ref, dest_ref[s, r], ybuf.at[s], r, sem).wait()
        return carry

    lax.fori_loop(0, tm, issue, 0)
    lax.fori_loop(0, tm, drain, 0)
    gate = gate_ref[...]
    y = alpha * x_ref[...]
    for s in range(top_k):
        y = y + gate[:, s:s + 1] * ybuf[s]
    o_ref[...] = _layer_norm(y, g_ref[...], b_ref[...])


def _combine(dest, x, gates, ys, ln_g, ln_b, alpha):
    t, d = x.shape
    top_k = dest.shape[0]
    tm = min(ROW_TILE, t)
    return pl.pallas_call(
        functools.partial(_combine_kernel, tm=tm, top_k=top_k, alpha=alpha),
        grid=(t // tm,),
        in_specs=[pl.BlockSpec((top_k, tm), lambda i: (0, i), memory_space=pltpu.SMEM),
                  pl.BlockSpec((tm, d), lambda i: (i, 0)),
                  pl.BlockSpec((tm, top_k), lambda i: (i, 0)),
                  pl.BlockSpec((1, d), lambda i: (0, 0)), pl.BlockSpec((1, d), lambda i: (0, 0)),
                  pl.BlockSpec(memory_space=pl.ANY)],
        out_specs=pl.BlockSpec((tm, d), lambda i: (i, 0)),
        out_shape=jax.ShapeDtypeStruct((t, d), F32),
        scratch_shapes=[pltpu.VMEM((top_k, tm, d), F32), pltpu.SemaphoreType.DMA(())],
        compiler_params=_params("arbitrary"),
        name="moe_combine_ln",
    )(dest, x, gates, ln_g.reshape(1, d), ln_b.reshape(1, d), ys)


def _moe_ln(x, w_c, b_c, w_f, b_f, wg, wu, wd, ln_g, ln_b, alpha):
    t, d = x.shape
    n_exp = w_f.shape[1]
    pad = LANES - E_GROUPS - n_exp
    w_route = jnp.concatenate([w_c.astype(F32), w_f.astype(F32), jnp.zeros((d, pad), F32)], axis=1)
    b_route = jnp.concatenate([b_c.astype(F32), b_f.astype(F32), jnp.zeros((pad,), F32)]).reshape(1, LANES)
    route, cnt = _router(x, w_route.astype(BF16), b_route, n_exp)
    experts = route[:, 0:2].astype(I32)
    ranks = route[:, 2:4].astype(I32)
    gates = route[:, 4:6]
    top_k = experts.shape[1]
    counts = cnt[0, E_GROUPS:E_GROUPS + n_exp].astype(I32)
    padded = (counts + MOE_BLOCK - 1) // MOE_BLOCK * MOE_BLOCK
    pend = jnp.cumsum(padded)
    poff = pend - padded
    dest = (poff[experts] + ranks).T
    n_blocks = (t * top_k) // MOE_BLOCK + n_exp
    block_e = jnp.minimum(jnp.searchsorted(pend, jnp.arange(n_blocks, dtype=I32) * MOE_BLOCK, side="right"),
                          n_exp - 1).astype(I32)
    n_used = (pend[-1] // MOE_BLOCK).astype(I32).reshape(1)
    xs = _dispatch(dest, x, n_blocks * MOE_BLOCK)
    ys = _experts(block_e, n_used, xs, wg, wu, wd)
    return _combine(dest, x, gates, ys, ln_g, ln_b, alpha)


def _head_sum(x, ones):
    hi = x.astype(BF16)
    lo = (x - hi.astype(F32)).astype(BF16)
    parts = []
    for j in range(x.shape[1] // LANES):
        sl = slice(j * LANES, (j + 1) * LANES)
        parts.append(_bdot(hi[:, sl], ones) + _bdot(lo[:, sl], ones))
    return jnp.concatenate(parts, axis=-1)


def _rw_prep_kernel(sh_ref, halo_ref, sh0_ref, mu_ref, w0_ref, w2_ref, a0_ref, a2_ref, g2_ref,
                    kk_ref, ka_ref, rk_ref, ones_ref,
                    r_o, w_o, k_o, v_o, kk_o, b_o, bonus_o, g_o, *, batch, cw):
    sh = sh_ref[...]
    first = jnp.where(pl.program_id(0) == 0, sh0_ref[...], halo_ref[...])
    prev = jnp.concatenate([first, sh[:sh.shape[0] - batch]], axis=0)
    shm = sh + (prev - sh) * mu_ref[...]
    r, k, v = shm[:, :cw], shm[:, cw:2 * cw], shm[:, 2 * cw:3 * cw]
    xwa = shm[:, 3 * cw:3 * cw + LANES]
    xg = shm[:, 3 * cw + LANES:]
    ones = ones_ref[...]
    z = w0_ref[...] + _bdot(jnp.tanh(xwa).astype(BF16), w2_ref[...])
    softplus = jnp.maximum(-z, 0.0) + jnp.log1p(jnp.exp(-jnp.abs(z)))
    w_o[...] = jnp.exp(-jnp.exp(-softplus - 0.5))
    a = _sigmoid(a0_ref[...] + _bdot(xwa.astype(BF16), a2_ref[...]))
    g_o[...] = _bdot(_sigmoid(xg).astype(BF16), g2_ref[...])
    kk = k * kk_ref[...]
    kk = kk / jnp.maximum(jnp.sqrt(_head_sum(kk * kk, ones)), 1e-12)
    k2 = k * (1.0 + (a - 1.0) * ka_ref[...])
    r_o[...] = r
    k_o[...] = k2
    v_o[...] = v
    kk_o[...] = kk
    b_o[...] = kk * a
    bonus_o[...] = _head_sum(r * k2 * rk_ref[...], ones) * v


def _rw_prep(sh, shift0, batch, prm):
    t, sw = sh.shape
    cw = prm["w0"].shape[-1]
    tm = min(256, t)
    hb = tm // batch
    row = lambda i: (i, 0)
    fix = lambda i: (0, 0)
    vec = lambda n: pl.BlockSpec((1, n), fix)
    return pl.pallas_call(
        functools.partial(_rw_prep_kernel, batch=batch, cw=cw),
        grid=(t // tm,),
        in_specs=[pl.BlockSpec((tm, sw), row),
                  pl.BlockSpec((batch, sw), lambda i: (jnp.maximum(i * hb - 1, 0), 0)),
                  pl.BlockSpec((batch, sw), fix), vec(sw), vec(cw),
                  pl.BlockSpec((LANES, cw), fix), vec(cw), pl.BlockSpec((LANES, cw), fix),
                  pl.BlockSpec((LANES, cw), fix), vec(cw), vec(cw), vec(cw),
                  pl.BlockSpec((LANES, LANES), fix)],
        out_specs=[pl.BlockSpec((tm, cw), row)] * 8,
        out_shape=[jax.ShapeDtypeStruct((t, cw), F32)] * 8,
        compiler_params=_params("parallel"),
        name="rwkv_prep",
    )(sh, sh, shift0, prm["mu"], prm["w0"], prm["w2"], prm["a0"], prm["a2"], prm["g2"],
      prm["k_k"], prm["k_a"], prm["r_k"], prm["ones"])


RW_TILES_A = 4
RW_TILES_B = 2
RW_UNROLL = 8


def _rwkv_kernel(r_ref, w_ref, k_ref, v_ref, kk_ref, b_ref, sa0_ref, sb0_ref, y_ref, san_ref, sbn_ref,
                 st_a, st_b, xa, xb, *, batch, steps):
    hd = HEAD_DIM
    hv = hd // 2
    ti = pl.program_id(0)

    @pl.when(ti == 0)
    def _():
        st_a[...] = sa0_ref[...]
        st_b[...] = sb0_ref[...]

    operands = (r_ref, w_ref, k_ref, v_ref, kk_ref, b_ref)
    def lane_of(rows):
        return lax.broadcasted_iota(I32, (rows, LANES), 1)

    lane_lo = lane_of(hd) < hd
    lane_lo_out = lane_of(batch) < hd
    even_quarter = (lane_of(hd) // hv) % 2 == 0
    quarter_v = lane_of(hv) // hv
    quarter_out = lane_of(batch) // hv

    def pick_quarter(q, blocks):
        return jnp.where(q == 0, blocks[0], jnp.where(q == 1, blocks[1], jnp.where(q == 2, blocks[2], blocks[3])))

    def padded(parts, rows):
        have = len(parts) * batch
        return parts + ([jnp.zeros((rows - have, LANES), F32)] if have < rows else [])

    def run(state, x, nv):
        def sa_body(kidx, acc):
            return acc + state[kidx] * x[4, pl.ds(kidx, 1), :]

        sa = lax.fori_loop(0, hd, sa_body, jnp.zeros((nv, LANES), F32), unroll=RW_UNROLL)
        vv = x[3, pl.ds(0, nv), :]

        def up_body(kidx, acc):
            row = pl.ds(kidx, 1)
            sn = state[kidx] * x[1, row, :] - sa * x[5, row, :] + vv * x[2, row, :]
            state[kidx] = sn
            return acc + sn * x[0, row, :]

        return lax.fori_loop(0, hd, up_body, jnp.zeros((nv, LANES), F32), unroll=RW_UNROLL)

    def time_step(t, carry):
        r0 = pl.multiple_of(t * batch, batch)
        for o, ref in enumerate(operands):
            slab = ref[pl.ds(r0, batch), :]
            tiles = [slab[:, j * LANES:(j + 1) * LANES] for j in range(RW_TILES_A + RW_TILES_B)]
            ta = jnp.concatenate(padded(tiles[:RW_TILES_A], hd) * 2, axis=0).T
            xa[o] = jnp.where(lane_lo, ta[:hd], ta[hd:])
            tb = jnp.concatenate(padded(tiles[RW_TILES_A:], hv) * 4, axis=0).T
            if o == 3:
                xb[o, pl.ds(0, hv), :] = pick_quarter(quarter_v, (tb[:hv], tb[hd:hd + hv], tb[hv:hd], tb[hd + hv:]))
            else:
                xb[o] = jnp.where(even_quarter, tb[:hd], tb[hd:])
        ya = run(st_a, xa, hd)
        yb = run(st_b, xb, hv)
        mta = jnp.concatenate([ya, ya], axis=0).T
        for j in range(RW_TILES_A):
            lo = mta[j * batch:(j + 1) * batch]
            hi = mta[hd + j * batch:hd + (j + 1) * batch]
            y_ref[pl.ds(r0, batch), j * LANES:(j + 1) * LANES] = jnp.where(lane_lo_out, lo, hi)
        mtb = jnp.concatenate([yb] * 4, axis=0).T
        for j in range(RW_TILES_B):
            blocks = [mtb[base + j * batch:base + (j + 1) * batch] for base in (0, hd, hv, hd + hv)]
            tile = RW_TILES_A + j
            y_ref[pl.ds(r0, batch), tile * LANES:(tile + 1) * LANES] = pick_quarter(quarter_out, blocks)
        return carry

    lax.fori_loop(0, steps, time_step, 0)

    @pl.when(ti == pl.num_programs(0) - 1)
    def _():
        san_ref[...] = st_a[...]
        sbn_ref[...] = st_b[...]


def _rwkv(ops, s0, batch):
    t, cw = ops[0].shape
    hd, hv = HEAD_DIM, HEAD_DIM // 2
    assert cw == (RW_TILES_A + RW_TILES_B) * LANES and RW_TILES_A * batch <= hd
    ca, cb = RW_TILES_A * batch, RW_TILES_B * batch
    if s0 is None:
        sa0 = jnp.zeros((hd, hd, LANES), F32)
        sb0 = jnp.zeros((hd, hv, LANES), F32)
    else:
        s0 = s0.astype(F32)
        sa0 = s0[:, :2 * RW_TILES_A].reshape(batch, RW_TILES_A, 2, hd, hd).transpose(4, 3, 2, 1, 0)
        sa0 = jnp.pad(sa0.reshape(hd, hd, 2, ca), ((0, 0),) * 3 + ((0, hd - ca),)).reshape(hd, hd, LANES)
        sb0 = s0[:, 2 * RW_TILES_A:].reshape(batch, RW_TILES_B, 2, 2, hv, hd).transpose(5, 4, 3, 2, 1, 0)
        sb0 = jnp.pad(sb0.reshape(hd, hv, 2, 2, cb), ((0, 0),) * 4 + ((0, hv - cb),)).reshape(hd, hv, LANES)
    steps = min(16, t // batch)
    rows = steps * batch
    blk = pl.BlockSpec((rows, cw), lambda i: (i, 0))
    st_a = pl.BlockSpec((hd, hd, LANES), lambda i: (0, 0, 0))
    st_b = pl.BlockSpec((hd, hv, LANES), lambda i: (0, 0, 0))
    y, san, sbn = pl.pallas_call(
        functools.partial(_rwkv_kernel, batch=batch, steps=steps),
        grid=(t // rows,),
        in_specs=[blk] * 6 + [st_a, st_b],
        out_specs=[blk, st_a, st_b],
        out_shape=[jax.ShapeDtypeStruct((t, cw), F32), jax.ShapeDtypeStruct((hd, hd, LANES), F32),
                   jax.ShapeDtypeStruct((hd, hv, LANES), F32)],
        scratch_shapes=[pltpu.VMEM((hd, hd, LANES), F32), pltpu.VMEM((hd, hv, LANES), F32),
                        pltpu.VMEM((6, hd, LANES), F32), pltpu.VMEM((6, hd, LANES), F32)],
        compiler_params=_params("arbitrary"),
        name="rwkv_recurrence",
    )(*ops, sa0, sb0)
    sa = san.reshape(hd, hd, 2, hd)[..., :ca].reshape(hd, hd, 2, RW_TILES_A, batch).transpose(4, 3, 2, 1, 0)
    sb = sbn.reshape(hd, hv, 2, 2, hv)[..., :cb].reshape(hd, hv, 2, 2, RW_TILES_B, batch).transpose(5, 4, 3, 2, 1, 0)
    s_last = jnp.concatenate([sa.reshape(batch, 2 * RW_TILES_A, hd, hd), sb.reshape(batch, 2 * RW_TILES_B, hd, hd)],
                             axis=1)
    return y, s_last


def _rw_post_kernel(y_ref, bonus_ref, g_ref, lg_ref, lb_ref, ones_ref, o_ref):
    y = y_ref[...]
    ones = ones_ref[...]
    mu = _head_sum(y, ones) * (1.0 / HEAD_DIM)
    d = y - mu
    var = _head_sum(d * d, ones) * (1.0 / HEAD_DIM)
    yn = d * lax.rsqrt(var + GN_EPS) * lg_ref[...] + lb_ref[...]
    o_ref[...] = (yn + bonus_ref[...]) * g_ref[...]


def _rw_post(y, bonus, g, ln_g, ln_b, ones):
    t, cw = y.shape
    tm = min(ROW_TILE, t)
    row = pl.BlockSpec((tm, cw), lambda i: (i, 0))
    vec = pl.BlockSpec((1, cw), lambda i: (0, 0))
    return pl.pallas_call(
        _rw_post_kernel, grid=(t // tm,),
        in_specs=[row, row, row, vec, vec, pl.BlockSpec((LANES, LANES), lambda i: (0, 0))],
        out_specs=row, out_shape=jax.ShapeDtypeStruct((t, cw), F32),
        compiler_params=_params("parallel"),
        name="rwkv_post",
    )(y, bonus, g, ln_g, ln_b, ones)


def _pool_kernel(main_ref, halo_ref, w_ref, scale_ref, o_ref, *, batch, pos0):
    tm = main_ref.shape[0]
    full = jnp.concatenate([main_ref[...], halo_ref[...]], axis=0)
    b = batch
    n = full.shape[0]
    a2 = full[b:] + full[:n - b]
    a4 = a2[2 * b:] + a2[:n - 3 * b]
    a8 = a4[4 * b:] + a4[:n - 7 * b]
    a16 = a8[8 * b:] + a8[:n - 15 * b]
    cur = full[16 * b:16 * b + tm]
    sums = (a2[15 * b:15 * b + tm], a4[13 * b:13 * b + tm], a8[9 * b:9 * b + tm], a16[b:b + tm])
    lane = lax.broadcasted_iota(I32, cur.shape, 1)
    grp = lane // (cur.shape[1] // len(POOL_WINDOWS))
    win_sum = sums[-1]
    width = jnp.full(cur.shape, float(POOL_WINDOWS[-1]), F32)
    for j in range(len(POOL_WINDOWS) - 2, -1, -1):
        win_sum = jnp.where(grp == j, sums[j], win_sum)
        width = jnp.where(grp == j, float(POOL_WINDOWS[j]), width)
    row = lax.broadcasted_iota(I32, cur.shape, 0)
    t_idx = pl.program_id(0) * (tm // b) + row // b
    cnt = jnp.minimum(width, (pos0 + t_idx + 1).astype(F32))
    pooled = win_sum / cnt - cur
    o_ref[...] = _bdot(pooled.astype(BF16), w_ref[...]) * scale_ref[...]


def _pool(fp, t, batch, w_bd, scale, pos0):
    dw = fp.shape[1]
    tm = min(ROW_TILE, t)
    halo = 16 * batch
    return pl.pallas_call(
        functools.partial(_pool_kernel, batch=batch, pos0=pos0),
        grid=(t // tm,),
        in_specs=[pl.BlockSpec((tm, dw), lambda i: (i, 0)),
                  pl.BlockSpec((halo, dw), lambda i: ((i + 1) * (tm // halo), 0)),
                  pl.BlockSpec((dw, dw), lambda i: (0, 0)), pl.BlockSpec((1, dw), lambda i: (0, 0))],
        out_specs=pl.BlockSpec((tm, dw), lambda i: (i, 0)),
        out_shape=jax.ShapeDtypeStruct((t, dw), F32),
        compiler_params=_params("parallel"),
        name="causal_pool",
    )(fp, fp, w_bd, scale)


def _to_batch_major(a, seq, batch):
    return a.reshape(seq, batch, a.shape[-1]).transpose(1, 0, 2)


def _to_time_major(a):
    bsz, seq, width = a.shape
    return a.transpose(1, 0, 2).reshape(seq * bsz, width)


def _even_mixer(x, batch, seq, h0, cache_k, cache_v, w, j, bias, alpha, ln_g, ln_b):
    a_width = w["s5_w_glu"].shape[-1]
    w_in = w["ev_w_in"][j]
    qw = a_width
    kvw = (w_in.shape[1] - 2 * a_width) // 2
    u, q, k, v = _matmul([x], [w_in.astype(BF16)], splits=(a_width, qw, kvw, kvw))
    prm = _s5_params(w["s5_a_re"][j], w["s5_a_im"][j], w["s5_log_dt"][j], w["s5_b_re"][j], w["s5_b_im"][j],
                     w["s5_c_re"][j], w["s5_c_im"][j], w["s5_d"][j])
    nt = a_width // LANES
    ns = prm["bre"].shape[-1]
    if h0 is None:
        h0re = jnp.zeros((nt, batch, ns), F32)
        h0im = h0re
    else:
        h0re = h0[..., 0].astype(F32).reshape(batch, nt, ns).transpose(1, 0, 2)
        h0im = h0[..., 1].astype(F32).reshape(batch, nt, ns).transpose(1, 0, 2)
    g, hn_re, hn_im = _s5(u, batch, h0re, h0im, prm)
    groups, n_state = w["s5_a_re"][j].shape
    h_last = jnp.stack([hn_re.transpose(1, 0, 2).reshape(batch, groups, n_state),
                        hn_im.transpose(1, 0, 2).reshape(batch, groups, n_state)], axis=-1)
    a_out = _matmul([g], [w["s5_w_glu"][j].astype(BF16)], epilogue="glu")

    qb, kb, vb = (_to_batch_major(a, seq, batch) for a in (q, k, v))
    sinks = w["swa_sinks"][j].astype(F32)
    if cache_k is None:
        att = _attention(qb, kb, vb, bias, sinks, CHUNK, CHUNK, WINDOW // CHUNK + 1, True)
        new_k, new_v = kb[:, seq - WINDOW:], vb[:, seq - WINDOW:]
    else:
        k_all = jnp.concatenate([cache_k.astype(F32).reshape(batch, WINDOW, kvw), kb], axis=1)
        v_all = jnp.concatenate([cache_v.astype(F32).reshape(batch, WINDOW, kvw), vb], axis=1)
        att = _attention(qb, k_all, v_all, bias, sinks, seq, WINDOW + seq, 1, False)
        new_k, new_v = k_all[:, seq:], v_all[:, seq:]
    att = _to_time_major(att)
    w_out = w["ev_w_out"][j].astype(BF16)
    xn = _matmul([a_out, att], [w_out[:a_width], w_out[a_width:]], epilogue="ln",
                 res=x, g=ln_g, b=ln_b, alpha=alpha)
    kv_shape = (batch, WINDOW, kvw // HEAD_DIM, HEAD_DIM)
    return xn, h_last, new_k.reshape(kv_shape), new_v.reshape(kv_shape)


def _odd_mixer(x, batch, seq, shift0, s0, pool0, pos0, w, j, alpha, ln_g, ln_b):
    t = x.shape[0]
    w_in = w["od_w_in"][j]
    cw = w["rw_w0"].shape[-1]
    sw = w["rw_mu"].shape[-1]
    dw = w_in.shape[1] - sw
    n_heads = cw // HEAD_DIM
    sh, p = _matmul([x], [w_in.astype(BF16)], splits=(sw, dw))
    zeros_half = jnp.zeros((LANES - w["rw_w2"].shape[1], cw), F32)
    head_lane = jnp.arange(LANES) // HEAD_DIM
    ones = (head_lane[:, None] == head_lane[None, :]).astype(BF16)
    prm = dict(
        mu=w["rw_mu"][j].astype(F32).reshape(1, sw), w0=w["rw_w0"][j].astype(F32).reshape(1, cw),
        w2=jnp.concatenate([w["rw_w2"][j].astype(F32), zeros_half], axis=0).astype(BF16),
        a0=w["rw_a0"][j].astype(F32).reshape(1, cw),
        a2=jnp.concatenate([zeros_half, w["rw_a2"][j].astype(F32)], axis=0).astype(BF16),
        g2=w["rw_g2"][j].astype(BF16),
        k_k=w["rw_k_k"][j].astype(F32).reshape(1, cw), k_a=w["rw_k_a"][j].astype(F32).reshape(1, cw),
        r_k=w["rw_r_k"][j].astype(F32).reshape(1, cw), ones=ones)
    if shift0 is None:
        shift0 = jnp.zeros((batch, sw), F32)
    r, dec, k2, v, kk, b, bonus, g = _rw_prep(sh, shift0.astype(F32), batch, prm)

    y, s_last = _rwkv((r, dec, k2, v, kk, b), s0, batch)
    c_out = _rw_post(y, bonus, g, w["rw_ln_g"][j].astype(F32).reshape(1, cw),
                     w["rw_ln_b"][j].astype(F32).reshape(1, cw), ones)

    if pool0 is None:
        hist = jnp.zeros((POOL_HIST * batch, dw), F32)
    else:
        hist = _to_time_major(pool0.astype(F32))
    fp = jnp.concatenate([jnp.zeros((batch, dw), F32), hist, p], axis=0)
    n_grp = len(POOL_WINDOWS)
    pg = dw // n_grp
    eye = jnp.eye(n_grp, dtype=F32)
    w_bd = (w["pool_w"][j].astype(F32)[:, :, None, :] * eye[:, None, :, None]).reshape(dw, dw).astype(BF16)
    d_out = _pool(fp, t, batch, w_bd, w["pool_scale"][j].astype(F32).reshape(1, dw), pos0)
    pool_new = _to_batch_major(fp[fp.shape[0] - POOL_HIST * batch:], POOL_HIST, batch)

    w_out = w["od_w_out"][j].astype(BF16)
    xn = _matmul([c_out, d_out], [w_out[:cw], w_out[cw:]], epilogue="ln", res=x, g=ln_g, b=ln_b, alpha=alpha)
    sh_last = sh[t - batch:]
    return xn, sh_last, s_last, pool_new


def _trunk(x3, s5_0, k_0, v_0, rw_0, shift_0, pool_0, pos0, w):
    batch, seq, _ = x3.shape
    depth = w["ln1_g"].shape[0]
    alpha = (2 * depth) ** 0.25
    x = _to_time_major(x3.astype(F32))
    prompt = s5_0 is None
    if prompt:
        bias = _bias_table(w["rel_bias"], CHUNK, (WINDOW // CHUNK + 1) * CHUNK)
    else:
        bias = _bias_table(w["rel_bias"], seq, WINDOW + seq)
    s5_new, k_new, v_new, rw_new, sh_new, pool_new = [], [], [], [], [], []
    for i in range(depth):
        j = i // 2
        if i % 2 == 0:
            x, h1, nk, nv = _even_mixer(
                x, batch, seq, None if prompt else s5_0[j], None if prompt else k_0[j],
                None if prompt else v_0[j], w, j, bias, alpha, w["ln1_g"][i], w["ln1_b"][i])
            s5_new.append(h1)
            k_new.append(nk)
            v_new.append(nv)
        else:
            x, sh1, s1, pl1 = _odd_mixer(
                x, batch, seq, None if prompt else shift_0[j], None if prompt else rw_0[j],
                None if prompt else pool_0[j], pos0, w, j, alpha, w["ln1_g"][i], w["ln1_b"][i])
            sh_new.append(sh1)
            rw_new.append(s1)
            pool_new.append(pl1)
        x = _moe_ln(x, w["moe_w_coarse"][i], w["moe_b_coarse"][i], w["moe_w_fine"][i], w["moe_b_fine"][i],
                    w["moe_w_gate"][i].astype(BF16), w["moe_w_up"][i].astype(BF16),
                    w["moe_w_down"][i].astype(BF16), w["ln2_g"][i], w["ln2_b"][i], alpha)
    y = _to_batch_major(x, seq, batch)
    return (y, jnp.stack(s5_new), jnp.stack(k_new), jnp.stack(v_new),
            jnp.stack(rw_new), jnp.stack(sh_new), jnp.stack(pool_new))


def kernel(x_prompt, x_sample, state_s5, cache_swa_k, cache_swa_v, state_rwkv, state_shift, state_pool, rel_bias, ev_w_in, ev_w_out, s5_a_re, s5_a_im, s5_log_dt, s5_b_re, s5_b_im, s5_c_re, s5_c_im, s5_d, s5_w_glu, swa_sinks, od_w_in, od_w_out, rw_mu, rw_w0, rw_w2, rw_a0, rw_a2, rw_g2, rw_k_k, rw_k_a, rw_r_k, rw_ln_g, rw_ln_b, pool_w, pool_scale, ln1_g, ln1_b, ln2_g, ln2_b, moe_w_coarse, moe_b_coarse, moe_w_fine, moe_b_fine, moe_w_gate, moe_w_up, moe_w_down):
    w = dict(rel_bias=rel_bias, ev_w_in=ev_w_in, ev_w_out=ev_w_out, s5_a_re=s5_a_re, s5_a_im=s5_a_im,
             s5_log_dt=s5_log_dt, s5_b_re=s5_b_re, s5_b_im=s5_b_im, s5_c_re=s5_c_re, s5_c_im=s5_c_im,
             s5_d=s5_d, s5_w_glu=s5_w_glu, swa_sinks=swa_sinks, od_w_in=od_w_in, od_w_out=od_w_out,
             rw_mu=rw_mu, rw_w0=rw_w0, rw_w2=rw_w2, rw_a0=rw_a0, rw_a2=rw_a2, rw_g2=rw_g2, rw_k_k=rw_k_k,
             rw_k_a=rw_k_a, rw_r_k=rw_r_k, rw_ln_g=rw_ln_g, rw_ln_b=rw_ln_b, pool_w=pool_w,
             pool_scale=pool_scale, ln1_g=ln1_g, ln1_b=ln1_b, ln2_g=ln2_g, ln2_b=ln2_b,
             moe_w_coarse=moe_w_coarse, moe_b_coarse=moe_b_coarse, moe_w_fine=moe_w_fine,
             moe_b_fine=moe_b_fine, moe_w_gate=moe_w_gate, moe_w_up=moe_w_up, moe_w_down=moe_w_down)
    y_s, s5_s, k_s, v_s, rw_s, sh_s, pool_s = _trunk(
        x_sample, state_s5, cache_swa_k, cache_swa_v, state_rwkv, state_shift, state_pool, PAST_LEN, w)
    y_p, s5_p, k_p, v_p, rw_p, sh_p, pool_p = _trunk(x_prompt, None, None, None, None, None, None, 0, w)
    return (y_p, y_s, s5_p, s5_s, k_p, v_p, k_s, v_s, rw_p, rw_s, sh_p, sh_s, pool_p, pool_s)
```

```python
import functools
import math

import jax
import jax.numpy as jnp
from jax import lax
from jax.experimental import pallas as pl
from jax.experimental.pallas import tpu as pltpu

F32 = jnp.float32
BF16 = jnp.bfloat16
I32 = jnp.int32

CHUNK = 64
WINDOW = 128
HEAD_DIM = 64
PAST_LEN = 1024
N_BUCKETS = 32
MAX_DISTANCE = 128
POOL_WINDOWS = (2, 4, 8, 16)
POOL_HIST = max(POOL_WINDOWS) - 1
E_GROUPS = 4
LN_EPS = 1e-5
GN_EPS = 64e-5
A_GROUP = 16

LANES = 128
SUBLANES = 8
VMEM_LIMIT_BYTES = 56 * 1024 * 1024

ROW_TILE = 512
MOE_BLOCK = 256


def _params(*sem):
    return pltpu.CompilerParams(dimension_semantics=sem, vmem_limit_bytes=VMEM_LIMIT_BYTES)


def _sigmoid(x):
    return 1.0 / (1.0 + jnp.exp(-x))


def _layer_norm(y, g, b):
    mu = jnp.mean(y, axis=-1, keepdims=True)
    d = y - mu
    var = jnp.mean(d * d, axis=-1, keepdims=True)
    return d * lax.rsqrt(var + LN_EPS) * g + b


def _bdot(a, b):
    return jnp.dot(a, b, preferred_element_type=F32)


def _mm_kernel(*refs, n_in, splits, epilogue, alpha):
    xs = refs[:n_in]
    ws = refs[n_in:2 * n_in]
    pos = 2 * n_in
    acc = None
    for x_ref, w_ref in zip(xs, ws):
        d = _bdot(x_ref[...].astype(BF16), w_ref[...])
        acc = d if acc is None else acc + d
    if epilogue == "ln":
        res_ref, g_ref, b_ref = refs[pos:pos + 3]
        pos += 3
        acc = _layer_norm(alpha * res_ref[...] + acc, g_ref[...], b_ref[...])
    elif epilogue == "glu":
        acc = xs[0][...] * _sigmoid(acc)
    off = 0
    for o_ref, n in zip(refs[pos:], splits):
        o_ref[...] = acc[:, off:off + n]
        off += n


def _matmul(xs, ws, splits=None, epilogue=None, res=None, g=None, b=None, alpha=None):
    t = xs[0].shape[0]
    n = ws[0].shape[1]
    splits = tuple(splits) if splits else (n,)
    tm = min(ROW_TILE, t)
    in_specs = [pl.BlockSpec((tm, x.shape[1]), lambda i: (i, 0)) for x in xs]
    in_specs += [pl.BlockSpec(w.shape, lambda i: (0, 0)) for w in ws]
    args = list(xs) + list(ws)
    if epilogue == "ln":
        in_specs += [pl.BlockSpec((tm, n), lambda i: (i, 0)),
                     pl.BlockSpec((1, n), lambda i: (0, 0)), pl.BlockSpec((1, n), lambda i: (0, 0))]
        args += [res, g.reshape(1, n), b.reshape(1, n)]
    outs = pl.pallas_call(
        functools.partial(_mm_kernel, n_in=len(xs), splits=splits, epilogue=epilogue, alpha=alpha),
        grid=(t // tm,),
        in_specs=in_specs,
        out_specs=[pl.BlockSpec((tm, s), lambda i: (i, 0)) for s in splits],
        out_shape=[jax.ShapeDtypeStruct((t, s), F32) for s in splits],
        compiler_params=_params("parallel"),
        name="matmul_" + (epilogue or "plain"),
    )(*args)
    return outs if len(splits) > 1 else outs[0]


def _s5_kernel(u_ref, bre_ref, bim_ref, cre_ref, cim_ref, are_ref, aim_ref, d_ref, h0re_ref, h0im_ref,
               g_ref, hnre_ref, hnim_ref, hre, him, car_re, car_im, *, batch, steps):
    ti = pl.program_id(1)

    @pl.when(ti == 0)
    def _():
        car_re[...] = h0re_ref[...]
        car_im[...] = h0im_ref[...]

    u = u_ref[...]
    ub = u.astype(BF16)
    hre[...] = _bdot(ub, bre_ref[...])
    him[...] = _bdot(ub, bim_ref[...])
    a_re = jnp.broadcast_to(are_ref[...], car_re.shape)
    a_im = jnp.broadcast_to(aim_ref[...], car_re.shape)

    def step(t, carry):
        h_re, h_im = carry
        r0 = pl.multiple_of(t * batch, batch)
        n_re = a_re * h_re - a_im * h_im + hre[pl.ds(r0, batch), :]
        n_im = a_re * h_im + a_im * h_re + him[pl.ds(r0, batch), :]
        hre[pl.ds(r0, batch), :] = n_re
        him[pl.ds(r0, batch), :] = n_im
        return n_re, n_im

    h_re, h_im = lax.fori_loop(0, steps, step, (car_re[...], car_im[...]))
    car_re[...] = h_re
    car_im[...] = h_im
    hnre_ref[...] = h_re
    hnim_ref[...] = h_im
    y = (_bdot(hre[...].astype(BF16), cre_ref[...]) - _bdot(him[...].astype(BF16), cim_ref[...])
         + d_ref[...] * u)
    cdf = 0.5 * (1.0 + jnp.tanh(math.sqrt(2.0 / math.pi) * (y + 0.044715 * (y * y * y))))
    g_ref[...] = y * cdf


def _s5(u, batch, h0re, h0im, prm):
    t, width = u.shape
    nt = width // LANES
    ns = prm["bre"].shape[-1]
    rows = min(1024, t)
    steps = rows // batch
    tile = lambda j, i: (j, 0, 0)
    return pl.pallas_call(
        functools.partial(_s5_kernel, batch=batch, steps=steps),
        grid=(nt, t // rows),
        in_specs=[pl.BlockSpec((rows, LANES), lambda j, i: (i, j)),
                  pl.BlockSpec((None, LANES, ns), tile), pl.BlockSpec((None, LANES, ns), tile),
                  pl.BlockSpec((None, ns, LANES), tile), pl.BlockSpec((None, ns, LANES), tile),
                  pl.BlockSpec((None, 1, ns), tile), pl.BlockSpec((None, 1, ns), tile),
                  pl.BlockSpec((None, 1, LANES), tile),
                  pl.BlockSpec((None, batch, ns), tile), pl.BlockSpec((None, batch, ns), tile)],
        out_specs=[pl.BlockSpec((rows, LANES), lambda j, i: (i, j)),
                   pl.BlockSpec((None, batch, ns), tile), pl.BlockSpec((None, batch, ns), tile)],
        out_shape=[jax.ShapeDtypeStruct((t, width), F32),
                   jax.ShapeDtypeStruct((nt, batch, ns), F32), jax.ShapeDtypeStruct((nt, batch, ns), F32)],
        scratch_shapes=[pltpu.VMEM((rows, ns), F32), pltpu.VMEM((rows, ns), F32),
                        pltpu.VMEM((batch, ns), F32), pltpu.VMEM((batch, ns), F32)],
        compiler_params=_params("parallel", "arbitrary"),
        name="s5_scan",
    )(u, prm["bre"], prm["bim"], prm["cre"], prm["cim"], prm["are"], prm["aim"], prm["d"], h0re, h0im)


def _s5_params(a_re, a_im, log_dt, b_re, b_im, c_re, c_im, d_skip):
    groups, n_state = a_re.shape
    gpt = LANES // A_GROUP
    nt = groups // gpt
    lam = lax.complex(a_re.astype(F32), a_im.astype(F32))
    dt = jnp.exp(log_dt.astype(F32))[:, None]
    a_bar = jnp.exp(lam * dt)
    b_bar = ((a_bar - 1.0) / lam)[..., None] * lax.complex(b_re.astype(F32), b_im.astype(F32))
    eye = jnp.eye(gpt, dtype=F32)

    def in_proj(m):
        m = m.reshape(nt, gpt, n_state, A_GROUP).transpose(0, 1, 3, 2)
        return (m[:, :, :, None, :] * eye[None, :, None, :, None]).reshape(nt, gpt * A_GROUP, gpt * n_state)

    def out_proj(m):
        m = m.astype(F32).reshape(nt, gpt, A_GROUP, n_state).transpose(0, 1, 3, 2)
        return (m[:, :, :, None, :] * eye[None, :, None, :, None]).reshape(nt, gpt * n_state, gpt * A_GROUP)

    return dict(bre=in_proj(b_bar.real).astype(BF16), bim=in_proj(b_bar.imag).astype(BF16),
                cre=out_proj(c_re).astype(BF16), cim=out_proj(c_im).astype(BF16),
                are=a_bar.real.reshape(nt, 1, gpt * n_state), aim=a_bar.imag.reshape(nt, 1, gpt * n_state),
                d=d_skip.astype(F32).reshape(nt, 1, LANES))


def _attn_kernel(*refs, n_kv, masked, tiles_per_kv):
    q_ref = refs[0]
    k_refs = refs[1:1 + n_kv]
    v_refs = refs[1 + n_kv:1 + 2 * n_kv]
    bias_ref, sink_ref, o_ref = refs[1 + 2 * n_kv:]
    c = pl.program_id(1)
    q = q_ref[...]
    kt = jnp.concatenate([r[...] for r in k_refs], axis=0)
    vt = jnp.concatenate([r[...] for r in v_refs], axis=0)
    cq, nk = q.shape[0], kt.shape[0]
    low_half = lax.broadcasted_iota(I32, kt.shape, 1) < HEAD_DIM
    k_sw = pltpu.roll(kt, HEAD_DIM, 1)
    v_sw = pltpu.roll(vt, HEAD_DIM, 1)

    def placed(t, t_sw, kvh, half):
        src = t if kvh == half else t_sw
        keep = low_half if half == 0 else jnp.logical_not(low_half)
        return jnp.where(keep, src, 0.0).astype(BF16)

    if masked:
        col = lax.broadcasted_iota(I32, (tiles_per_kv * cq, nk), 1)
        valid = (c - (n_kv - 1)) * CHUNK + col >= 0
    scores = []
    for kvh in range(2):
        tiles = [q[:, (kvh * tiles_per_kv + i) * LANES:(kvh * tiles_per_kv + i + 1) * LANES]
                 for i in range(tiles_per_kv)]
        qs = jnp.concatenate(tiles, axis=0).astype(BF16)
        for half in range(2):
            s = lax.dot_general(qs, placed(kt, k_sw, kvh, half), (((1,), (1,)), ((), ())),
                                preferred_element_type=F32)
            s = s * (HEAD_DIM ** -0.5) + bias_ref[2 * kvh + half]
            if masked:
                s = jnp.where(valid, s, -jnp.inf)
            scores.append(s)
    probs = []
    for grp, s in enumerate(scores):
        sink = sink_ref[grp]
        m = jnp.maximum(jnp.max(s, axis=-1, keepdims=True), sink)
        p = jnp.exp(s - m)
        denom = jnp.sum(p, axis=-1, keepdims=True) + jnp.exp(sink - m)
        probs.append((p / denom).astype(BF16))
    for kvh in range(2):
        o = (_bdot(probs[2 * kvh], placed(vt, v_sw, kvh, 0))
             + _bdot(probs[2 * kvh + 1], placed(vt, v_sw, kvh, 1)))
        for i in range(tiles_per_kv):
            tile = kvh * tiles_per_kv + i
            o_ref[:, tile * LANES:(tile + 1) * LANES] = o[i * cq:(i + 1) * cq]


def _attention(q, k, v, bias, sinks, cq, ck, n_kv, masked):
    bsz, lq, qw = q.shape
    kw = k.shape[-1]
    n_heads = qw // HEAD_DIM
    assert kw == LANES and n_heads % 4 == 0, "two kv heads in one lane tile, an even number of query tiles each"
    tiles_per_kv = n_heads // 4
    heads = [[2 * (kvh * tiles_per_kv + i) + half for i in range(tiles_per_kv)]
             for kvh in range(2) for half in range(2)]
    bias_g = jnp.stack([jnp.concatenate([bias[h] for h in hs], axis=0) for hs in heads])
    sink_g = jnp.stack([jnp.concatenate([jnp.ones((cq, 1), F32) * sinks[h] for h in hs], axis=0)
                        for hs in heads])

    def kv_spec(s):
        return pl.BlockSpec((None, ck, kw), lambda b, c: (b, jnp.maximum(c - (n_kv - 1) + s, 0), 0))

    return pl.pallas_call(
        functools.partial(_attn_kernel, n_kv=n_kv, masked=masked, tiles_per_kv=tiles_per_kv),
        grid=(bsz, lq // cq),
        in_specs=[pl.BlockSpec((None, cq, qw), lambda b, c: (b, c, 0))]
        + [kv_spec(s) for s in range(n_kv)] + [kv_spec(s) for s in range(n_kv)]
        + [pl.BlockSpec(bias_g.shape, lambda b, c: (0, 0, 0)), pl.BlockSpec(sink_g.shape, lambda b, c: (0, 0, 0))],
        out_specs=pl.BlockSpec((None, cq, qw), lambda b, c: (b, c, 0)),
        out_shape=jax.ShapeDtypeStruct((bsz, lq, qw), F32),
        compiler_params=_params("parallel", "arbitrary"),
        name="swa_attention",
    )(q, *([k] * n_kv), *([v] * n_kv), bias_g, sink_g)


def _t5_bucket(rel):
    half = N_BUCKETS // 2
    max_exact = half // 2
    n = jnp.abs(rel)
    large = max_exact + (jnp.log(jnp.maximum(n, 1).astype(F32) / max_exact)
                         / math.log(MAX_DISTANCE / max_exact) * (half - max_exact)).astype(I32)
    large = jnp.minimum(large, half - 1)
    return jnp.where(rel > 0, half, 0) + jnp.where(n < max_exact, n, large)


def _bias_table(rel_bias, nq, nk):
    rel = (jnp.arange(nk) - WINDOW)[None, :] - jnp.arange(nq)[:, None]
    return jnp.transpose(rel_bias.astype(F32)[_t5_bucket(rel)], (2, 0, 1))


def _router_kernel(x_ref, w_ref, b_ref, route_ref, cnt_ref, carry, *, n_fine):
    @pl.when(pl.program_id(0) == 0)
    def _():
        carry[...] = jnp.zeros_like(carry)

    logits = _bdot(x_ref[...].astype(BF16), w_ref[...]) + b_ref[...]
    tm = logits.shape[0]
    lane = lax.broadcasted_iota(I32, logits.shape, 1)
    lanef = lane.astype(F32)
    per_group = n_fine // E_GROUPS
    big = float(LANES)

    def first_max(mask):
        mx = jnp.max(jnp.where(mask, logits, -jnp.inf), axis=-1, keepdims=True)
        idx = jnp.min(jnp.where(mask & (logits == mx), lanef, big), axis=-1, keepdims=True)
        return mx, idx

    cmask = lane < E_GROUPS
    mc, grp = first_max(cmask)
    p_grp = 1.0 / jnp.sum(jnp.where(cmask, jnp.exp(logits - mc), 0.0), axis=-1, keepdims=True)
    lo = E_GROUPS + per_group * grp
    fmask = (lanef >= lo) & (lanef < lo + per_group)
    m1, i1 = first_max(fmask)
    sel0 = lanef == i1
    m2, i2 = first_max(fmask & jnp.logical_not(sel0))
    e = jnp.exp(m2 - m1)
    g0 = (1.0 / (1.0 + e)) * p_grp
    g1 = (e / (1.0 + e)) * p_grp

    j1 = i1 - lo
    j2 = i2 - lo
    swap = j1 > j2
    j_lo = jnp.minimum(j1, j2)
    j_hi = jnp.maximum(j1, j2)
    n_pairs_group = per_group * (per_group - 1) // 2
    bucket = grp * n_pairs_group + j_lo * (2 * per_group - 1 - j_lo) * 0.5 + (j_hi - j_lo - 1.0)
    g_lo = jnp.where(swap, g1, g0)
    g_hi = jnp.where(swap, g0, g1)

    sel = lanef == bucket
    cmat = jnp.where(sel, 1.0, 0.0)
    row = lax.broadcasted_iota(I32, (tm, tm), 0)
    col = lax.broadcasted_iota(I32, (tm, tm), 1)
    tril = jnp.where(col < row, 1.0, 0.0).astype(BF16)
    before = carry[...] + _bdot(tril, cmat.astype(BF16))
    rank = jnp.sum(jnp.where(sel, before, 0.0), axis=-1, keepdims=True)
    carry[...] = carry[...] + jnp.sum(cmat, axis=0, keepdims=True)
    cnt_ref[...] = carry[...]
    vals = (bucket, rank, g_lo, g_hi)
    route = jnp.zeros_like(logits)
    for j, val in enumerate(vals):
        route = jnp.where(lane == j, val, route)
    route_ref[...] = route


def _router(x, w_route, b_route, n_fine):
    t, d = x.shape
    tm = min(ROW_TILE, t)
    return pl.pallas_call(
        functools.partial(_router_kernel, n_fine=n_fine),
        grid=(t // tm,),
        in_specs=[pl.BlockSpec((tm, d), lambda i: (i, 0)), pl.BlockSpec((d, LANES), lambda i: (0, 0)),
                  pl.BlockSpec((1, LANES), lambda i: (0, 0))],
        out_specs=[pl.BlockSpec((tm, LANES), lambda i: (i, 0)), pl.BlockSpec((1, LANES), lambda i: (0, 0))],
        out_shape=[jax.ShapeDtypeStruct((t, LANES), F32), jax.ShapeDtypeStruct((1, LANES), F32)],
        scratch_shapes=[pltpu.VMEM((1, LANES), F32)],
        compiler_params=_params("arbitrary"),
        name="moe_router",
    )(x, w_route, b_route)


def _row_copy(src_ref, src_row, dst_ref, dst_row, sem):
    return pltpu.make_async_copy(src_ref.at[pl.ds(src_row, 1)], dst_ref.at[pl.ds(dst_row, 1)], sem)


def _dispatch_kernel(dest_ref, x_ref, xs_in_ref, xs_ref, sem, *, tm):
    del xs_in_ref

    def issue(r, carry):
        _row_copy(x_ref, r, xs_ref, dest_ref[0, r], sem).start()
        return carry

    def drain(r, carry):
        _row_copy(x_ref, r, xs_ref, dest_ref[0, r], sem).wait()
        return carry

    lax.fori_loop(0, tm, issue, 0, unroll=8)
    lax.fori_loop(0, tm, drain, 0, unroll=8)


def _dispatch(dest, x, n_rows):
    t, d = x.shape
    tm = min(ROW_TILE, t)
    return pl.pallas_call(
        functools.partial(_dispatch_kernel, tm=tm),
        grid=(t // tm,),
        in_specs=[pl.BlockSpec((1, tm), lambda i: (0, i), memory_space=pltpu.SMEM),
                  pl.BlockSpec((tm, d), lambda i: (i, 0)),
                  pl.BlockSpec(memory_space=pl.ANY)],
        out_specs=pl.BlockSpec(memory_space=pl.ANY),
        out_shape=jax.ShapeDtypeStruct((n_rows, d), F32),
        scratch_shapes=[pltpu.SemaphoreType.DMA(())],
        input_output_aliases={2: 0},
        compiler_params=_params("arbitrary"),
        name="moe_dispatch",
    )(dest, x, jnp.zeros((n_rows, d), F32))


def _expert_kernel(ea_ref, eb_ref, nu_ref, xs_ref, wga_ref, wua_ref, wda_ref, wgb_ref, wub_ref, wdb_ref, ys_ref):
    del ea_ref, eb_ref
    i = pl.program_id(0)
    d = xs_ref.shape[1]

    @pl.when(i < nu_ref[0])
    def _():
        xb = xs_ref[...].astype(BF16)
        for half, (wg_ref, wu_ref, wd_ref) in enumerate(((wga_ref, wua_ref, wda_ref), (wgb_ref, wub_ref, wdb_ref))):
            a = _bdot(xb, wg_ref[...])
            h = (a * _sigmoid(a)) * _bdot(xb, wu_ref[...])
            ys_ref[:, half * d:(half + 1) * d] = _bdot(h.astype(BF16), wd_ref[...])

    @pl.when(i >= nu_ref[0])
    def _():
        ys_ref[...] = jnp.zeros_like(ys_ref)


def _experts(block_ea, block_eb, n_used, xs, wg, wu, wd, blk):
    n_rows, d = xs.shape
    de = wg.shape[-1]
    low = lambda i, ea, eb, nu: (ea[i], 0, 0)
    high = lambda i, ea, eb, nu: (eb[i], 0, 0)
    grid_spec = pltpu.PrefetchScalarGridSpec(
        num_scalar_prefetch=3,
        grid=(n_rows // blk,),
        in_specs=[pl.BlockSpec((blk, d), lambda i, ea, eb, nu: (i, 0)),
                  pl.BlockSpec((None, d, de), low), pl.BlockSpec((None, d, de), low),
                  pl.BlockSpec((None, de, d), low),
                  pl.BlockSpec((None, d, de), high), pl.BlockSpec((None, d, de), high),
                  pl.BlockSpec((None, de, d), high)],
        out_specs=pl.BlockSpec((blk, 2 * d), lambda i, ea, eb, nu: (i, 0)),
    )
    return pl.pallas_call(
        _expert_kernel, grid_spec=grid_spec,
        out_shape=jax.ShapeDtypeStruct((n_rows, 2 * d), F32),
        compiler_params=_params("arbitrary"),
        name="moe_experts",
    )(block_ea, block_eb, n_used, xs, wg, wu, wd, wg, wu, wd)


def _combine_kernel(dest_ref, x_ref, gate_ref, g_ref, b_ref, ys_ref, o_ref, ybuf, sem, *, tm, alpha):
    def issue(r, carry):
        _row_copy(ys_ref, dest_ref[0, r], ybuf, r, sem).start()
        return carry

    def drain(r, carry):
        _row_copy(ys_ref, dest_ref[0, r], ybuf, r, sem).wait()
        return carry

    lax.fori_loop(0, tm, issue, 0, unroll=8)
    lax.fori_loop(0, tm, drain, 0, unroll=8)
    gate = gate_ref[...]
    d = x_ref.shape[1]
    y = alpha * x_ref[...] + gate[:, 0:1] * ybuf[:, :d] + gate[:, 1:2] * ybuf[:, d:]
    o_ref[...] = _layer_norm(y, g_ref[...], b_ref[...])


def _combine(dest, x, gates, ys, ln_g, ln_b, alpha):
    t, d = x.shape
    tm = min(ROW_TILE, t)
    return pl.pallas_call(
        functools.partial(_combine_kernel, tm=tm, alpha=alpha),
        grid=(t // tm,),
        in_specs=[pl.BlockSpec((1, tm), lambda i: (0, i), memory_space=pltpu.SMEM),
                  pl.BlockSpec((tm, d), lambda i: (i, 0)),
                  pl.BlockSpec((tm, 2), lambda i: (i, 0)),
                  pl.BlockSpec((1, d), lambda i: (0, 0)), pl.BlockSpec((1, d), lambda i: (0, 0)),
                  pl.BlockSpec(memory_space=pl.ANY)],
        out_specs=pl.BlockSpec((tm, d), lambda i: (i, 0)),
        out_shape=jax.ShapeDtypeStruct((t, d), F32),
        scratch_shapes=[pltpu.VMEM((tm, 2 * d), F32), pltpu.SemaphoreType.DMA(())],
        compiler_params=_params("arbitrary"),
        name="moe_combine_ln",
    )(dest, x, gates, ln_g.reshape(1, d), ln_b.reshape(1, d), ys)


def _moe_ln(x, w_c, b_c, w_f, b_f, wg, wu, wd, ln_g, ln_b, alpha):
    t, d = x.shape
    n_exp = w_f.shape[1]
    per_group = n_exp // E_GROUPS
    pad = LANES - E_GROUPS - n_exp
    w_route = jnp.concatenate([w_c.astype(F32), w_f.astype(F32), jnp.zeros((d, pad), F32)], axis=1)
    b_route = jnp.concatenate([b_c.astype(F32), b_f.astype(F32), jnp.zeros((pad,), F32)]).reshape(1, LANES)
    route, cnt = _router(x, w_route.astype(BF16), b_route, n_exp)
    bucket = route[:, 0].astype(I32)
    rank = route[:, 1].astype(I32)
    gates = route[:, 2:4]
    pairs = [(g * per_group + a, g * per_group + b) for g in range(E_GROUPS)
             for a in range(per_group) for b in range(a + 1, per_group)]
    n_buckets = len(pairs)
    assert n_buckets <= LANES
    blk = MOE_BLOCK if t >= 8 * MOE_BLOCK else 2 * SUBLANES
    counts = cnt[0, :n_buckets].astype(I32)
    padded = (counts + blk - 1) // blk * blk
    pend = jnp.cumsum(padded)
    poff = pend - padded
    dest = (poff[bucket] + rank).reshape(1, t)
    n_blocks = -(-t // blk) + n_buckets
    starts = jnp.arange(n_blocks, dtype=I32) * blk
    block_bucket = jnp.minimum(jnp.sum(pend[None, :] <= starts[:, None], axis=1), n_buckets - 1)
    pair_tab = jnp.array(pairs, dtype=I32)
    block_ea = pair_tab[block_bucket, 0]
    block_eb = pair_tab[block_bucket, 1]
    n_used = (pend[-1] // blk).astype(I32).reshape(1)
    xs = _dispatch(dest, x, n_blocks * blk)
    ys = _experts(block_ea, block_eb, n_used, xs, wg, wu, wd, blk)
    return _combine(dest, x, gates, ys, ln_g, ln_b, alpha)


def _head_sum(x, ones):
    hi = x.astype(BF16)
    lo = (x - hi.astype(F32)).astype(BF16)
    parts = []
    for j in range(x.shape[1] // LANES):
        sl = slice(j * LANES, (j + 1) * LANES)
        parts.append(_bdot(hi[:, sl], ones) + _bdot(lo[:, sl], ones))
    return jnp.concatenate(parts, axis=-1)


def _rw_prep_kernel(sh_ref, halo_ref, sh0_ref, mu_ref, w0_ref, w2_ref, a0_ref, a2_ref, g2_ref,
                    kk_ref, ka_ref, rk_ref, ones_ref,
                    r_o, w_o, k_o, v_o, kk_o, b_o, bonus_o, g_o, *, batch, cw):
    sh = sh_ref[...]
    first = jnp.where(pl.program_id(0) == 0, sh0_ref[...], halo_ref[...])
    prev = jnp.concatenate([first, sh[:sh.shape[0] - batch]], axis=0)
    shm = sh + (prev - sh) * mu_ref[...]
    r, k, v = shm[:, :cw], shm[:, cw:2 * cw], shm[:, 2 * cw:3 * cw]
    xwa = shm[:, 3 * cw:3 * cw + LANES]
    xg = shm[:, 3 * cw + LANES:]
    ones = ones_ref[...]
    z = w0_ref[...] + _bdot(jnp.tanh(xwa).astype(BF16), w2_ref[...])
    softplus = jnp.maximum(-z, 0.0) + jnp.log1p(jnp.exp(-jnp.abs(z)))
    w_o[...] = jnp.exp(-jnp.exp(-softplus - 0.5))
    a = _sigmoid(a0_ref[...] + _bdot(xwa.astype(BF16), a2_ref[...]))
    g_o[...] = _bdot(_sigmoid(xg).astype(BF16), g2_ref[...])
    kk = k * kk_ref[...]
    kk = kk / jnp.maximum(jnp.sqrt(_head_sum(kk * kk, ones)), 1e-12)
    k2 = k * (1.0 + (a - 1.0) * ka_ref[...])
    r_o[...] = r
    k_o[...] = k2
    v_o[...] = v
    kk_o[...] = kk
    b_o[...] = kk * a
    bonus_o[...] = _head_sum(r * k2 * rk_ref[...], ones) * v


def _rw_prep(sh, shift0, batch, prm):
    t, sw = sh.shape
    cw = prm["w0"].shape[-1]
    tm = min(256, t)
    hb = tm // batch
    row = lambda i: (i, 0)
    fix = lambda i: (0, 0)
    vec = lambda n: pl.BlockSpec((1, n), fix)
    return pl.pallas_call(
        functools.partial(_rw_prep_kernel, batch=batch, cw=cw),
        grid=(t // tm,),
        in_specs=[pl.BlockSpec((tm, sw), row),
                  pl.BlockSpec((batch, sw), lambda i: (jnp.maximum(i * hb - 1, 0), 0)),
                  pl.BlockSpec((batch, sw), fix), vec(sw), vec(cw),
                  pl.BlockSpec((LANES, cw), fix), vec(cw), pl.BlockSpec((LANES, cw), fix),
                  pl.BlockSpec((LANES, cw), fix), vec(cw), vec(cw), vec(cw),
                  pl.BlockSpec((LANES, LANES), fix)],
        out_specs=[pl.BlockSpec((tm, cw), row)] * 8,
        out_shape=[jax.ShapeDtypeStruct((t, cw), F32)] * 8,
        compiler_params=_params("parallel"),
        name="rwkv_prep",
    )(sh, sh, shift0, prm["mu"], prm["w0"], prm["w2"], prm["a0"], prm["a2"], prm["g2"],
      prm["k_k"], prm["k_a"], prm["r_k"], prm["ones"])


RW_TILES_A = 4
RW_TILES_B = 2
RW_TICK = 8


def _rwkv_kernel(r_ref, w_ref, k_ref, v_ref, kk_ref, b_ref, sa0_ref, sb0_ref, y_ref, san_ref, sbn_ref,
                 st_a, st_b, xa0, xb0, xa1, xb1, *, batch, steps):
    hd = HEAD_DIM
    hv = hd // 2
    ti = pl.program_id(0)

    @pl.when(ti == 0)
    def _():
        st_a[...] = sa0_ref[...]
        st_b[...] = sb0_ref[...]

    operands = (r_ref, w_ref, k_ref, v_ref, kk_ref, b_ref)
    def lane_of(rows):
        return lax.broadcasted_iota(I32, (rows, LANES), 1)

    lane_lo = lane_of(hd) < hd
    lane_lo_out = lane_of(batch) < hd
    even_quarter = (lane_of(hd) // hv) % 2 == 0
    quarter_v = lane_of(hv) // hv
    quarter_out = lane_of(batch) // hv

    def pick_quarter(q, blocks):
        return jnp.where(q == 0, blocks[0], jnp.where(q == 1, blocks[1], jnp.where(q == 2, blocks[2], blocks[3])))

    def padded(parts, rows):
        have = len(parts) * batch
        return parts + ([jnp.zeros((rows - have, LANES), F32)] if have < rows else [])

    def run(state, x, nv, tick):
        acc = jnp.zeros((nv, LANES), F32)
        for kidx in range(hd):
            acc = acc + state[kidx] * x[4, kidx:kidx + 1, :]
            if kidx % RW_TICK == RW_TICK - 1:
                tick()
        sa = acc
        vv = x[3, 0:nv, :]
        acc = jnp.zeros((nv, LANES), F32)
        for kidx in range(hd):
            row = slice(kidx, kidx + 1)
            sn = state[kidx] * x[1, row, :] - sa * x[5, row, :] + vv * x[2, row, :]
            state[kidx] = sn
            acc = acc + sn * x[0, row, :]
            if kidx % RW_TICK == RW_TICK - 1:
                tick()
        return acc

    def transpose_operand(o, r0, xa, xb):
        slab = operands[o][pl.ds(r0, batch), :]
        tiles = [slab[:, j * LANES:(j + 1) * LANES] for j in range(RW_TILES_A + RW_TILES_B)]
        ta = jnp.concatenate(padded(tiles[:RW_TILES_A], hd) * 2, axis=0).T
        xa[o] = jnp.where(lane_lo, ta[:hd], ta[hd:])
        tb = jnp.concatenate(padded(tiles[RW_TILES_A:], hv) * 4, axis=0).T
        if o == 3:
            xb[o, pl.ds(0, hv), :] = pick_quarter(quarter_v, (tb[:hv], tb[hd:hd + hv], tb[hv:hd], tb[hd + hv:]))
        else:
            xb[o] = jnp.where(even_quarter, tb[:hd], tb[hd:])

    def time_step(t, cur, nxt):
        r0 = pl.multiple_of(t * batch, batch)
        rn = pl.multiple_of(jnp.minimum(t + 1, steps - 1) * batch, batch)
        pending = [functools.partial(transpose_operand, o, rn, *nxt) for o in range(len(operands))]

        def tick():
            if pending:
                pending.pop(0)()

        ya = run(st_a, cur[0], hd, tick)
        yb = run(st_b, cur[1], hv, tick)
        while pending:
            tick()
        mta = jnp.concatenate([ya, ya], axis=0).T
        for j in range(RW_TILES_A):
            lo = mta[j * batch:(j + 1) * batch]
            hi = mta[hd + j * batch:hd + (j + 1) * batch]
            y_ref[pl.ds(r0, batch), j * LANES:(j + 1) * LANES] = jnp.where(lane_lo_out, lo, hi)
        mtb = jnp.concatenate([yb] * 4, axis=0).T
        for j in range(RW_TILES_B):
            blocks = [mtb[base + j * batch:base + (j + 1) * batch] for base in (0, hd, hv, hd + hv)]
            tile = RW_TILES_A + j
            y_ref[pl.ds(r0, batch), tile * LANES:(tile + 1) * LANES] = pick_quarter(quarter_out, blocks)

    even_bufs, odd_bufs = (xa0, xb0), (xa1, xb1)
    for o in range(len(operands)):
        transpose_operand(o, 0, *even_bufs)

    def step_pair(p, carry):
        time_step(2 * p, even_bufs, odd_bufs)
        time_step(2 * p + 1, odd_bufs, even_bufs)
        return carry

    lax.fori_loop(0, steps // 2, step_pair, 0)

    @pl.when(ti == pl.num_programs(0) - 1)
    def _():
        san_ref[...] = st_a[...]
        sbn_ref[...] = st_b[...]


def _rwkv(ops, s0, batch):
    t, cw = ops[0].shape
    hd, hv = HEAD_DIM, HEAD_DIM // 2
    assert cw == (RW_TILES_A + RW_TILES_B) * LANES and RW_TILES_A * batch <= hd
    ca, cb = RW_TILES_A * batch, RW_TILES_B * batch
    if s0 is None:
        sa0 = jnp.zeros((hd, hd, LANES), F32)
        sb0 = jnp.zeros((hd, hv, LANES), F32)
    else:
        s0 = s0.astype(F32)
        sa0 = s0[:, :2 * RW_TILES_A].reshape(batch, RW_TILES_A, 2, hd, hd).transpose(4, 3, 2, 1, 0)
        sa0 = jnp.pad(sa0.reshape(hd, hd, 2, ca), ((0, 0),) * 3 + ((0, hd - ca),)).reshape(hd, hd, LANES)
        sb0 = s0[:, 2 * RW_TILES_A:].reshape(batch, RW_TILES_B, 2, 2, hv, hd).transpose(5, 4, 3, 2, 1, 0)
        sb0 = jnp.pad(sb0.reshape(hd, hv, 2, 2, cb), ((0, 0),) * 4 + ((0, hv - cb),)).reshape(hd, hv, LANES)
    steps = min(16, t // batch)
    assert steps % 2 == 0, "the time loop is unrolled in even/odd operand-buffer pairs"
    rows = steps * batch
    blk = pl.BlockSpec((rows, cw), lambda i: (i, 0))
    st_a = pl.BlockSpec((hd, hd, LANES), lambda i: (0, 0, 0))
    st_b = pl.BlockSpec((hd, hv, LANES), lambda i: (0, 0, 0))
    y, san, sbn = pl.pallas_call(
        functools.partial(_rwkv_kernel, batch=batch, steps=steps),
        grid=(t // rows,),
        in_specs=[blk] * 6 + [st_a, st_b],
        out_specs=[blk, st_a, st_b],
        out_shape=[jax.ShapeDtypeStruct((t, cw), F32), jax.ShapeDtypeStruct((hd, hd, LANES), F32),
                   jax.ShapeDtypeStruct((hd, hv, LANES), F32)],
        scratch_shapes=[pltpu.VMEM((hd, hd, LANES), F32), pltpu.VMEM((hd, hv, LANES), F32),
                        ] + [pltpu.VMEM((6, hd, LANES), F32)] * 4,
        compiler_params=_params("arbitrary"),
        name="rwkv_recurrence",
    )(*ops, sa0, sb0)
    sa = san.reshape(hd, hd, 2, hd)[..., :ca].reshape(hd, hd, 2, RW_TILES_A, batch).transpose(4, 3, 2, 1, 0)
    sb = sbn.reshape(hd, hv, 2, 2, hv)[..., :cb].reshape(hd, hv, 2, 2, RW_TILES_B, batch).transpose(5, 4, 3, 2, 1, 0)
    s_last = jnp.concatenate([sa.reshape(batch, 2 * RW_TILES_A, hd, hd), sb.reshape(batch, 2 * RW_TILES_B, hd, hd)],
                             axis=1)
    return y, s_last


def _rw_post_kernel(y_ref, bonus_ref, g_ref, lg_ref, lb_ref, ones_ref, o_ref):
    y = y_ref[...]
    ones = ones_ref[...]
    mu = _head_sum(y, ones) * (1.0 / HEAD_DIM)
    d = y - mu
    var = _head_sum(d * d, ones) * (1.0 / HEAD_DIM)
    yn = d * lax.rsqrt(var + GN_EPS) * lg_ref[...] + lb_ref[...]
    o_ref[...] = (yn + bonus_ref[...]) * g_ref[...]


def _rw_post(y, bonus, g, ln_g, ln_b, ones):
    t, cw = y.shape
    tm = min(ROW_TILE, t)
    row = pl.BlockSpec((tm, cw), lambda i: (i, 0))
    vec = pl.BlockSpec((1, cw), lambda i: (0, 0))
    return pl.pallas_call(
        _rw_post_kernel, grid=(t // tm,),
        in_specs=[row, row, row, vec, vec, pl.BlockSpec((LANES, LANES), lambda i: (0, 0))],
        out_specs=row, out_shape=jax.ShapeDtypeStruct((t, cw), F32),
        compiler_params=_params("parallel"),
        name="rwkv_post",
    )(y, bonus, g, ln_g, ln_b, ones)


def _pool_kernel(main_ref, halo_ref, w_ref, scale_ref, o_ref, *, batch, pos0):
    tm = main_ref.shape[0]
    full = jnp.concatenate([main_ref[...], halo_ref[...]], axis=0)
    b = batch
    n = full.shape[0]
    a2 = full[b:] + full[:n - b]
    a4 = a2[2 * b:] + a2[:n - 3 * b]
    a8 = a4[4 * b:] + a4[:n - 7 * b]
    a16 = a8[8 * b:] + a8[:n - 15 * b]
    cur = full[16 * b:16 * b + tm]
    sums = (a2[15 * b:15 * b + tm], a4[13 * b:13 * b + tm], a8[9 * b:9 * b + tm], a16[b:b + tm])
    lane = lax.broadcasted_iota(I32, cur.shape, 1)
    grp = lane // (cur.shape[1] // len(POOL_WINDOWS))
    win_sum = sums[-1]
    width = jnp.full(cur.shape, float(POOL_WINDOWS[-1]), F32)
    for j in range(len(POOL_WINDOWS) - 2, -1, -1):
        win_sum = jnp.where(grp == j, sums[j], win_sum)
        width = jnp.where(grp == j, float(POOL_WINDOWS[j]), width)
    row = lax.broadcasted_iota(I32, cur.shape, 0)
    t_idx = pl.program_id(0) * (tm // b) + row // b
    cnt = jnp.minimum(width, (pos0 + t_idx + 1).astype(F32))
    pooled = win_sum / cnt - cur
    o_ref[...] = _bdot(pooled.astype(BF16), w_ref[...]) * scale_ref[...]


def _pool(fp, t, batch, w_bd, scale, pos0):
    dw = fp.shape[1]
    tm = min(ROW_TILE, t)
    halo = 16 * batch
    return pl.pallas_call(
        functools.partial(_pool_kernel, batch=batch, pos0=pos0),
        grid=(t // tm,),
        in_specs=[pl.BlockSpec((tm, dw), lambda i: (i, 0)),
                  pl.BlockSpec((halo, dw), lambda i: ((i + 1) * (tm // halo), 0)),
                  pl.BlockSpec((dw, dw), lambda i: (0, 0)), pl.BlockSpec((1, dw), lambda i: (0, 0))],
        out_specs=pl.BlockSpec((tm, dw), lambda i: (i, 0)),
        out_shape=jax.ShapeDtypeStruct((t, dw), F32),
        compiler_params=_params("parallel"),
        name="causal_pool",
    )(fp, fp, w_bd, scale)


def _to_batch_major(a, seq, batch):
    return a.reshape(seq, batch, a.shape[-1]).transpose(1, 0, 2)


def _to_time_major(a):
    bsz, seq, width = a.shape
    return a.transpose(1, 0, 2).reshape(seq * bsz, width)


def _even_mixer(x, batch, seq, h0, cache_k, cache_v, w, j, bias, alpha, ln_g, ln_b):
    a_width = w["s5_w_glu"].shape[-1]
    w_in = w["ev_w_in"][j]
    qw = a_width
    kvw = (w_in.shape[1] - 2 * a_width) // 2
    u, q, k, v = _matmul([x], [w_in.astype(BF16)], splits=(a_width, qw, kvw, kvw))
    prm = _s5_params(w["s5_a_re"][j], w["s5_a_im"][j], w["s5_log_dt"][j], w["s5_b_re"][j], w["s5_b_im"][j],
                     w["s5_c_re"][j], w["s5_c_im"][j], w["s5_d"][j])
    nt = a_width // LANES
    ns = prm["bre"].shape[-1]
    if h0 is None:
        h0re = jnp.zeros((nt, batch, ns), F32)
        h0im = h0re
    else:
        h0re = h0[..., 0].astype(F32).reshape(batch, nt, ns).transpose(1, 0, 2)
        h0im = h0[..., 1].astype(F32).reshape(batch, nt, ns).transpose(1, 0, 2)
    g, hn_re, hn_im = _s5(u, batch, h0re, h0im, prm)
    groups, n_state = w["s5_a_re"][j].shape
    h_last = jnp.stack([hn_re.transpose(1, 0, 2).reshape(batch, groups, n_state),
                        hn_im.transpose(1, 0, 2).reshape(batch, groups, n_state)], axis=-1)
    a_out = _matmul([g], [w["s5_w_glu"][j].astype(BF16)], epilogue="glu")

    qb, kb, vb = (_to_batch_major(a, seq, batch) for a in (q, k, v))
    sinks = w["swa_sinks"][j].astype(F32)
    if cache_k is None:
        att = _attention(qb, kb, vb, bias, sinks, CHUNK, CHUNK, WINDOW // CHUNK + 1, True)
        new_k, new_v = kb[:, seq - WINDOW:], vb[:, seq - WINDOW:]
    else:
        k_all = jnp.concatenate([cache_k.astype(F32).reshape(batch, WINDOW, kvw), kb], axis=1)
        v_all = jnp.concatenate([cache_v.astype(F32).reshape(batch, WINDOW, kvw), vb], axis=1)
        att = _attention(qb, k_all, v_all, bias, sinks, seq, WINDOW + seq, 1, False)
        new_k, new_v = k_all[:, seq:], v_all[:, seq:]
    att = _to_time_major(att)
    w_out = w["ev_w_out"][j].astype(BF16)
    xn = _matmul([a_out, att], [w_out[:a_width], w_out[a_width:]], epilogue="ln",
                 res=x, g=ln_g, b=ln_b, alpha=alpha)
    kv_shape = (batch, WINDOW, kvw // HEAD_DIM, HEAD_DIM)
    return xn, h_last, new_k.reshape(kv_shape), new_v.reshape(kv_shape)


def _odd_mixer(x, batch, seq, shift0, s0, pool0, pos0, w, j, alpha, ln_g, ln_b):
    t = x.shape[0]
    w_in = w["od_w_in"][j]
    cw = w["rw_w0"].shape[-1]
    sw = w["rw_mu"].shape[-1]
    dw = w_in.shape[1] - sw
    n_heads = cw // HEAD_DIM
    sh, p = _matmul([x], [w_in.astype(BF16)], splits=(sw, dw))
    zeros_half = jnp.zeros((LANES - w["rw_w2"].shape[1], cw), F32)
    head_lane = jnp.arange(LANES) // HEAD_DIM
    ones = (head_lane[:, None] == head_lane[None, :]).astype(BF16)
    prm = dict(
        mu=w["rw_mu"][j].astype(F32).reshape(1, sw), w0=w["rw_w0"][j].astype(F32).reshape(1, cw),
        w2=jnp.concatenate([w["rw_w2"][j].astype(F32), zeros_half], axis=0).astype(BF16),
        a0=w["rw_a0"][j].astype(F32).reshape(1, cw),
        a2=jnp.concatenate([zeros_half, w["rw_a2"][j].astype(F32)], axis=0).astype(BF16),
        g2=w["rw_g2"][j].astype(BF16),
        k_k=w["rw_k_k"][j].astype(F32).reshape(1, cw), k_a=w["rw_k_a"][j].astype(F32).reshape(1, cw),
        r_k=w["rw_r_k"][j].astype(F32).reshape(1, cw), ones=ones)
    if shift0 is None:
        shift0 = jnp.zeros((batch, sw), F32)
    r, dec, k2, v, kk, b, bonus, g = _rw_prep(sh, shift0.astype(F32), batch, prm)

    y, s_last = _rwkv((r, dec, k2, v, kk, b), s0, batch)
    c_out = _rw_post(y, bonus, g, w["rw_ln_g"][j].astype(F32).reshape(1, cw),
                     w["rw_ln_b"][j].astype(F32).reshape(1, cw), ones)

    if pool0 is None:
        hist = jnp.zeros((POOL_HIST * batch, dw), F32)
    else:
        hist = _to_time_major(pool0.astype(F32))
    fp = jnp.concatenate([jnp.zeros((batch, dw), F32), hist, p], axis=0)
    n_grp = len(POOL_WINDOWS)
    pg = dw // n_grp
    eye = jnp.eye(n_grp, dtype=F32)
    w_bd = (w["pool_w"][j].astype(F32)[:, :, None, :] * eye[:, None, :, None]).reshape(dw, dw).astype(BF16)
    d_out = _pool(fp, t, batch, w_bd, w["pool_scale"][j].astype(F32).reshape(1, dw), pos0)
    pool_new = _to_batch_major(fp[fp.shape[0] - POOL_HIST * batch:], POOL_HIST, batch)

    w_out = w["od_w_out"][j].astype(BF16)
    xn = _matmul([c_out, d_out], [w_out[:cw], w_out[cw:]], epilogue="ln", res=x, g=ln_g, b=ln_b, alpha=alpha)
    sh_last = sh[t - batch:]
    return xn, sh_last, s_last, pool_new


def _trunk(x3, s5_0, k_0, v_0, rw_0, shift_0, pool_0, pos0, w):
    batch, seq, _ = x3.shape
    depth = w["ln1_g"].shape[0]
    alpha = (2 * depth) ** 0.25
    x = _to_time_major(x3.astype(F32))
    prompt = s5_0 is None
    if prompt:
        bias = _bias_table(w["rel_bias"], CHUNK, (WINDOW // CHUNK + 1) * CHUNK)
    else:
        bias = _bias_table(w["rel_bias"], seq, WINDOW + seq)
    s5_new, k_new, v_new, rw_new, sh_new, pool_new = [], [], [], [], [], []
    for i in range(depth):
        j = i // 2
        if i % 2 == 0:
            x, h1, nk, nv = _even_mixer(
                x, batch, seq, None if prompt else s5_0[j], None if prompt else k_0[j],
                None if prompt else v_0[j], w, j, bias, alpha, w["ln1_g"][i], w["ln1_b"][i])
            s5_new.append(h1)
            k_new.append(nk)
            v_new.append(nv)
        else:
            x, sh1, s1, pl1 = _odd_mixer(
                x, batch, seq, None if prompt else shift_0[j], None if prompt else rw_0[j],
                None if prompt else pool_0[j], pos0, w, j, alpha, w["ln1_g"][i], w["ln1_b"][i])
            sh_new.append(sh1)
            rw_new.append(s1)
            pool_new.append(pl1)
        x = _moe_ln(x, w["moe_w_coarse"][i], w["moe_b_coarse"][i], w["moe_w_fine"][i], w["moe_b_fine"][i],
                    w["moe_w_gate"][i].astype(BF16), w["moe_w_up"][i].astype(BF16),
                    w["moe_w_down"][i].astype(BF16), w["ln2_g"][i], w["ln2_b"][i], alpha)
    y = _to_batch_major(x, seq, batch)
    return (y, jnp.stack(s5_new), jnp.stack(k_new), jnp.stack(v_new),
            jnp.stack(rw_new), jnp.stack(sh_new), jnp.stack(pool_new))


def kernel(x_prompt, x_sample, state_s5, cache_swa_k, cache_swa_v, state_rwkv, state_shift, state_pool, rel_bias, ev_w_in, ev_w_out, s5_a_re, s5_a_im, s5_log_dt, s5_b_re, s5_b_im, s5_c_re, s5_c_im, s5_d, s5_w_glu, swa_sinks, od_w_in, od_w_out, rw_mu, rw_w0, rw_w2, rw_a0, rw_a2, rw_g2, rw_k_k, rw_k_a, rw_r_k, rw_ln_g, rw_ln_b, pool_w, pool_scale, ln1_g, ln1_b, ln2_g, ln2_b, moe_w_coarse, moe_b_coarse, moe_w_fine, moe_b_fine, moe_w_gate, moe_w_up, moe_w_down):
    w = dict(rel_bias=rel_bias, ev_w_in=ev_w_in, ev_w_out=ev_w_out, s5_a_re=s5_a_re, s5_a_im=s5_a_im,
             s5_log_dt=s5_log_dt, s5_b_re=s5_b_re, s5_b_im=s5_b_im, s5_c_re=s5_c_re, s5_c_im=s5_c_im,
             s5_d=s5_d, s5_w_glu=s5_w_glu, swa_sinks=swa_sinks, od_w_in=od_w_in, od_w_out=od_w_out,
             rw_mu=rw_mu, rw_w0=rw_w0, rw_w2=rw_w2, rw_a0=rw_a0, rw_a2=rw_a2, rw_g2=rw_g2, rw_k_k=rw_k_k,
             rw_k_a=rw_k_a, rw_r_k=rw_r_k, rw_ln_g=rw_ln_g, rw_ln_b=rw_ln_b, pool_w=pool_w,
             pool_scale=pool_scale, ln1_g=ln1_g, ln1_b=ln1_b, ln2_g=ln2_g, ln2_b=ln2_b,
             moe_w_coarse=moe_w_coarse, moe_b_coarse=moe_b_coarse, moe_w_fine=moe_w_fine,
             moe_b_fine=moe_b_fine, moe_w_gate=moe_w_gate, moe_w_up=moe_w_up, moe_w_down=moe_w_down)
    y_s, s5_s, k_s, v_s, rw_s, sh_s, pool_s = _trunk(
        x_sample, state_s5, cache_swa_k, cache_swa_v, state_rwkv, state_shift, state_pool, PAST_LEN, w)
    y_p, s5_p, k_p, v_p, rw_p, sh_p, pool_p = _trunk(x_prompt, None, None, None, None, None, None, 0, w)
    return (y_p, y_s, s5_p, s5_s, k_p, v_p, k_s, v_s, rw_p, rw_s, sh_p, sh_s, pool_p, pool_s)
```

```python
import functools
import math

import jax
import jax.numpy as jnp
from jax import lax
from jax.experimental import pallas as pl
from jax.experimental.pallas import tpu as pltpu

F32 = jnp.float32
BF16 = jnp.bfloat16
I32 = jnp.int32

CHUNK = 64
WINDOW = 128
HEAD_DIM = 64
PAST_LEN = 1024
N_BUCKETS = 32
MAX_DISTANCE = 128
POOL_WINDOWS = (2, 4, 8, 16)
POOL_HIST = max(POOL_WINDOWS) - 1
E_GROUPS = 4
LN_EPS = 1e-5
GN_EPS = 64e-5
A_GROUP = 16

LANES = 128
SUBLANES = 8
VMEM_LIMIT_BYTES = 56 * 1024 * 1024

ROW_TILE = 512
MOE_BLOCK = 256


def _params(*sem):
    return pltpu.CompilerParams(dimension_semantics=sem, vmem_limit_bytes=VMEM_LIMIT_BYTES)


def _sigmoid(x):
    return 1.0 / (1.0 + jnp.exp(-x))


def _layer_norm(y, g, b):
    mu = jnp.mean(y, axis=-1, keepdims=True)
    d = y - mu
    var = jnp.mean(d * d, axis=-1, keepdims=True)
    return d * lax.rsqrt(var + LN_EPS) * g + b


def _bdot(a, b):
    return jnp.dot(a, b, preferred_element_type=F32)


def _mm_kernel(*refs, n_in, splits, epilogue, alpha):
    xs = refs[:n_in]
    ws = refs[n_in:2 * n_in]
    pos = 2 * n_in
    acc = None
    for x_ref, w_ref in zip(xs, ws):
        d = _bdot(x_ref[...].astype(BF16), w_ref[...])
        acc = d if acc is None else acc + d
    if epilogue == "ln":
        res_ref, g_ref, b_ref = refs[pos:pos + 3]
        pos += 3
        acc = _layer_norm(alpha * res_ref[...] + acc, g_ref[...], b_ref[...])
    elif epilogue == "glu":
        acc = xs[0][...] * _sigmoid(acc)
    off = 0
    for o_ref, n in zip(refs[pos:], splits):
        o_ref[...] = acc[:, off:off + n]
        off += n


def _matmul(xs, ws, splits=None, epilogue=None, res=None, g=None, b=None, alpha=None):
    t = xs[0].shape[0]
    n = ws[0].shape[1]
    splits = tuple(splits) if splits else (n,)
    tm = min(ROW_TILE, t)
    in_specs = [pl.BlockSpec((tm, x.shape[1]), lambda i: (i, 0)) for x in xs]
    in_specs += [pl.BlockSpec(w.shape, lambda i: (0, 0)) for w in ws]
    args = list(xs) + list(ws)
    if epilogue == "ln":
        in_specs += [pl.BlockSpec((tm, n), lambda i: (i, 0)),
                     pl.BlockSpec((1, n), lambda i: (0, 0)), pl.BlockSpec((1, n), lambda i: (0, 0))]
        args += [res, g.reshape(1, n), b.reshape(1, n)]
    outs = pl.pallas_call(
        functools.partial(_mm_kernel, n_in=len(xs), splits=splits, epilogue=epilogue, alpha=alpha),
        grid=(t // tm,),
        in_specs=in_specs,
        out_specs=[pl.BlockSpec((tm, s), lambda i: (i, 0)) for s in splits],
        out_shape=[jax.ShapeDtypeStruct((t, s), F32) for s in splits],
        compiler_params=_params("parallel"),
        name="matmul_" + (epilogue or "plain"),
    )(*args)
    return outs if len(splits) > 1 else outs[0]


def _s5_kernel(u_ref, bre_ref, bim_ref, cre_ref, cim_ref, are_ref, aim_ref, d_ref, h0re_ref, h0im_ref,
               g_ref, hnre_ref, hnim_ref, hre, him, car_re, car_im, *, batch, steps):
    ti = pl.program_id(1)

    @pl.when(ti == 0)
    def _():
        car_re[...] = h0re_ref[...]
        car_im[...] = h0im_ref[...]

    u = u_ref[...]
    ub = u.astype(BF16)
    hre[...] = _bdot(ub, bre_ref[...])
    him[...] = _bdot(ub, bim_ref[...])
    a_re = jnp.broadcast_to(are_ref[...], car_re.shape)
    a_im = jnp.broadcast_to(aim_ref[...], car_re.shape)

    def step(t, carry):
        h_re, h_im = carry
        r0 = pl.multiple_of(t * batch, batch)
        n_re = a_re * h_re - a_im * h_im + hre[pl.ds(r0, batch), :]
        n_im = a_re * h_im + a_im * h_re + him[pl.ds(r0, batch), :]
        hre[pl.ds(r0, batch), :] = n_re
        him[pl.ds(r0, batch), :] = n_im
        return n_re, n_im

    h_re, h_im = lax.fori_loop(0, steps, step, (car_re[...], car_im[...]))
    car_re[...] = h_re
    car_im[...] = h_im
    hnre_ref[...] = h_re
    hnim_ref[...] = h_im
    y = (_bdot(hre[...].astype(BF16), cre_ref[...]) - _bdot(him[...].astype(BF16), cim_ref[...])
         + d_ref[...] * u)
    cdf = 0.5 * (1.0 + jnp.tanh(math.sqrt(2.0 / math.pi) * (y + 0.044715 * (y * y * y))))
    g_ref[...] = y * cdf


def _s5(u, batch, h0re, h0im, prm):
    t, width = u.shape
    nt = width // LANES
    ns = prm["bre"].shape[-1]
    rows = min(1024, t)
    steps = rows // batch
    tile = lambda j, i: (j, 0, 0)
    return pl.pallas_call(
        functools.partial(_s5_kernel, batch=batch, steps=steps),
        grid=(nt, t // rows),
        in_specs=[pl.BlockSpec((rows, LANES), lambda j, i: (i, j)),
                  pl.BlockSpec((None, LANES, ns), tile), pl.BlockSpec((None, LANES, ns), tile),
                  pl.BlockSpec((None, ns, LANES), tile), pl.BlockSpec((None, ns, LANES), tile),
                  pl.BlockSpec((None, 1, ns), tile), pl.BlockSpec((None, 1, ns), tile),
                  pl.BlockSpec((None, 1, LANES), tile),
                  pl.BlockSpec((None, batch, ns), tile), pl.BlockSpec((None, batch, ns), tile)],
        out_specs=[pl.BlockSpec((rows, LANES), lambda j, i: (i, j)),
                   pl.BlockSpec((None, batch, ns), tile), pl.BlockSpec((None, batch, ns), tile)],
        out_shape=[jax.ShapeDtypeStruct((t, width), F32),
                   jax.ShapeDtypeStruct((nt, batch, ns), F32), jax.ShapeDtypeStruct((nt, batch, ns), F32)],
        scratch_shapes=[pltpu.VMEM((rows, ns), F32), pltpu.VMEM((rows, ns), F32),
                        pltpu.VMEM((batch, ns), F32), pltpu.VMEM((batch, ns), F32)],
        compiler_params=_params("parallel", "arbitrary"),
        name="s5_scan",
    )(u, prm["bre"], prm["bim"], prm["cre"], prm["cim"], prm["are"], prm["aim"], prm["d"], h0re, h0im)


def _s5_params(a_re, a_im, log_dt, b_re, b_im, c_re, c_im, d_skip):
    groups, n_state = a_re.shape
    gpt = LANES // A_GROUP
    nt = groups // gpt
    lam = lax.complex(a_re.astype(F32), a_im.astype(F32))
    dt = jnp.exp(log_dt.astype(F32))[:, None]
    a_bar = jnp.exp(lam * dt)
    b_bar = ((a_bar - 1.0) / lam)[..., None] * lax.complex(b_re.astype(F32), b_im.astype(F32))
    eye = jnp.eye(gpt, dtype=F32)

    def in_proj(m):
        m = m.reshape(nt, gpt, n_state, A_GROUP).transpose(0, 1, 3, 2)
        return (m[:, :, :, None, :] * eye[None, :, None, :, None]).reshape(nt, gpt * A_GROUP, gpt * n_state)

    def out_proj(m):
        m = m.astype(F32).reshape(nt, gpt, A_GROUP, n_state).transpose(0, 1, 3, 2)
        return (m[:, :, :, None, :] * eye[None, :, None, :, None]).reshape(nt, gpt * n_state, gpt * A_GROUP)

    return dict(bre=in_proj(b_bar.real).astype(BF16), bim=in_proj(b_bar.imag).astype(BF16),
                cre=out_proj(c_re).astype(BF16), cim=out_proj(c_im).astype(BF16),
                are=a_bar.real.reshape(nt, 1, gpt * n_state), aim=a_bar.imag.reshape(nt, 1, gpt * n_state),
                d=d_skip.astype(F32).reshape(nt, 1, LANES))


def _attn_kernel(*refs, n_kv, masked, tiles_per_kv, batch):
    n_q = 2 * tiles_per_kv
    q_refs = refs[:n_q]
    k_refs = refs[n_q:n_q + n_kv]
    v_refs = refs[n_q + n_kv:n_q + 2 * n_kv]
    bias_ref, sink_ref, o_ref, o_tiles = refs[n_q + 2 * n_kv:]
    c = pl.program_id(0)
    cq = q_refs[0].shape[0] // batch
    ck = k_refs[0].shape[0] // batch
    nk = n_kv * ck
    low_half = lax.broadcasted_iota(I32, (nk, LANES), 1) < HEAD_DIM
    if masked:
        col = lax.broadcasted_iota(I32, (tiles_per_kv * cq, nk), 1)
        valid = (c - (n_kv - 1)) * CHUNK + col >= 0

    def one_sequence(b, carry):
        kt = jnp.concatenate([r[pl.ds(b, ck, stride=batch), :] for r in k_refs], axis=0)
        vt = jnp.concatenate([r[pl.ds(b, ck, stride=batch), :] for r in v_refs], axis=0)
        k_sw = pltpu.roll(kt, HEAD_DIM, 1)
        v_sw = pltpu.roll(vt, HEAD_DIM, 1)

        def placed(t, t_sw, kvh, half):
            src = t if kvh == half else t_sw
            keep = low_half if half == 0 else jnp.logical_not(low_half)
            return jnp.where(keep, src, 0.0).astype(BF16)

        scores = []
        for kvh in range(2):
            tiles = [q_refs[kvh * tiles_per_kv + i][pl.ds(b, cq, stride=batch), :] for i in range(tiles_per_kv)]
            qs = jnp.concatenate(tiles, axis=0).astype(BF16)
            for half in range(2):
                s = lax.dot_general(qs, placed(kt, k_sw, kvh, half), (((1,), (1,)), ((), ())),
                                    preferred_element_type=F32)
                s = s * (HEAD_DIM ** -0.5) + bias_ref[2 * kvh + half]
                if masked:
                    s = jnp.where(valid, s, -jnp.inf)
                scores.append(s)
        probs = []
        for grp, s in enumerate(scores):
            sink = sink_ref[grp]
            m = jnp.maximum(jnp.max(s, axis=-1, keepdims=True), sink)
            p = jnp.exp(s - m)
            denom = jnp.sum(p, axis=-1, keepdims=True) + jnp.exp(sink - m)
            probs.append((p / denom).astype(BF16))
        for kvh in range(2):
            o = (_bdot(probs[2 * kvh], placed(vt, v_sw, kvh, 0))
                 + _bdot(probs[2 * kvh + 1], placed(vt, v_sw, kvh, 1)))
            for i in range(tiles_per_kv):
                tile = kvh * tiles_per_kv + i
                o_tiles[tile, pl.ds(b, cq, stride=batch), :] = o[i * cq:(i + 1) * cq]
        return carry

    lax.fori_loop(0, batch, one_sequence, 0, unroll=2)
    for tile in range(n_q):
        o_ref[:, tile * LANES:(tile + 1) * LANES] = o_tiles[tile]


def _attention(q, k, v, bias, sinks, batch, cq, ck, n_kv, masked):
    tq, qw = q.shape
    kw = k.shape[-1]
    n_heads = qw // HEAD_DIM
    assert kw == LANES and n_heads % 4 == 0, "two kv heads in one lane tile, an even number of query tiles each"
    tiles_per_kv = n_heads // 4
    heads = [[2 * (kvh * tiles_per_kv + i) + half for i in range(tiles_per_kv)]
             for kvh in range(2) for half in range(2)]
    bias_g = jnp.stack([jnp.concatenate([bias[h] for h in hs], axis=0) for hs in heads])
    sink_g = jnp.stack([jnp.concatenate([jnp.ones((cq, 1), F32) * sinks[h] for h in hs], axis=0)
                        for hs in heads])

    def kv_spec(s):
        return pl.BlockSpec((ck * batch, kw), lambda c: (jnp.maximum(c - (n_kv - 1) + s, 0), 0))

    return pl.pallas_call(
        functools.partial(_attn_kernel, n_kv=n_kv, masked=masked, tiles_per_kv=tiles_per_kv, batch=batch),
        grid=(tq // (cq * batch),),
        in_specs=[pl.BlockSpec((cq * batch, LANES), lambda c, j=j: (c, j)) for j in range(qw // LANES)]
        + [kv_spec(s) for s in range(n_kv)] + [kv_spec(s) for s in range(n_kv)]
        + [pl.BlockSpec(bias_g.shape, lambda c: (0, 0, 0)), pl.BlockSpec(sink_g.shape, lambda c: (0, 0, 0))],
        out_specs=pl.BlockSpec((cq * batch, qw), lambda c: (c, 0)),
        out_shape=jax.ShapeDtypeStruct((tq, qw), F32),
        scratch_shapes=[pltpu.VMEM((qw // LANES, cq * batch, LANES), F32)],
        compiler_params=_params("parallel"),
        name="swa_attention",
    )(*([q] * (qw // LANES)), *([k] * n_kv), *([v] * n_kv), bias_g, sink_g)


def _t5_bucket(rel):
    half = N_BUCKETS // 2
    max_exact = half // 2
    n = jnp.abs(rel)
    large = max_exact + (jnp.log(jnp.maximum(n, 1).astype(F32) / max_exact)
                         / math.log(MAX_DISTANCE / max_exact) * (half - max_exact)).astype(I32)
    large = jnp.minimum(large, half - 1)
    return jnp.where(rel > 0, half, 0) + jnp.where(n < max_exact, n, large)


def _bias_table(rel_bias, nq, nk):
    rel = (jnp.arange(nk) - WINDOW)[None, :] - jnp.arange(nq)[:, None]
    return jnp.transpose(rel_bias.astype(F32)[_t5_bucket(rel)], (2, 0, 1))


def _router_kernel(x_ref, w_ref, b_ref, route_ref, cnt_ref, carry, *, n_fine):
    @pl.when(pl.program_id(0) == 0)
    def _():
        carry[...] = jnp.zeros_like(carry)

    logits = _bdot(x_ref[...].astype(BF16), w_ref[...]) + b_ref[...]
    tm = logits.shape[0]
    lane = lax.broadcasted_iota(I32, logits.shape, 1)
    lanef = lane.astype(F32)
    per_group = n_fine // E_GROUPS
    big = float(LANES)

    def first_max(mask):
        mx = jnp.max(jnp.where(mask, logits, -jnp.inf), axis=-1, keepdims=True)
        idx = jnp.min(jnp.where(mask & (logits == mx), lanef, big), axis=-1, keepdims=True)
        return mx, idx

    cmask = lane < E_GROUPS
    mc, grp = first_max(cmask)
    p_grp = 1.0 / jnp.sum(jnp.where(cmask, jnp.exp(logits - mc), 0.0), axis=-1, keepdims=True)
    lo = E_GROUPS + per_group * grp
    fmask = (lanef >= lo) & (lanef < lo + per_group)
    m1, i1 = first_max(fmask)
    sel0 = lanef == i1
    m2, i2 = first_max(fmask & jnp.logical_not(sel0))
    e = jnp.exp(m2 - m1)
    g0 = (1.0 / (1.0 + e)) * p_grp
    g1 = (e / (1.0 + e)) * p_grp

    j1 = i1 - lo
    j2 = i2 - lo
    swap = j1 > j2
    j_lo = jnp.minimum(j1, j2)
    j_hi = jnp.maximum(j1, j2)
    n_pairs_group = per_group * (per_group - 1) // 2
    bucket = grp * n_pairs_group + j_lo * (2 * per_group - 1 - j_lo) * 0.5 + (j_hi - j_lo - 1.0)
    g_lo = jnp.where(swap, g1, g0)
    g_hi = jnp.where(swap, g0, g1)

    sel = lanef == bucket
    cmat = jnp.where(sel, 1.0, 0.0)
    row = lax.broadcasted_iota(I32, (tm, tm), 0)
    col = lax.broadcasted_iota(I32, (tm, tm), 1)
    tril = jnp.where(col < row, 1.0, 0.0).astype(BF16)
    before = carry[...] + _bdot(tril, cmat.astype(BF16))
    rank = jnp.sum(jnp.where(sel, before, 0.0), axis=-1, keepdims=True)
    carry[...] = carry[...] + jnp.sum(cmat, axis=0, keepdims=True)
    cnt_ref[...] = carry[...]
    vals = (bucket, rank, g_lo, g_hi)
    route = jnp.zeros_like(logits)
    for j, val in enumerate(vals):
        route = jnp.where(lane == j, val, route)
    route_ref[...] = route


def _router(x, w_route, b_route, n_fine):
    t, d = x.shape
    tm = min(ROW_TILE, t)
    return pl.pallas_call(
        functools.partial(_router_kernel, n_fine=n_fine),
        grid=(t // tm,),
        in_specs=[pl.BlockSpec((tm, d), lambda i: (i, 0)), pl.BlockSpec((d, LANES), lambda i: (0, 0)),
                  pl.BlockSpec((1, LANES), lambda i: (0, 0))],
        out_specs=[pl.BlockSpec((tm, LANES), lambda i: (i, 0)), pl.BlockSpec((1, LANES), lambda i: (0, 0))],
        out_shape=[jax.ShapeDtypeStruct((t, LANES), F32), jax.ShapeDtypeStruct((1, LANES), F32)],
        scratch_shapes=[pltpu.VMEM((1, LANES), F32)],
        compiler_params=_params("arbitrary"),
        name="moe_router",
    )(x, w_route, b_route)


def _row_copy(src_ref, src_row, dst_ref, dst_row, sem):
    return pltpu.make_async_copy(src_ref.at[pl.ds(src_row, 1)], dst_ref.at[pl.ds(dst_row, 1)], sem)


def _dispatch_kernel(dest_ref, x_ref, xs_in_ref, xs_ref, sem, *, tm):
    del xs_in_ref

    def issue(r, carry):
        _row_copy(x_ref, r, xs_ref, dest_ref[0, r], sem).start()
        return carry

    def drain(r, carry):
        _row_copy(x_ref, r, xs_ref, dest_ref[0, r], sem).wait()
        return carry

    lax.fori_loop(0, tm, issue, 0, unroll=8)
    lax.fori_loop(0, tm, drain, 0, unroll=8)


def _dispatch(dest, x, n_rows):
    t, d = x.shape
    tm = min(ROW_TILE, t)
    return pl.pallas_call(
        functools.partial(_dispatch_kernel, tm=tm),
        grid=(t // tm,),
        in_specs=[pl.BlockSpec((1, tm), lambda i: (0, i), memory_space=pltpu.SMEM),
                  pl.BlockSpec((tm, d), lambda i: (i, 0)),
                  pl.BlockSpec(memory_space=pl.ANY)],
        out_specs=pl.BlockSpec(memory_space=pl.ANY),
        out_shape=jax.ShapeDtypeStruct((n_rows, d), F32),
        scratch_shapes=[pltpu.SemaphoreType.DMA(())],
        input_output_aliases={2: 0},
        compiler_params=_params("arbitrary"),
        name="moe_dispatch",
    )(dest, x, jnp.zeros((n_rows, d), F32))


def _expert_kernel(ea_ref, eb_ref, nu_ref, xs_ref, wga_ref, wua_ref, wda_ref, wgb_ref, wub_ref, wdb_ref, ys_ref):
    del ea_ref, eb_ref
    i = pl.program_id(0)
    d = xs_ref.shape[1]

    @pl.when(i < nu_ref[0])
    def _():
        xb = xs_ref[...].astype(BF16)
        for half, (wg_ref, wu_ref, wd_ref) in enumerate(((wga_ref, wua_ref, wda_ref), (wgb_ref, wub_ref, wdb_ref))):
            a = _bdot(xb, wg_ref[...])
            h = (a * _sigmoid(a)) * _bdot(xb, wu_ref[...])
            ys_ref[:, half * d:(half + 1) * d] = _bdot(h.astype(BF16), wd_ref[...])

    @pl.when(i >= nu_ref[0])
    def _():
        ys_ref[...] = jnp.zeros_like(ys_ref)


def _experts(block_ea, block_eb, n_used, xs, wg, wu, wd, blk):
    n_rows, d = xs.shape
    de = wg.shape[-1]
    low = lambda i, ea, eb, nu: (ea[i], 0, 0)
    high = lambda i, ea, eb, nu: (eb[i], 0, 0)
    grid_spec = pltpu.PrefetchScalarGridSpec(
        num_scalar_prefetch=3,
        grid=(n_rows // blk,),
        in_specs=[pl.BlockSpec((blk, d), lambda i, ea, eb, nu: (i, 0)),
                  pl.BlockSpec((None, d, de), low), pl.BlockSpec((None, d, de), low),
                  pl.BlockSpec((None, de, d), low),
                  pl.BlockSpec((None, d, de), high), pl.BlockSpec((None, d, de), high),
                  pl.BlockSpec((None, de, d), high)],
        out_specs=pl.BlockSpec((blk, 2 * d), lambda i, ea, eb, nu: (i, 0)),
    )
    return pl.pallas_call(
        _expert_kernel, grid_spec=grid_spec,
        out_shape=jax.ShapeDtypeStruct((n_rows, 2 * d), F32),
        compiler_params=_params("arbitrary"),
        name="moe_experts",
    )(block_ea, block_eb, n_used, xs, wg, wu, wd, wg, wu, wd)


def _combine_kernel(dest_ref, x_ref, gate_ref, g_ref, b_ref, ys_ref, o_ref, ybuf, sem, *, tm, alpha):
    def issue(r, carry):
        _row_copy(ys_ref, dest_ref[0, r], ybuf, r, sem).start()
        return carry

    def drain(r, carry):
        _row_copy(ys_ref, dest_ref[0, r], ybuf, r, sem).wait()
        return carry

    lax.fori_loop(0, tm, issue, 0, unroll=8)
    lax.fori_loop(0, tm, drain, 0, unroll=8)
    gate = gate_ref[...]
    d = x_ref.shape[1]
    y = alpha * x_ref[...] + gate[:, 0:1] * ybuf[:, :d] + gate[:, 1:2] * ybuf[:, d:]
    o_ref[...] = _layer_norm(y, g_ref[...], b_ref[...])


def _combine(dest, x, gates, ys, ln_g, ln_b, alpha):
    t, d = x.shape
    tm = min(ROW_TILE, t)
    return pl.pallas_call(
        functools.partial(_combine_kernel, tm=tm, alpha=alpha),
        grid=(t // tm,),
        in_specs=[pl.BlockSpec((1, tm), lambda i: (0, i), memory_space=pltpu.SMEM),
                  pl.BlockSpec((tm, d), lambda i: (i, 0)),
                  pl.BlockSpec((tm, 2), lambda i: (i, 0)),
                  pl.BlockSpec((1, d), lambda i: (0, 0)), pl.BlockSpec((1, d), lambda i: (0, 0)),
                  pl.BlockSpec(memory_space=pl.ANY)],
        out_specs=pl.BlockSpec((tm, d), lambda i: (i, 0)),
        out_shape=jax.ShapeDtypeStruct((t, d), F32),
        scratch_shapes=[pltpu.VMEM((tm, 2 * d), F32), pltpu.SemaphoreType.DMA(())],
        compiler_params=_params("arbitrary"),
        name="moe_combine_ln",
    )(dest, x, gates, ln_g.reshape(1, d), ln_b.reshape(1, d), ys)


def _moe_ln(x, w_c, b_c, w_f, b_f, wg, wu, wd, ln_g, ln_b, alpha):
    t, d = x.shape
    n_exp = w_f.shape[1]
    per_group = n_exp // E_GROUPS
    pad = LANES - E_GROUPS - n_exp
    w_route = jnp.concatenate([w_c.astype(F32), w_f.astype(F32), jnp.zeros((d, pad), F32)], axis=1)
    b_route = jnp.concatenate([b_c.astype(F32), b_f.astype(F32), jnp.zeros((pad,), F32)]).reshape(1, LANES)
    route, cnt = _router(x, w_route.astype(BF16), b_route, n_exp)
    bucket = route[:, 0].astype(I32)
    rank = route[:, 1].astype(I32)
    gates = route[:, 2:4]
    pairs = [(g * per_group + a, g * per_group + b) for g in range(E_GROUPS)
             for a in range(per_group) for b in range(a + 1, per_group)]
    n_buckets = len(pairs)
    assert n_buckets <= LANES
    blk = MOE_BLOCK if t >= 8 * MOE_BLOCK else 2 * SUBLANES
    counts = cnt[0, :n_buckets].astype(I32)
    padded = (counts + blk - 1) // blk * blk
    pend = jnp.cumsum(padded)
    poff = pend - padded
    ids = jnp.arange(n_buckets, dtype=I32)

    def lookup(table, idx):
        return jnp.sum(jnp.where(idx[:, None] == ids[None, :], table[None, :], 0), axis=1)

    dest = (lookup(poff, bucket) + rank).reshape(1, t)
    n_blocks = -(-t // blk) + n_buckets
    starts = jnp.arange(n_blocks, dtype=I32) * blk
    block_bucket = jnp.minimum(jnp.sum(pend[None, :] <= starts[:, None], axis=1), n_buckets - 1)
    pair_tab = jnp.array(pairs, dtype=I32)
    block_ea = lookup(pair_tab[:, 0], block_bucket)
    block_eb = lookup(pair_tab[:, 1], block_bucket)
    n_used = (pend[-1] // blk).astype(I32).reshape(1)
    xs = _dispatch(dest, x, n_blocks * blk)
    ys = _experts(block_ea, block_eb, n_used, xs, wg, wu, wd, blk)
    return _combine(dest, x, gates, ys, ln_g, ln_b, alpha)


def _head_sum(x, ones):
    hi = x.astype(BF16)
    lo = (x - hi.astype(F32)).astype(BF16)
    parts = []
    for j in range(x.shape[1] // LANES):
        sl = slice(j * LANES, (j + 1) * LANES)
        parts.append(_bdot(hi[:, sl], ones) + _bdot(lo[:, sl], ones))
    return jnp.concatenate(parts, axis=-1)


def _rw_prep_kernel(sh_ref, halo_ref, sh0_ref, mu_ref, w0_ref, w2_ref, a0_ref, a2_ref, g2_ref,
                    kk_ref, ka_ref, rk_ref, ones_ref,
                    r_o, w_o, k_o, v_o, kk_o, b_o, bonus_o, g_o, *, batch, cw):
    sh = sh_ref[...]
    first = jnp.where(pl.program_id(0) == 0, sh0_ref[...], halo_ref[...])
    prev = jnp.concatenate([first, sh[:sh.shape[0] - batch]], axis=0)
    shm = sh + (prev - sh) * mu_ref[...]
    r, k, v = shm[:, :cw], shm[:, cw:2 * cw], shm[:, 2 * cw:3 * cw]
    xwa = shm[:, 3 * cw:3 * cw + LANES]
    xg = shm[:, 3 * cw + LANES:]
    ones = ones_ref[...]
    z = w0_ref[...] + _bdot(jnp.tanh(xwa).astype(BF16), w2_ref[...])
    softplus = jnp.maximum(-z, 0.0) + jnp.log1p(jnp.exp(-jnp.abs(z)))
    w_o[...] = jnp.exp(-jnp.exp(-softplus - 0.5))
    a = _sigmoid(a0_ref[...] + _bdot(xwa.astype(BF16), a2_ref[...]))
    g_o[...] = _bdot(_sigmoid(xg).astype(BF16), g2_ref[...])
    kk = k * kk_ref[...]
    kk = kk / jnp.maximum(jnp.sqrt(_head_sum(kk * kk, ones)), 1e-12)
    k2 = k * (1.0 + (a - 1.0) * ka_ref[...])
    r_o[...] = r
    k_o[...] = k2
    v_o[...] = v
    kk_o[...] = kk
    b_o[...] = kk * a
    bonus_o[...] = _head_sum(r * k2 * rk_ref[...], ones) * v


def _rw_prep(sh, shift0, batch, prm):
    t, sw = sh.shape
    cw = prm["w0"].shape[-1]
    tm = min(256, t)
    hb = tm // batch
    row = lambda i: (i, 0)
    fix = lambda i: (0, 0)
    vec = lambda n: pl.BlockSpec((1, n), fix)
    return pl.pallas_call(
        functools.partial(_rw_prep_kernel, batch=batch, cw=cw),
        grid=(t // tm,),
        in_specs=[pl.BlockSpec((tm, sw), row),
                  pl.BlockSpec((batch, sw), lambda i: (jnp.maximum(i * hb - 1, 0), 0)),
                  pl.BlockSpec((batch, sw), fix), vec(sw), vec(cw),
                  pl.BlockSpec((LANES, cw), fix), vec(cw), pl.BlockSpec((LANES, cw), fix),
                  pl.BlockSpec((LANES, cw), fix), vec(cw), vec(cw), vec(cw),
                  pl.BlockSpec((LANES, LANES), fix)],
        out_specs=[pl.BlockSpec((tm, cw), row)] * 8,
        out_shape=[jax.ShapeDtypeStruct((t, cw), F32)] * 8,
        compiler_params=_params("parallel"),
        name="rwkv_prep",
    )(sh, sh, shift0, prm["mu"], prm["w0"], prm["w2"], prm["a0"], prm["a2"], prm["g2"],
      prm["k_k"], prm["k_a"], prm["r_k"], prm["ones"])


RW_TILES_A = 4
RW_TILES_B = 2
RW_TICK = 8


def _rwkv_kernel(r_ref, w_ref, k_ref, v_ref, kk_ref, b_ref, sa0_ref, sb0_ref, y_ref, san_ref, sbn_ref,
                 st_a, st_b, xa0, xb0, xa1, xb1, *, batch, steps):
    hd = HEAD_DIM
    hv = hd // 2
    ti = pl.program_id(0)

    @pl.when(ti == 0)
    def _():
        st_a[...] = sa0_ref[...]
        st_b[...] = sb0_ref[...]

    operands = (r_ref, w_ref, k_ref, v_ref, kk_ref, b_ref)
    def lane_of(rows):
        return lax.broadcasted_iota(I32, (rows, LANES), 1)

    lane_lo = lane_of(hd) < hd
    lane_lo_out = lane_of(batch) < hd
    even_quarter = (lane_of(hd) // hv) % 2 == 0
    quarter_v = lane_of(hv) // hv
    quarter_out = lane_of(batch) // hv

    def pick_quarter(q, blocks):
        return jnp.where(q == 0, blocks[0], jnp.where(q == 1, blocks[1], jnp.where(q == 2, blocks[2], blocks[3])))

    def padded(parts, rows):
        have = len(parts) * batch
        return parts + ([jnp.zeros((rows - have, LANES), F32)] if have < rows else [])

    def run(state, x, nv, tick):
        acc = jnp.zeros((nv, LANES), F32)
        for kidx in range(hd):
            acc = acc + state[kidx] * x[4, kidx:kidx + 1, :]
            if kidx % RW_TICK == RW_TICK - 1:
                tick()
        sa = acc
        vv = x[3, 0:nv, :]
        acc = jnp.zeros((nv, LANES), F32)
        for kidx in range(hd):
            row = slice(kidx, kidx + 1)
            sn = state[kidx] * x[1, row, :] - sa * x[5, row, :] + vv * x[2, row, :]
            state[kidx] = sn
            acc = acc + sn * x[0, row, :]
            if kidx % RW_TICK == RW_TICK - 1:
                tick()
        return acc

    def transpose_operand(o, r0, xa, xb):
        slab = operands[o][pl.ds(r0, batch), :]
        tiles = [slab[:, j * LANES:(j + 1) * LANES] for j in range(RW_TILES_A + RW_TILES_B)]
        ta = jnp.concatenate(padded(tiles[:RW_TILES_A], hd) * 2, axis=0).T
        xa[o] = jnp.where(lane_lo, ta[:hd], ta[hd:])
        tb = jnp.concatenate(padded(tiles[RW_TILES_A:], hv) * 4, axis=0).T
        if o == 3:
            xb[o, pl.ds(0, hv), :] = pick_quarter(quarter_v, (tb[:hv], tb[hd:hd + hv], tb[hv:hd], tb[hd + hv:]))
        else:
            xb[o] = jnp.where(even_quarter, tb[:hd], tb[hd:])

    def time_step(t, cur, nxt):
        r0 = pl.multiple_of(t * batch, batch)
        rn = pl.multiple_of(jnp.minimum(t + 1, steps - 1) * batch, batch)
        pending = [functools.partial(transpose_operand, o, rn, *nxt) for o in range(len(operands))]

        def tick():
            if pending:
                pending.pop(0)()

        ya = run(st_a, cur[0], hd, tick)
        yb = run(st_b, cur[1], hv, tick)
        while pending:
            tick()
        mta = jnp.concatenate([ya, ya], axis=0).T
        for j in range(RW_TILES_A):
            lo = mta[j * batch:(j + 1) * batch]
            hi = mta[hd + j * batch:hd + (j + 1) * batch]
            y_ref[pl.ds(r0, batch), j * LANES:(j + 1) * LANES] = jnp.where(lane_lo_out, lo, hi)
        mtb = jnp.concatenate([yb] * 4, axis=0).T
        for j in range(RW_TILES_B):
            blocks = [mtb[base + j * batch:base + (j + 1) * batch] for base in (0, hd, hv, hd + hv)]
            tile = RW_TILES_A + j
            y_ref[pl.ds(r0, batch), tile * LANES:(tile + 1) * LANES] = pick_quarter(quarter_out, blocks)

    even_bufs, odd_bufs = (xa0, xb0), (xa1, xb1)
    for o in range(len(operands)):
        transpose_operand(o, 0, *even_bufs)

    def step_pair(p, carry):
        time_step(2 * p, even_bufs, odd_bufs)
        time_step(2 * p + 1, odd_bufs, even_bufs)
        return carry

    lax.fori_loop(0, steps // 2, step_pair, 0)

    @pl.when(ti == pl.num_programs(0) - 1)
    def _():
        san_ref[...] = st_a[...]
        sbn_ref[...] = st_b[...]


def _rwkv(ops, s0, batch):
    t, cw = ops[0].shape
    hd, hv = HEAD_DIM, HEAD_DIM // 2
    assert cw == (RW_TILES_A + RW_TILES_B) * LANES and RW_TILES_A * batch <= hd
    ca, cb = RW_TILES_A * batch, RW_TILES_B * batch
    if s0 is None:
        sa0 = jnp.zeros((hd, hd, LANES), F32)
        sb0 = jnp.zeros((hd, hv, LANES), F32)
    else:
        s0 = s0.astype(F32)
        sa0 = s0[:, :2 * RW_TILES_A].reshape(batch, RW_TILES_A, 2, hd, hd).transpose(4, 3, 2, 1, 0)
        sa0 = jnp.pad(sa0.reshape(hd, hd, 2, ca), ((0, 0),) * 3 + ((0, hd - ca),)).reshape(hd, hd, LANES)
        sb0 = s0[:, 2 * RW_TILES_A:].reshape(batch, RW_TILES_B, 2, 2, hv, hd).transpose(5, 4, 3, 2, 1, 0)
        sb0 = jnp.pad(sb0.reshape(hd, hv, 2, 2, cb), ((0, 0),) * 4 + ((0, hv - cb),)).reshape(hd, hv, LANES)
    steps = min(16, t // batch)
    assert steps % 2 == 0, "the time loop is unrolled in even/odd operand-buffer pairs"
    rows = steps * batch
    blk = pl.BlockSpec((rows, cw), lambda i: (i, 0))
    st_a = pl.BlockSpec((hd, hd, LANES), lambda i: (0, 0, 0))
    st_b = pl.BlockSpec((hd, hv, LANES), lambda i: (0, 0, 0))
    y, san, sbn = pl.pallas_call(
        functools.partial(_rwkv_kernel, batch=batch, steps=steps),
        grid=(t // rows,),
        in_specs=[blk] * 6 + [st_a, st_b],
        out_specs=[blk, st_a, st_b],
        out_shape=[jax.ShapeDtypeStruct((t, cw), F32), jax.ShapeDtypeStruct((hd, hd, LANES), F32),
                   jax.ShapeDtypeStruct((hd, hv, LANES), F32)],
        scratch_shapes=[pltpu.VMEM((hd, hd, LANES), F32), pltpu.VMEM((hd, hv, LANES), F32),
                        ] + [pltpu.VMEM((6, hd, LANES), F32)] * 4,
        compiler_params=_params("arbitrary"),
        name="rwkv_recurrence",
    )(*ops, sa0, sb0)
    sa = san.reshape(hd, hd, 2, hd)[..., :ca].reshape(hd, hd, 2, RW_TILES_A, batch).transpose(4, 3, 2, 1, 0)
    sb = sbn.reshape(hd, hv, 2, 2, hv)[..., :cb].reshape(hd, hv, 2, 2, RW_TILES_B, batch).transpose(5, 4, 3, 2, 1, 0)
    s_last = jnp.concatenate([sa.reshape(batch, 2 * RW_TILES_A, hd, hd), sb.reshape(batch, 2 * RW_TILES_B, hd, hd)],
                             axis=1)
    return y, s_last


def _rw_post_kernel(y_ref, bonus_ref, g_ref, lg_ref, lb_ref, ones_ref, o_ref):
    y = y_ref[...]
    ones = ones_ref[...]
    mu = _head_sum(y, ones) * (1.0 / HEAD_DIM)
    d = y - mu
    var = _head_sum(d * d, ones) * (1.0 / HEAD_DIM)
    yn = d * lax.rsqrt(var + GN_EPS) * lg_ref[...] + lb_ref[...]
    o_ref[...] = (yn + bonus_ref[...]) * g_ref[...]


def _rw_post(y, bonus, g, ln_g, ln_b, ones):
    t, cw = y.shape
    tm = min(ROW_TILE, t)
    row = pl.BlockSpec((tm, cw), lambda i: (i, 0))
    vec = pl.BlockSpec((1, cw), lambda i: (0, 0))
    return pl.pallas_call(
        _rw_post_kernel, grid=(t // tm,),
        in_specs=[row, row, row, vec, vec, pl.BlockSpec((LANES, LANES), lambda i: (0, 0))],
        out_specs=row, out_shape=jax.ShapeDtypeStruct((t, cw), F32),
        compiler_params=_params("parallel"),
        name="rwkv_post",
    )(y, bonus, g, ln_g, ln_b, ones)


def _pool_kernel(main_ref, halo_ref, w_ref, scale_ref, o_ref, *, batch, pos0):
    tm = main_ref.shape[0]
    full = jnp.concatenate([main_ref[...], halo_ref[...]], axis=0)
    b = batch
    n = full.shape[0]
    a2 = full[b:] + full[:n - b]
    a4 = a2[2 * b:] + a2[:n - 3 * b]
    a8 = a4[4 * b:] + a4[:n - 7 * b]
    a16 = a8[8 * b:] + a8[:n - 15 * b]
    cur = full[16 * b:16 * b + tm]
    sums = (a2[15 * b:15 * b + tm], a4[13 * b:13 * b + tm], a8[9 * b:9 * b + tm], a16[b:b + tm])
    lane = lax.broadcasted_iota(I32, cur.shape, 1)
    grp = lane // (cur.shape[1] // len(POOL_WINDOWS))
    win_sum = sums[-1]
    width = jnp.full(cur.shape, float(POOL_WINDOWS[-1]), F32)
    for j in range(len(POOL_WINDOWS) - 2, -1, -1):
        win_sum = jnp.where(grp == j, sums[j], win_sum)
        width = jnp.where(grp == j, float(POOL_WINDOWS[j]), width)
    row = lax.broadcasted_iota(I32, cur.shape, 0)
    t_idx = pl.program_id(0) * (tm // b) + row // b
    cnt = jnp.minimum(width, (pos0 + t_idx + 1).astype(F32))
    pooled = win_sum / cnt - cur
    o_ref[...] = _bdot(pooled.astype(BF16), w_ref[...]) * scale_ref[...]


def _pool(fp, t, batch, w_bd, scale, pos0):
    dw = fp.shape[1]
    tm = min(ROW_TILE, t)
    halo = 16 * batch
    return pl.pallas_call(
        functools.partial(_pool_kernel, batch=batch, pos0=pos0),
        grid=(t // tm,),
        in_specs=[pl.BlockSpec((tm, dw), lambda i: (i, 0)),
                  pl.BlockSpec((halo, dw), lambda i: ((i + 1) * (tm // halo), 0)),
                  pl.BlockSpec((dw, dw), lambda i: (0, 0)), pl.BlockSpec((1, dw), lambda i: (0, 0))],
        out_specs=pl.BlockSpec((tm, dw), lambda i: (i, 0)),
        out_shape=jax.ShapeDtypeStruct((t, dw), F32),
        compiler_params=_params("parallel"),
        name="causal_pool",
    )(fp, fp, w_bd, scale)


def _to_batch_major(a, seq, batch):
    return a.reshape(seq, batch, a.shape[-1]).transpose(1, 0, 2)


def _to_time_major(a):
    bsz, seq, width = a.shape
    return a.transpose(1, 0, 2).reshape(seq * bsz, width)


def _even_mixer(x, batch, seq, h0, cache_k, cache_v, w, j, bias, alpha, ln_g, ln_b):
    a_width = w["s5_w_glu"].shape[-1]
    w_in = w["ev_w_in"][j]
    qw = a_width
    kvw = (w_in.shape[1] - 2 * a_width) // 2
    u, q, k, v = _matmul([x], [w_in.astype(BF16)], splits=(a_width, qw, kvw, kvw))
    prm = _s5_params(w["s5_a_re"][j], w["s5_a_im"][j], w["s5_log_dt"][j], w["s5_b_re"][j], w["s5_b_im"][j],
                     w["s5_c_re"][j], w["s5_c_im"][j], w["s5_d"][j])
    nt = a_width // LANES
    ns = prm["bre"].shape[-1]
    if h0 is None:
        h0re = jnp.zeros((nt, batch, ns), F32)
        h0im = h0re
    else:
        h0re = h0[..., 0].astype(F32).reshape(batch, nt, ns).transpose(1, 0, 2)
        h0im = h0[..., 1].astype(F32).reshape(batch, nt, ns).transpose(1, 0, 2)
    g, hn_re, hn_im = _s5(u, batch, h0re, h0im, prm)
    groups, n_state = w["s5_a_re"][j].shape
    h_last = jnp.stack([hn_re.transpose(1, 0, 2).reshape(batch, groups, n_state),
                        hn_im.transpose(1, 0, 2).reshape(batch, groups, n_state)], axis=-1)
    a_out = _matmul([g], [w["s5_w_glu"][j].astype(BF16)], epilogue="glu")

    sinks = w["swa_sinks"][j].astype(F32)
    if cache_k is None:
        att = _attention(q, k, v, bias, sinks, batch, CHUNK, CHUNK, WINDOW // CHUNK + 1, True)
        k_all, v_all = k, v
    else:
        k_all = jnp.concatenate([_to_time_major(cache_k.astype(F32).reshape(batch, WINDOW, kvw)), k], axis=0)
        v_all = jnp.concatenate([_to_time_major(cache_v.astype(F32).reshape(batch, WINDOW, kvw)), v], axis=0)
        att = _attention(q, k_all, v_all, bias, sinks, batch, seq, WINDOW + seq, 1, False)
    keep = k_all.shape[0] - WINDOW * batch
    new_k = _to_batch_major(k_all[keep:], WINDOW, batch)
    new_v = _to_batch_major(v_all[keep:], WINDOW, batch)
    w_out = w["ev_w_out"][j].astype(BF16)
    xn = _matmul([a_out, att], [w_out[:a_width], w_out[a_width:]], epilogue="ln",
                 res=x, g=ln_g, b=ln_b, alpha=alpha)
    kv_shape = (batch, WINDOW, kvw // HEAD_DIM, HEAD_DIM)
    return xn, h_last, new_k.reshape(kv_shape), new_v.reshape(kv_shape)


def _odd_mixer(x, batch, seq, shift0, s0, pool0, pos0, w, j, alpha, ln_g, ln_b):
    t = x.shape[0]
    w_in = w["od_w_in"][j]
    cw = w["rw_w0"].shape[-1]
    sw = w["rw_mu"].shape[-1]
    dw = w_in.shape[1] - sw
    n_heads = cw // HEAD_DIM
    sh, p = _matmul([x], [w_in.astype(BF16)], splits=(sw, dw))
    zeros_half = jnp.zeros((LANES - w["rw_w2"].shape[1], cw), F32)
    head_lane = jnp.arange(LANES) // HEAD_DIM
    ones = (head_lane[:, None] == head_lane[None, :]).astype(BF16)
    prm = dict(
        mu=w["rw_mu"][j].astype(F32).reshape(1, sw), w0=w["rw_w0"][j].astype(F32).reshape(1, cw),
        w2=jnp.concatenate([w["rw_w2"][j].astype(F32), zeros_half], axis=0).astype(BF16),
        a0=w["rw_a0"][j].astype(F32).reshape(1, cw),
        a2=jnp.concatenate([zeros_half, w["rw_a2"][j].astype(F32)], axis=0).astype(BF16),
        g2=w["rw_g2"][j].astype(BF16),
        k_k=w["rw_k_k"][j].astype(F32).reshape(1, cw), k_a=w["rw_k_a"][j].astype(F32).reshape(1, cw),
        r_k=w["rw_r_k"][j].astype(F32).reshape(1, cw), ones=ones)
    if shift0 is None:
        shift0 = jnp.zeros((batch, sw), F32)
    r, dec, k2, v, kk, b, bonus, g = _rw_prep(sh, shift0.astype(F32), batch, prm)

    y, s_last = _rwkv((r, dec, k2, v, kk, b), s0, batch)
    c_out = _rw_post(y, bonus, g, w["rw_ln_g"][j].astype(F32).reshape(1, cw),
                     w["rw_ln_b"][j].astype(F32).reshape(1, cw), ones)

    if pool0 is None:
        hist = jnp.zeros((POOL_HIST * batch, dw), F32)
    else:
        hist = _to_time_major(pool0.astype(F32))
    fp = jnp.concatenate([jnp.zeros((batch, dw), F32), hist, p], axis=0)
    n_grp = len(POOL_WINDOWS)
    pg = dw // n_grp
    eye = jnp.eye(n_grp, dtype=F32)
    w_bd = (w["pool_w"][j].astype(F32)[:, :, None, :] * eye[:, None, :, None]).reshape(dw, dw).astype(BF16)
    d_out = _pool(fp, t, batch, w_bd, w["pool_scale"][j].astype(F32).reshape(1, dw), pos0)
    pool_new = _to_batch_major(fp[fp.shape[0] - POOL_HIST * batch:], POOL_HIST, batch)

    w_out = w["od_w_out"][j].astype(BF16)
    xn = _matmul([c_out, d_out], [w_out[:cw], w_out[cw:]], epilogue="ln", res=x, g=ln_g, b=ln_b, alpha=alpha)
    sh_last = sh[t - batch:]
    return xn, sh_last, s_last, pool_new


def _trunk(x3, s5_0, k_0, v_0, rw_0, shift_0, pool_0, pos0, w):
    batch, seq, _ = x3.shape
    depth = w["ln1_g"].shape[0]
    alpha = (2 * depth) ** 0.25
    x = _to_time_major(x3.astype(F32))
    prompt = s5_0 is None
    if prompt:
        bias = _bias_table(w["rel_bias"], CHUNK, (WINDOW // CHUNK + 1) * CHUNK)
    else:
        bias = _bias_table(w["rel_bias"], seq, WINDOW + seq)
    s5_new, k_new, v_new, rw_new, sh_new, pool_new = [], [], [], [], [], []
    for i in range(depth):
        j = i // 2
        if i % 2 == 0:
            x, h1, nk, nv = _even_mixer(
                x, batch, seq, None if prompt else s5_0[j], None if prompt else k_0[j],
                None if prompt else v_0[j], w, j, bias, alpha, w["ln1_g"][i], w["ln1_b"][i])
            s5_new.append(h1)
            k_new.append(nk)
            v_new.append(nv)
        else:
            x, sh1, s1, pl1 = _odd_mixer(
                x, batch, seq, None if prompt else shift_0[j], None if prompt else rw_0[j],
                None if prompt else pool_0[j], pos0, w, j, alpha, w["ln1_g"][i], w["ln1_b"][i])
            sh_new.append(sh1)
            rw_new.append(s1)
            pool_new.append(pl1)
        x = _moe_ln(x, w["moe_w_coarse"][i], w["moe_b_coarse"][i], w["moe_w_fine"][i], w["moe_b_fine"][i],
                    w["moe_w_gate"][i].astype(BF16), w["moe_w_up"][i].astype(BF16),
                    w["moe_w_down"][i].astype(BF16), w["ln2_g"][i], w["ln2_b"][i], alpha)
    y = _to_batch_major(x, seq, batch)
    return (y, jnp.stack(s5_new), jnp.stack(k_new), jnp.stack(v_new),
            jnp.stack(rw_new), jnp.stack(sh_new), jnp.stack(pool_new))


def kernel(x_prompt, x_sample, state_s5, cache_swa_k, cache_swa_v, state_rwkv, state_shift, state_pool, rel_bias, ev_w_in, ev_w_out, s5_a_re, s5_a_im, s5_log_dt, s5_b_re, s5_b_im, s5_c_re, s5_c_im, s5_d, s5_w_glu, swa_sinks, od_w_in, od_w_out, rw_mu, rw_w0, rw_w2, rw_a0, rw_a2, rw_g2, rw_k_k, rw_k_a, rw_r_k, rw_ln_g, rw_ln_b, pool_w, pool_scale, ln1_g, ln1_b, ln2_g, ln2_b, moe_w_coarse, moe_b_coarse, moe_w_fine, moe_b_fine, moe_w_gate, moe_w_up, moe_w_down):
    w = dict(rel_bias=rel_bias, ev_w_in=ev_w_in, ev_w_out=ev_w_out, s5_a_re=s5_a_re, s5_a_im=s5_a_im,
             s5_log_dt=s5_log_dt, s5_b_re=s5_b_re, s5_b_im=s5_b_im, s5_c_re=s5_c_re, s5_c_im=s5_c_im,
             s5_d=s5_d, s5_w_glu=s5_w_glu, swa_sinks=swa_sinks, od_w_in=od_w_in, od_w_out=od_w_out,
             rw_mu=rw_mu, rw_w0=rw_w0, rw_w2=rw_w2, rw_a0=rw_a0, rw_a2=rw_a2, rw_g2=rw_g2, rw_k_k=rw_k_k,
             rw_k_a=rw_k_a, rw_r_k=rw_r_k, rw_ln_g=rw_ln_g, rw_ln_b=rw_ln_b, pool_w=pool_w,
             pool_scale=pool_scale, ln1_g=ln1_g, ln1_b=ln1_b, ln2_g=ln2_g, ln2_b=ln2_b,
             moe_w_coarse=moe_w_coarse, moe_b_coarse=moe_b_coarse, moe_w_fine=moe_w_fine,
             moe_b_fine=moe_b_fine, moe_w_gate=moe_w_gate, moe_w_up=moe_w_up, moe_w_down=moe_w_down)
    y_s, s5_s, k_s, v_s, rw_s, sh_s, pool_s = _trunk(
        x_sample, state_s5, cache_swa_k, cache_swa_v, state_rwkv, state_shift, state_pool, PAST_LEN, w)
    y_p, s5_p, k_p, v_p, rw_p, sh_p, pool_p = _trunk(x_prompt, None, None, None, None, None, None, 0, w)
    return (y_p, y_s, s5_p, s5_s, k_p, v_p, k_s, v_s, rw_p, rw_s, sh_p, sh_s, pool_p, pool_s)
```

```python
import functools
import math

import jax
import jax.numpy as jnp
from jax import lax
from jax.experimental import pallas as pl
from jax.experimental.pallas import tpu as pltpu

F32 = jnp.float32
BF16 = jnp.bfloat16
I32 = jnp.int32

CHUNK = 64
WINDOW = 128
HEAD_DIM = 64
PAST_LEN = 1024
N_BUCKETS = 32
MAX_DISTANCE = 128
POOL_WINDOWS = (2, 4, 8, 16)
POOL_HIST = max(POOL_WINDOWS) - 1
E_GROUPS = 4
LN_EPS = 1e-5
GN_EPS = 64e-5
A_GROUP = 16

LANES = 128
SUBLANES = 8
VMEM_LIMIT_BYTES = 56 * 1024 * 1024

ROW_TILE = 512
MOE_BLOCK = 256


def _params(*sem):
    return pltpu.CompilerParams(dimension_semantics=sem, vmem_limit_bytes=VMEM_LIMIT_BYTES)


def _sigmoid(x):
    return 1.0 / (1.0 + jnp.exp(-x))


def _layer_norm(y, g, b):
    mu = jnp.mean(y, axis=-1, keepdims=True)
    d = y - mu
    var = jnp.mean(d * d, axis=-1, keepdims=True)
    return d * lax.rsqrt(var + LN_EPS) * g + b


def _bdot(a, b):
    return jnp.dot(a, b, preferred_element_type=F32)


def _mm_kernel(*refs, n_in, splits, epilogue, alpha):
    xs = refs[:n_in]
    ws = refs[n_in:2 * n_in]
    pos = 2 * n_in
    acc = None
    for x_ref, w_ref in zip(xs, ws):
        d = _bdot(x_ref[...].astype(BF16), w_ref[...])
        acc = d if acc is None else acc + d
    if epilogue == "ln":
        res_ref, g_ref, b_ref = refs[pos:pos + 3]
        pos += 3
        acc = _layer_norm(alpha * res_ref[...] + acc, g_ref[...], b_ref[...])
    elif epilogue == "glu":
        acc = xs[0][...] * _sigmoid(acc)
    off = 0
    for o_ref, n in zip(refs[pos:], splits):
        o_ref[...] = acc[:, off:off + n]
        off += n


def _matmul(xs, ws, splits=None, epilogue=None, res=None, g=None, b=None, alpha=None):
    t = xs[0].shape[0]
    n = ws[0].shape[1]
    splits = tuple(splits) if splits else (n,)
    tm = min(ROW_TILE, t)
    in_specs = [pl.BlockSpec((tm, x.shape[1]), lambda i: (i, 0)) for x in xs]
    in_specs += [pl.BlockSpec(w.shape, lambda i: (0, 0)) for w in ws]
    args = list(xs) + list(ws)
    if epilogue == "ln":
        in_specs += [pl.BlockSpec((tm, n), lambda i: (i, 0)),
                     pl.BlockSpec((1, n), lambda i: (0, 0)), pl.BlockSpec((1, n), lambda i: (0, 0))]
        args += [res, g.reshape(1, n), b.reshape(1, n)]
    outs = pl.pallas_call(
        functools.partial(_mm_kernel, n_in=len(xs), splits=splits, epilogue=epilogue, alpha=alpha),
        grid=(t // tm,),
        in_specs=in_specs,
        out_specs=[pl.BlockSpec((tm, s), lambda i: (i, 0)) for s in splits],
        out_shape=[jax.ShapeDtypeStruct((t, s), F32) for s in splits],
        compiler_params=_params("parallel"),
        name="matmul_" + (epilogue or "plain"),
    )(*args)
    return outs if len(splits) > 1 else outs[0]


def _s5_kernel(u_ref, bre_ref, bim_ref, cre_ref, cim_ref, are_ref, aim_ref, d_ref, h0re_ref, h0im_ref,
               g_ref, hnre_ref, hnim_ref, hre, him, car_re, car_im, *, batch, steps):
    ti = pl.program_id(1)

    @pl.when(ti == 0)
    def _():
        car_re[...] = h0re_ref[...]
        car_im[...] = h0im_ref[...]

    u = u_ref[...]
    ub = u.astype(BF16)
    hre[...] = _bdot(ub, bre_ref[...])
    him[...] = _bdot(ub, bim_ref[...])
    a_re = jnp.broadcast_to(are_ref[...], car_re.shape)
    a_im = jnp.broadcast_to(aim_ref[...], car_re.shape)

    def step(t, carry):
        h_re, h_im = carry
        r0 = pl.multiple_of(t * batch, batch)
        n_re = a_re * h_re - a_im * h_im + hre[pl.ds(r0, batch), :]
        n_im = a_re * h_im + a_im * h_re + him[pl.ds(r0, batch), :]
        hre[pl.ds(r0, batch), :] = n_re
        him[pl.ds(r0, batch), :] = n_im
        return n_re, n_im

    h_re, h_im = lax.fori_loop(0, steps, step, (car_re[...], car_im[...]))
    car_re[...] = h_re
    car_im[...] = h_im
    hnre_ref[...] = h_re
    hnim_ref[...] = h_im
    y = (_bdot(hre[...].astype(BF16), cre_ref[...]) - _bdot(him[...].astype(BF16), cim_ref[...])
         + d_ref[...] * u)
    cdf = 0.5 * (1.0 + jnp.tanh(math.sqrt(2.0 / math.pi) * (y + 0.044715 * (y * y * y))))
    g_ref[...] = y * cdf


def _s5(u, batch, h0re, h0im, prm):
    t, width = u.shape
    nt = width // LANES
    ns = prm["bre"].shape[-1]
    rows = min(1024, t)
    steps = rows // batch
    tile = lambda j, i: (j, 0, 0)
    return pl.pallas_call(
        functools.partial(_s5_kernel, batch=batch, steps=steps),
        grid=(nt, t // rows),
        in_specs=[pl.BlockSpec((rows, LANES), lambda j, i: (i, j)),
                  pl.BlockSpec((None, LANES, ns), tile), pl.BlockSpec((None, LANES, ns), tile),
                  pl.BlockSpec((None, ns, LANES), tile), pl.BlockSpec((None, ns, LANES), tile),
                  pl.BlockSpec((None, 1, ns), tile), pl.BlockSpec((None, 1, ns), tile),
                  pl.BlockSpec((None, 1, LANES), tile),
                  pl.BlockSpec((None, batch, ns), tile), pl.BlockSpec((None, batch, ns), tile)],
        out_specs=[pl.BlockSpec((rows, LANES), lambda j, i: (i, j)),
                   pl.BlockSpec((None, batch, ns), tile), pl.BlockSpec((None, batch, ns), tile)],
        out_shape=[jax.ShapeDtypeStruct((t, width), F32),
                   jax.ShapeDtypeStruct((nt, batch, ns), F32), jax.ShapeDtypeStruct((nt, batch, ns), F32)],
        scratch_shapes=[pltpu.VMEM((rows, ns), F32), pltpu.VMEM((rows, ns), F32),
                        pltpu.VMEM((batch, ns), F32), pltpu.VMEM((batch, ns), F32)],
        compiler_params=_params("parallel", "arbitrary"),
        name="s5_scan",
    )(u, prm["bre"], prm["bim"], prm["cre"], prm["cim"], prm["are"], prm["aim"], prm["d"], h0re, h0im)


def _s5_params(a_re, a_im, log_dt, b_re, b_im, c_re, c_im, d_skip):
    groups, n_state = a_re.shape
    gpt = LANES // A_GROUP
    nt = groups // gpt
    lam = lax.complex(a_re.astype(F32), a_im.astype(F32))
    dt = jnp.exp(log_dt.astype(F32))[:, None]
    a_bar = jnp.exp(lam * dt)
    b_bar = ((a_bar - 1.0) / lam)[..., None] * lax.complex(b_re.astype(F32), b_im.astype(F32))
    eye = jnp.eye(gpt, dtype=F32)

    def in_proj(m):
        m = m.reshape(nt, gpt, n_state, A_GROUP).transpose(0, 1, 3, 2)
        return (m[:, :, :, None, :] * eye[None, :, None, :, None]).reshape(nt, gpt * A_GROUP, gpt * n_state)

    def out_proj(m):
        m = m.astype(F32).reshape(nt, gpt, A_GROUP, n_state).transpose(0, 1, 3, 2)
        return (m[:, :, :, None, :] * eye[None, :, None, :, None]).reshape(nt, gpt * n_state, gpt * A_GROUP)

    return dict(bre=in_proj(b_bar.real).astype(BF16), bim=in_proj(b_bar.imag).astype(BF16),
                cre=out_proj(c_re).astype(BF16), cim=out_proj(c_im).astype(BF16),
                are=a_bar.real.reshape(nt, 1, gpt * n_state), aim=a_bar.imag.reshape(nt, 1, gpt * n_state),
                d=d_skip.astype(F32).reshape(nt, 1, LANES))


def _attn_kernel(*refs, n_kv, masked, tiles_per_kv, batch):
    n_q = 2 * tiles_per_kv
    q_refs = refs[:n_q]
    k_refs = refs[n_q:n_q + n_kv]
    v_refs = refs[n_q + n_kv:n_q + 2 * n_kv]
    bias_ref, sink_ref, o_ref, o_tiles = refs[n_q + 2 * n_kv:]
    c = pl.program_id(0)
    cq = q_refs[0].shape[0] // batch
    ck = k_refs[0].shape[0] // batch
    nk = n_kv * ck
    low_half = lax.broadcasted_iota(I32, (nk, LANES), 1) < HEAD_DIM
    if masked:
        key = lax.broadcasted_iota(I32, (nk, tiles_per_kv * cq), 0)
        valid = (c - (n_kv - 1)) * CHUNK + key >= 0

    def one_sequence(b, carry):
        kt = jnp.concatenate([r[pl.ds(b, ck, stride=batch), :] for r in k_refs], axis=0)
        vt = jnp.concatenate([r[pl.ds(b, ck, stride=batch), :] for r in v_refs], axis=0)
        k_sw = pltpu.roll(kt, HEAD_DIM, 1)
        v_sw = pltpu.roll(vt, HEAD_DIM, 1)

        def placed(t, t_sw, kvh, half):
            src = t if kvh == half else t_sw
            keep = low_half if half == 0 else jnp.logical_not(low_half)
            return jnp.where(keep, src, 0.0).astype(BF16)

        scores = []
        for kvh in range(2):
            tiles = [q_refs[kvh * tiles_per_kv + i][pl.ds(b, cq, stride=batch), :] for i in range(tiles_per_kv)]
            qs = jnp.concatenate(tiles, axis=0).astype(BF16)
            for half in range(2):
                s = lax.dot_general(placed(kt, k_sw, kvh, half), qs, (((1,), (1,)), ((), ())),
                                    preferred_element_type=F32)
                s = s * (HEAD_DIM ** -0.5) + bias_ref[2 * kvh + half]
                if masked:
                    s = jnp.where(valid, s, -jnp.inf)
                scores.append(s)
        probs = []
        for grp, s in enumerate(scores):
            sink = sink_ref[grp]
            m = jnp.maximum(jnp.max(s, axis=0, keepdims=True), sink)
            p = jnp.exp(s - m)
            denom = jnp.sum(p, axis=0, keepdims=True) + jnp.exp(sink - m)
            probs.append((p / denom).astype(BF16))
        tn = (((0,), (0,)), ((), ()))
        for kvh in range(2):
            o_t = (lax.dot_general(placed(vt, v_sw, kvh, 0), probs[2 * kvh], tn, preferred_element_type=F32)
                   + lax.dot_general(placed(vt, v_sw, kvh, 1), probs[2 * kvh + 1], tn, preferred_element_type=F32))
            o = o_t.T
            for i in range(tiles_per_kv):
                tile = kvh * tiles_per_kv + i
                o_tiles[tile, pl.ds(b, cq, stride=batch), :] = o[i * cq:(i + 1) * cq]
        return carry

    lax.fori_loop(0, batch, one_sequence, 0, unroll=2)
    for tile in range(n_q):
        o_ref[:, tile * LANES:(tile + 1) * LANES] = o_tiles[tile]


def _attention(q, k, v, bias, sinks, batch, cq, ck, n_kv, masked):
    tq, qw = q.shape
    kw = k.shape[-1]
    n_heads = qw // HEAD_DIM
    assert kw == LANES and n_heads % 4 == 0, "two kv heads in one lane tile, an even number of query tiles each"
    tiles_per_kv = n_heads // 4
    heads = [[2 * (kvh * tiles_per_kv + i) + half for i in range(tiles_per_kv)]
             for kvh in range(2) for half in range(2)]
    bias_g = jnp.stack([jnp.concatenate([bias[h].T for h in hs], axis=1) for hs in heads])
    sink_g = jnp.stack([jnp.concatenate([jnp.ones((1, cq), F32) * sinks[h] for h in hs], axis=1)
                        for hs in heads])

    def kv_spec(s):
        return pl.BlockSpec((ck * batch, kw), lambda c: (jnp.maximum(c - (n_kv - 1) + s, 0), 0))

    return pl.pallas_call(
        functools.partial(_attn_kernel, n_kv=n_kv, masked=masked, tiles_per_kv=tiles_per_kv, batch=batch),
        grid=(tq // (cq * batch),),
        in_specs=[pl.BlockSpec((cq * batch, LANES), lambda c, j=j: (c, j)) for j in range(qw // LANES)]
        + [kv_spec(s) for s in range(n_kv)] + [kv_spec(s) for s in range(n_kv)]
        + [pl.BlockSpec(bias_g.shape, lambda c: (0, 0, 0)), pl.BlockSpec(sink_g.shape, lambda c: (0, 0, 0))],
        out_specs=pl.BlockSpec((cq * batch, qw), lambda c: (c, 0)),
        out_shape=jax.ShapeDtypeStruct((tq, qw), F32),
        scratch_shapes=[pltpu.VMEM((qw // LANES, cq * batch, LANES), F32)],
        compiler_params=_params("parallel"),
        name="swa_attention",
    )(*([q] * (qw // LANES)), *([k] * n_kv), *([v] * n_kv), bias_g, sink_g)


def _t5_bucket(rel):
    half = N_BUCKETS // 2
    max_exact = half // 2
    n = jnp.abs(rel)
    large = max_exact + (jnp.log(jnp.maximum(n, 1).astype(F32) / max_exact)
                         / math.log(MAX_DISTANCE / max_exact) * (half - max_exact)).astype(I32)
    large = jnp.minimum(large, half - 1)
    return jnp.where(rel > 0, half, 0) + jnp.where(n < max_exact, n, large)


def _bias_table(rel_bias, nq, nk):
    rel = (jnp.arange(nk) - WINDOW)[None, :] - jnp.arange(nq)[:, None]
    return jnp.transpose(rel_bias.astype(F32)[_t5_bucket(rel)], (2, 0, 1))


def _router_kernel(x_ref, w_ref, b_ref, route_ref, cnt_ref, carry, *, n_fine):
    @pl.when(pl.program_id(0) == 0)
    def _():
        carry[...] = jnp.zeros_like(carry)

    logits = _bdot(x_ref[...].astype(BF16), w_ref[...]) + b_ref[...]
    tm = logits.shape[0]
    lane = lax.broadcasted_iota(I32, logits.shape, 1)
    lanef = lane.astype(F32)
    per_group = n_fine // E_GROUPS
    big = float(LANES)

    def first_max(mask):
        mx = jnp.max(jnp.where(mask, logits, -jnp.inf), axis=-1, keepdims=True)
        idx = jnp.min(jnp.where(mask & (logits == mx), lanef, big), axis=-1, keepdims=True)
        return mx, idx

    cmask = lane < E_GROUPS
    mc, grp = first_max(cmask)
    p_grp = 1.0 / jnp.sum(jnp.where(cmask, jnp.exp(logits - mc), 0.0), axis=-1, keepdims=True)
    lo = E_GROUPS + per_group * grp
    fmask = (lanef >= lo) & (lanef < lo + per_group)
    m1, i1 = first_max(fmask)
    sel0 = lanef == i1
    m2, i2 = first_max(fmask & jnp.logical_not(sel0))
    e = jnp.exp(m2 - m1)
    g0 = (1.0 / (1.0 + e)) * p_grp
    g1 = (e / (1.0 + e)) * p_grp

    j1 = i1 - lo
    j2 = i2 - lo
    swap = j1 > j2
    j_lo = jnp.minimum(j1, j2)
    j_hi = jnp.maximum(j1, j2)
    n_pairs_group = per_group * (per_group - 1) // 2
    bucket = grp * n_pairs_group + j_lo * (2 * per_group - 1 - j_lo) * 0.5 + (j_hi - j_lo - 1.0)
    g_lo = jnp.where(swap, g1, g0)
    g_hi = jnp.where(swap, g0, g1)

    sel = lanef == bucket
    cmat = jnp.where(sel, 1.0, 0.0)
    row = lax.broadcasted_iota(I32, (tm, tm), 0)
    col = lax.broadcasted_iota(I32, (tm, tm), 1)
    tril = jnp.where(col < row, 1.0, 0.0).astype(BF16)
    before = carry[...] + _bdot(tril, cmat.astype(BF16))
    rank = jnp.sum(jnp.where(sel, before, 0.0), axis=-1, keepdims=True)
    carry[...] = carry[...] + jnp.sum(cmat, axis=0, keepdims=True)
    cnt_ref[...] = carry[...]
    vals = (bucket, rank, g_lo, g_hi)
    route = jnp.zeros_like(logits)
    for j, val in enumerate(vals):
        route = jnp.where(lane == j, val, route)
    route_ref[...] = route


def _router(x, w_route, b_route, n_fine):
    t, d = x.shape
    tm = min(ROW_TILE, t)
    return pl.pallas_call(
        functools.partial(_router_kernel, n_fine=n_fine),
        grid=(t // tm,),
        in_specs=[pl.BlockSpec((tm, d), lambda i: (i, 0)), pl.BlockSpec((d, LANES), lambda i: (0, 0)),
                  pl.BlockSpec((1, LANES), lambda i: (0, 0))],
        out_specs=[pl.BlockSpec((tm, LANES), lambda i: (i, 0)), pl.BlockSpec((1, LANES), lambda i: (0, 0))],
        out_shape=[jax.ShapeDtypeStruct((t, LANES), F32), jax.ShapeDtypeStruct((1, LANES), F32)],
        scratch_shapes=[pltpu.VMEM((1, LANES), F32)],
        compiler_params=_params("arbitrary"),
        name="moe_router",
    )(x, w_route, b_route)


def _row_copy(src_ref, src_row, dst_ref, dst_row, sem):
    return pltpu.make_async_copy(src_ref.at[pl.ds(src_row, 1)], dst_ref.at[pl.ds(dst_row, 1)], sem)


def _dispatch_kernel(dest_ref, x_ref, xs_in_ref, xs_ref, sem, *, tm):
    del xs_in_ref

    def issue(r, carry):
        _row_copy(x_ref, r, xs_ref, dest_ref[0, r], sem).start()
        return carry

    def drain(r, carry):
        _row_copy(x_ref, r, xs_ref, dest_ref[0, r], sem).wait()
        return carry

    lax.fori_loop(0, tm, issue, 0, unroll=8)
    lax.fori_loop(0, tm, drain, 0, unroll=8)


def _dispatch(dest, x, n_rows):
    t, d = x.shape
    tm = min(ROW_TILE, t)
    return pl.pallas_call(
        functools.partial(_dispatch_kernel, tm=tm),
        grid=(t // tm,),
        in_specs=[pl.BlockSpec((1, tm), lambda i: (0, i), memory_space=pltpu.SMEM),
                  pl.BlockSpec((tm, d), lambda i: (i, 0)),
                  pl.BlockSpec(memory_space=pl.ANY)],
        out_specs=pl.BlockSpec(memory_space=pl.ANY),
        out_shape=jax.ShapeDtypeStruct((n_rows, d), F32),
        scratch_shapes=[pltpu.SemaphoreType.DMA(())],
        input_output_aliases={2: 0},
        compiler_params=_params("arbitrary"),
        name="moe_dispatch",
    )(dest, x, jnp.zeros((n_rows, d), F32))


def _expert_kernel(ea_ref, eb_ref, nu_ref, xs_ref, wga_ref, wua_ref, wda_ref, wgb_ref, wub_ref, wdb_ref, ys_ref):
    del ea_ref, eb_ref
    i = pl.program_id(0)
    d = xs_ref.shape[1]

    @pl.when(i < nu_ref[0])
    def _():
        xb = xs_ref[...].astype(BF16)
        for half, (wg_ref, wu_ref, wd_ref) in enumerate(((wga_ref, wua_ref, wda_ref), (wgb_ref, wub_ref, wdb_ref))):
            a = _bdot(xb, wg_ref[...])
            h = (a * _sigmoid(a)) * _bdot(xb, wu_ref[...])
            ys_ref[:, half * d:(half + 1) * d] = _bdot(h.astype(BF16), wd_ref[...])

    @pl.when(i >= nu_ref[0])
    def _():
        ys_ref[...] = jnp.zeros_like(ys_ref)


def _experts(block_ea, block_eb, n_used, xs, wg, wu, wd, blk):
    n_rows, d = xs.shape
    de = wg.shape[-1]
    low = lambda i, ea, eb, nu: (ea[i], 0, 0)
    high = lambda i, ea, eb, nu: (eb[i], 0, 0)
    grid_spec = pltpu.PrefetchScalarGridSpec(
        num_scalar_prefetch=3,
        grid=(n_rows // blk,),
        in_specs=[pl.BlockSpec((blk, d), lambda i, ea, eb, nu: (i, 0)),
                  pl.BlockSpec((None, d, de), low), pl.BlockSpec((None, d, de), low),
                  pl.BlockSpec((None, de, d), low),
                  pl.BlockSpec((None, d, de), high), pl.BlockSpec((None, d, de), high),
                  pl.BlockSpec((None, de, d), high)],
        out_specs=pl.BlockSpec((blk, 2 * d), lambda i, ea, eb, nu: (i, 0)),
    )
    return pl.pallas_call(
        _expert_kernel, grid_spec=grid_spec,
        out_shape=jax.ShapeDtypeStruct((n_rows, 2 * d), F32),
        compiler_params=_params("arbitrary"),
        name="moe_experts",
    )(block_ea, block_eb, n_used, xs, wg, wu, wd, wg, wu, wd)


def _combine_kernel(dest_ref, x_ref, gate_ref, g_ref, b_ref, ys_ref, o_ref, ybuf, sem, *, tm, alpha):
    def issue(r, carry):
        _row_copy(ys_ref, dest_ref[0, r], ybuf, r, sem).start()
        return carry

    def drain(r, carry):
        _row_copy(ys_ref, dest_ref[0, r], ybuf, r, sem).wait()
        return carry

    lax.fori_loop(0, tm, issue, 0, unroll=8)
    lax.fori_loop(0, tm, drain, 0, unroll=8)
    gate = gate_ref[...]
    d = x_ref.shape[1]
    y = alpha * x_ref[...] + gate[:, 0:1] * ybuf[:, :d] + gate[:, 1:2] * ybuf[:, d:]
    o_ref[...] = _layer_norm(y, g_ref[...], b_ref[...])


def _combine(dest, x, gates, ys, ln_g, ln_b, alpha):
    t, d = x.shape
    tm = min(ROW_TILE, t)
    return pl.pallas_call(
        functools.partial(_combine_kernel, tm=tm, alpha=alpha),
        grid=(t // tm,),
        in_specs=[pl.BlockSpec((1, tm), lambda i: (0, i), memory_space=pltpu.SMEM),
                  pl.BlockSpec((tm, d), lambda i: (i, 0)),
                  pl.BlockSpec((tm, 2), lambda i: (i, 0)),
                  pl.BlockSpec((1, d), lambda i: (0, 0)), pl.BlockSpec((1, d), lambda i: (0, 0)),
                  pl.BlockSpec(memory_space=pl.ANY)],
        out_specs=pl.BlockSpec((tm, d), lambda i: (i, 0)),
        out_shape=jax.ShapeDtypeStruct((t, d), F32),
        scratch_shapes=[pltpu.VMEM((tm, 2 * d), F32), pltpu.SemaphoreType.DMA(())],
        compiler_params=_params("arbitrary"),
        name="moe_combine_ln",
    )(dest, x, gates, ln_g.reshape(1, d), ln_b.reshape(1, d), ys)


def _moe_ln(x, w_c, b_c, w_f, b_f, wg, wu, wd, ln_g, ln_b, alpha):
    t, d = x.shape
    n_exp = w_f.shape[1]
    per_group = n_exp // E_GROUPS
    pad = LANES - E_GROUPS - n_exp
    w_route = jnp.concatenate([w_c.astype(F32), w_f.astype(F32), jnp.zeros((d, pad), F32)], axis=1)
    b_route = jnp.concatenate([b_c.astype(F32), b_f.astype(F32), jnp.zeros((pad,), F32)]).reshape(1, LANES)
    route, cnt = _router(x, w_route.astype(BF16), b_route, n_exp)
    bucket = route[:, 0].astype(I32)
    rank = route[:, 1].astype(I32)
    gates = route[:, 2:4]
    pairs = [(g * per_group + a, g * per_group + b) for g in range(E_GROUPS)
             for a in range(per_group) for b in range(a + 1, per_group)]
    n_buckets = len(pairs)
    assert n_buckets <= LANES
    blk = MOE_BLOCK if t >= 8 * MOE_BLOCK else 2 * SUBLANES
    counts = cnt[0, :n_buckets].astype(I32)
    padded = (counts + blk - 1) // blk * blk
    pend = jnp.cumsum(padded)
    poff = pend - padded
    ids = jnp.arange(n_buckets, dtype=I32)

    def lookup(table, idx):
        return jnp.sum(jnp.where(idx[:, None] == ids[None, :], table[None, :], 0), axis=1)

    dest = (lookup(poff, bucket) + rank).reshape(1, t)
    n_blocks = -(-t // blk) + n_buckets
    starts = jnp.arange(n_blocks, dtype=I32) * blk
    block_bucket = jnp.minimum(jnp.sum(pend[None, :] <= starts[:, None], axis=1), n_buckets - 1)
    pair_tab = jnp.array(pairs, dtype=I32)
    block_ea = lookup(pair_tab[:, 0], block_bucket)
    block_eb = lookup(pair_tab[:, 1], block_bucket)
    n_used = (pend[-1] // blk).astype(I32).reshape(1)
    xs = _dispatch(dest, x, n_blocks * blk)
    ys = _experts(block_ea, block_eb, n_used, xs, wg, wu, wd, blk)
    return _combine(dest, x, gates, ys, ln_g, ln_b, alpha)


def _head_sum(x, ones):
    hi = x.astype(BF16)
    lo = (x - hi.astype(F32)).astype(BF16)
    parts = []
    for j in range(x.shape[1] // LANES):
        sl = slice(j * LANES, (j + 1) * LANES)
        parts.append(_bdot(hi[:, sl], ones) + _bdot(lo[:, sl], ones))
    return jnp.concatenate(parts, axis=-1)


def _rw_prep_kernel(sh_ref, halo_ref, sh0_ref, mu_ref, w0_ref, w2_ref, a0_ref, a2_ref, g2_ref,
                    kk_ref, ka_ref, rk_ref, ones_ref,
                    r_o, w_o, k_o, v_o, kk_o, b_o, bonus_o, g_o, *, batch, cw):
    sh = sh_ref[...]
    first = jnp.where(pl.program_id(0) == 0, sh0_ref[...], halo_ref[...])
    prev = jnp.concatenate([first, sh[:sh.shape[0] - batch]], axis=0)
    shm = sh + (prev - sh) * mu_ref[...]
    r, k, v = shm[:, :cw], shm[:, cw:2 * cw], shm[:, 2 * cw:3 * cw]
    xwa = shm[:, 3 * cw:3 * cw + LANES]
    xg = shm[:, 3 * cw + LANES:]
    ones = ones_ref[...]
    z = w0_ref[...] + _bdot(jnp.tanh(xwa).astype(BF16), w2_ref[...])
    softplus = jnp.maximum(-z, 0.0) + jnp.log1p(jnp.exp(-jnp.abs(z)))
    w_o[...] = jnp.exp(-jnp.exp(-softplus - 0.5))
    a = _sigmoid(a0_ref[...] + _bdot(xwa.astype(BF16), a2_ref[...]))
    g_o[...] = _bdot(_sigmoid(xg).astype(BF16), g2_ref[...])
    kk = k * kk_ref[...]
    kk = kk / jnp.maximum(jnp.sqrt(_head_sum(kk * kk, ones)), 1e-12)
    k2 = k * (1.0 + (a - 1.0) * ka_ref[...])
    r_o[...] = r
    k_o[...] = k2
    v_o[...] = v
    kk_o[...] = kk
    b_o[...] = kk * a
    bonus_o[...] = _head_sum(r * k2 * rk_ref[...], ones) * v


def _rw_prep(sh, shift0, batch, prm):
    t, sw = sh.shape
    cw = prm["w0"].shape[-1]
    tm = min(256, t)
    hb = tm // batch
    row = lambda i: (i, 0)
    fix = lambda i: (0, 0)
    vec = lambda n: pl.BlockSpec((1, n), fix)
    return pl.pallas_call(
        functools.partial(_rw_prep_kernel, batch=batch, cw=cw),
        grid=(t // tm,),
        in_specs=[pl.BlockSpec((tm, sw), row),
                  pl.BlockSpec((batch, sw), lambda i: (jnp.maximum(i * hb - 1, 0), 0)),
                  pl.BlockSpec((batch, sw), fix), vec(sw), vec(cw),
                  pl.BlockSpec((LANES, cw), fix), vec(cw), pl.BlockSpec((LANES, cw), fix),
                  pl.BlockSpec((LANES, cw), fix), vec(cw), vec(cw), vec(cw),
                  pl.BlockSpec((LANES, LANES), fix)],
        out_specs=[pl.BlockSpec((tm, cw), row)] * 8,
        out_shape=[jax.ShapeDtypeStruct((t, cw), F32)] * 8,
        compiler_params=_params("parallel"),
        name="rwkv_prep",
    )(sh, sh, shift0, prm["mu"], prm["w0"], prm["w2"], prm["a0"], prm["a2"], prm["g2"],
      prm["k_k"], prm["k_a"], prm["r_k"], prm["ones"])


RW_TILES_A = 4
RW_TILES_B = 2
RW_TICK = 8


def _rwkv_kernel(r_ref, w_ref, k_ref, v_ref, kk_ref, b_ref, sa0_ref, sb0_ref, y_ref, san_ref, sbn_ref,
                 st_a, st_b, xa0, xb0, xa1, xb1, *, batch, steps):
    hd = HEAD_DIM
    hv = hd // 2
    ti = pl.program_id(0)

    @pl.when(ti == 0)
    def _():
        st_a[...] = sa0_ref[...]
        st_b[...] = sb0_ref[...]

    operands = (r_ref, w_ref, k_ref, v_ref, kk_ref, b_ref)
    def lane_of(rows):
        return lax.broadcasted_iota(I32, (rows, LANES), 1)

    lane_lo = lane_of(hd) < hd
    lane_lo_out = lane_of(batch) < hd
    even_quarter = (lane_of(hd) // hv) % 2 == 0
    quarter_v = lane_of(hv) // hv
    quarter_out = lane_of(batch) // hv

    def pick_quarter(q, blocks):
        return jnp.where(q == 0, blocks[0], jnp.where(q == 1, blocks[1], jnp.where(q == 2, blocks[2], blocks[3])))

    def padded(parts, rows):
        have = len(parts) * batch
        return parts + ([jnp.zeros((rows - have, LANES), F32)] if have < rows else [])

    def run(state, x, nv, tick):
        acc = jnp.zeros((nv, LANES), F32)
        for kidx in range(hd):
            acc = acc + state[kidx] * x[4, kidx:kidx + 1, :]
            if kidx % RW_TICK == RW_TICK - 1:
                tick()
        sa = acc
        vv = x[3, 0:nv, :]
        acc = jnp.zeros((nv, LANES), F32)
        for kidx in range(hd):
            row = slice(kidx, kidx + 1)
            sn = state[kidx] * x[1, row, :] - sa * x[5, row, :] + vv * x[2, row, :]
            state[kidx] = sn
            acc = acc + sn * x[0, row, :]
            if kidx % RW_TICK == RW_TICK - 1:
                tick()
        return acc

    def transpose_operand(o, r0, xa, xb):
        slab = operands[o][pl.ds(r0, batch), :]
        tiles = [slab[:, j * LANES:(j + 1) * LANES] for j in range(RW_TILES_A + RW_TILES_B)]
        ta = jnp.concatenate(padded(tiles[:RW_TILES_A], hd) * 2, axis=0).T
        xa[o] = jnp.where(lane_lo, ta[:hd], ta[hd:])
        tb = jnp.concatenate(padded(tiles[RW_TILES_A:], hv) * 4, axis=0).T
        if o == 3:
            xb[o, pl.ds(0, hv), :] = pick_quarter(quarter_v, (tb[:hv], tb[hd:hd + hv], tb[hv:hd], tb[hd + hv:]))
        else:
            xb[o] = jnp.where(even_quarter, tb[:hd], tb[hd:])

    def time_step(t, cur, nxt):
        r0 = pl.multiple_of(t * batch, batch)
        rn = pl.multiple_of(jnp.minimum(t + 1, steps - 1) * batch, batch)
        pending = [functools.partial(transpose_operand, o, rn, *nxt) for o in range(len(operands))]

        def tick():
            if pending:
                pending.pop(0)()

        ya = run(st_a, cur[0], hd, tick)
        yb = run(st_b, cur[1], hv, tick)
        while pending:
            tick()
        mta = jnp.concatenate([ya, ya], axis=0).T
        for j in range(RW_TILES_A):
            lo = mta[j * batch:(j + 1) * batch]
            hi = mta[hd + j * batch:hd + (j + 1) * batch]
            y_ref[pl.ds(r0, batch), j * LANES:(j + 1) * LANES] = jnp.where(lane_lo_out, lo, hi)
        mtb = jnp.concatenate([yb] * 4, axis=0).T
        for j in range(RW_TILES_B):
            blocks = [mtb[base + j * batch:base + (j + 1) * batch] for base in (0, hd, hv, hd + hv)]
            tile = RW_TILES_A + j
            y_ref[pl.ds(r0, batch), tile * LANES:(tile + 1) * LANES] = pick_quarter(quarter_out, blocks)

    even_bufs, odd_bufs = (xa0, xb0), (xa1, xb1)
    for o in range(len(operands)):
        transpose_operand(o, 0, *even_bufs)

    def step_pair(p, carry):
        time_step(2 * p, even_bufs, odd_bufs)
        time_step(2 * p + 1, odd_bufs, even_bufs)
        return carry

    lax.fori_loop(0, steps // 2, step_pair, 0)

    @pl.when(ti == pl.num_programs(0) - 1)
    def _():
        san_ref[...] = st_a[...]
        sbn_ref[...] = st_b[...]


def _rwkv(ops, s0, batch):
    t, cw = ops[0].shape
    hd, hv = HEAD_DIM, HEAD_DIM // 2
    assert cw == (RW_TILES_A + RW_TILES_B) * LANES and RW_TILES_A * batch <= hd
    ca, cb = RW_TILES_A * batch, RW_TILES_B * batch
    if s0 is None:
        sa0 = jnp.zeros((hd, hd, LANES), F32)
        sb0 = jnp.zeros((hd, hv, LANES), F32)
    else:
        s0 = s0.astype(F32)
        sa0 = s0[:, :2 * RW_TILES_A].reshape(batch, RW_TILES_A, 2, hd, hd).transpose(4, 3, 2, 1, 0)
        sa0 = jnp.pad(sa0.reshape(hd, hd, 2, ca), ((0, 0),) * 3 + ((0, hd - ca),)).reshape(hd, hd, LANES)
        sb0 = s0[:, 2 * RW_TILES_A:].reshape(batch, RW_TILES_B, 2, 2, hv, hd).transpose(5, 4, 3, 2, 1, 0)
        sb0 = jnp.pad(sb0.reshape(hd, hv, 2, 2, cb), ((0, 0),) * 4 + ((0, hv - cb),)).reshape(hd, hv, LANES)
    steps = min(16, t // batch)
    assert steps % 2 == 0, "the time loop is unrolled in even/odd operand-buffer pairs"
    rows = steps * batch
    blk = pl.BlockSpec((rows, cw), lambda i: (i, 0))
    st_a = pl.BlockSpec((hd, hd, LANES), lambda i: (0, 0, 0))
    st_b = pl.BlockSpec((hd, hv, LANES), lambda i: (0, 0, 0))
    y, san, sbn = pl.pallas_call(
        functools.partial(_rwkv_kernel, batch=batch, steps=steps),
        grid=(t // rows,),
        in_specs=[blk] * 6 + [st_a, st_b],
        out_specs=[blk, st_a, st_b],
        out_shape=[jax.ShapeDtypeStruct((t, cw), F32), jax.ShapeDtypeStruct((hd, hd, LANES), F32),
                   jax.ShapeDtypeStruct((hd, hv, LANES), F32)],
        scratch_shapes=[pltpu.VMEM((hd, hd, LANES), F32), pltpu.VMEM((hd, hv, LANES), F32),
                        ] + [pltpu.VMEM((6, hd, LANES), F32)] * 4,
        compiler_params=_params("arbitrary"),
        name="rwkv_recurrence",
    )(*ops, sa0, sb0)
    sa = san.reshape(hd, hd, 2, hd)[..., :ca].reshape(hd, hd, 2, RW_TILES_A, batch).transpose(4, 3, 2, 1, 0)
    sb = sbn.reshape(hd, hv, 2, 2, hv)[..., :cb].reshape(hd, hv, 2, 2, RW_TILES_B, batch).transpose(5, 4, 3, 2, 1, 0)
    s_last = jnp.concatenate([sa.reshape(batch, 2 * RW_TILES_A, hd, hd), sb.reshape(batch, 2 * RW_TILES_B, hd, hd)],
                             axis=1)
    return y, s_last


def _rw_post_kernel(y_ref, bonus_ref, g_ref, lg_ref, lb_ref, ones_ref, o_ref):
    y = y_ref[...]
    ones = ones_ref[...]
    mu = _head_sum(y, ones) * (1.0 / HEAD_DIM)
    d = y - mu
    var = _head_sum(d * d, ones) * (1.0 / HEAD_DIM)
    yn = d * lax.rsqrt(var + GN_EPS) * lg_ref[...] + lb_ref[...]
    o_ref[...] = (yn + bonus_ref[...]) * g_ref[...]


def _rw_post(y, bonus, g, ln_g, ln_b, ones):
    t, cw = y.shape
    tm = min(ROW_TILE, t)
    row = pl.BlockSpec((tm, cw), lambda i: (i, 0))
    vec = pl.BlockSpec((1, cw), lambda i: (0, 0))
    return pl.pallas_call(
        _rw_post_kernel, grid=(t // tm,),
        in_specs=[row, row, row, vec, vec, pl.BlockSpec((LANES, LANES), lambda i: (0, 0))],
        out_specs=row, out_shape=jax.ShapeDtypeStruct((t, cw), F32),
        compiler_params=_params("parallel"),
        name="rwkv_post",
    )(y, bonus, g, ln_g, ln_b, ones)


def _pool_kernel(main_ref, halo_ref, w_ref, scale_ref, o_ref, *, batch, pos0):
    tm = main_ref.shape[0]
    full = jnp.concatenate([main_ref[...], halo_ref[...]], axis=0)
    b = batch
    n = full.shape[0]
    a2 = full[b:] + full[:n - b]
    a4 = a2[2 * b:] + a2[:n - 3 * b]
    a8 = a4[4 * b:] + a4[:n - 7 * b]
    a16 = a8[8 * b:] + a8[:n - 15 * b]
    cur = full[16 * b:16 * b + tm]
    sums = (a2[15 * b:15 * b + tm], a4[13 * b:13 * b + tm], a8[9 * b:9 * b + tm], a16[b:b + tm])
    lane = lax.broadcasted_iota(I32, cur.shape, 1)
    grp = lane // (cur.shape[1] // len(POOL_WINDOWS))
    win_sum = sums[-1]
    width = jnp.full(cur.shape, float(POOL_WINDOWS[-1]), F32)
    for j in range(len(POOL_WINDOWS) - 2, -1, -1):
        win_sum = jnp.where(grp == j, sums[j], win_sum)
        width = jnp.where(grp == j, float(POOL_WINDOWS[j]), width)
    row = lax.broadcasted_iota(I32, cur.shape, 0)
    t_idx = pl.program_id(0) * (tm // b) + row // b
    cnt = jnp.minimum(width, (pos0 + t_idx + 1).astype(F32))
    pooled = win_sum / cnt - cur
    o_ref[...] = _bdot(pooled.astype(BF16), w_ref[...]) * scale_ref[...]


def _pool(fp, t, batch, w_bd, scale, pos0):
    dw = fp.shape[1]
    tm = min(ROW_TILE, t)
    halo = 16 * batch
    return pl.pallas_call(
        functools.partial(_pool_kernel, batch=batch, pos0=pos0),
        grid=(t // tm,),
        in_specs=[pl.BlockSpec((tm, dw), lambda i: (i, 0)),
                  pl.BlockSpec((halo, dw), lambda i: ((i + 1) * (tm // halo), 0)),
                  pl.BlockSpec((dw, dw), lambda i: (0, 0)), pl.BlockSpec((1, dw), lambda i: (0, 0))],
        out_specs=pl.BlockSpec((tm, dw), lambda i: (i, 0)),
        out_shape=jax.ShapeDtypeStruct((t, dw), F32),
        compiler_params=_params("parallel"),
        name="causal_pool",
    )(fp, fp, w_bd, scale)


def _to_batch_major(a, seq, batch):
    return a.reshape(seq, batch, a.shape[-1]).transpose(1, 0, 2)


def _to_time_major(a):
    bsz, seq, width = a.shape
    return a.transpose(1, 0, 2).reshape(seq * bsz, width)


def _even_mixer(x, batch, seq, h0, cache_k, cache_v, w, j, bias, alpha, ln_g, ln_b):
    a_width = w["s5_w_glu"].shape[-1]
    w_in = w["ev_w_in"][j]
    qw = a_width
    kvw = (w_in.shape[1] - 2 * a_width) // 2
    u, q, k, v = _matmul([x], [w_in.astype(BF16)], splits=(a_width, qw, kvw, kvw))
    prm = _s5_params(w["s5_a_re"][j], w["s5_a_im"][j], w["s5_log_dt"][j], w["s5_b_re"][j], w["s5_b_im"][j],
                     w["s5_c_re"][j], w["s5_c_im"][j], w["s5_d"][j])
    nt = a_width // LANES
    ns = prm["bre"].shape[-1]
    if h0 is None:
        h0re = jnp.zeros((nt, batch, ns), F32)
        h0im = h0re
    else:
        h0re = h0[..., 0].astype(F32).reshape(batch, nt, ns).transpose(1, 0, 2)
        h0im = h0[..., 1].astype(F32).reshape(batch, nt, ns).transpose(1, 0, 2)
    g, hn_re, hn_im = _s5(u, batch, h0re, h0im, prm)
    groups, n_state = w["s5_a_re"][j].shape
    h_last = jnp.stack([hn_re.transpose(1, 0, 2).reshape(batch, groups, n_state),
                        hn_im.transpose(1, 0, 2).reshape(batch, groups, n_state)], axis=-1)
    a_out = _matmul([g], [w["s5_w_glu"][j].astype(BF16)], epilogue="glu")

    sinks = w["swa_sinks"][j].astype(F32)
    if cache_k is None:
        att = _attention(q, k, v, bias, sinks, batch, CHUNK, CHUNK, WINDOW // CHUNK + 1, True)
        k_all, v_all = k, v
    else:
        k_all = jnp.concatenate([_to_time_major(cache_k.astype(F32).reshape(batch, WINDOW, kvw)), k], axis=0)
        v_all = jnp.concatenate([_to_time_major(cache_v.astype(F32).reshape(batch, WINDOW, kvw)), v], axis=0)
        att = _attention(q, k_all, v_all, bias, sinks, batch, seq, WINDOW + seq, 1, False)
    keep = k_all.shape[0] - WINDOW * batch
    new_k = _to_batch_major(k_all[keep:], WINDOW, batch)
    new_v = _to_batch_major(v_all[keep:], WINDOW, batch)
    w_out = w["ev_w_out"][j].astype(BF16)
    xn = _matmul([a_out, att], [w_out[:a_width], w_out[a_width:]], epilogue="ln",
                 res=x, g=ln_g, b=ln_b, alpha=alpha)
    kv_shape = (batch, WINDOW, kvw // HEAD_DIM, HEAD_DIM)
    return xn, h_last, new_k.reshape(kv_shape), new_v.reshape(kv_shape)


def _odd_mixer(x, batch, seq, shift0, s0, pool0, pos0, w, j, alpha, ln_g, ln_b):
    t = x.shape[0]
    w_in = w["od_w_in"][j]
    cw = w["rw_w0"].shape[-1]
    sw = w["rw_mu"].shape[-1]
    dw = w_in.shape[1] - sw
    n_heads = cw // HEAD_DIM
    sh, p = _matmul([x], [w_in.astype(BF16)], splits=(sw, dw))
    zeros_half = jnp.zeros((LANES - w["rw_w2"].shape[1], cw), F32)
    head_lane = jnp.arange(LANES) // HEAD_DIM
    ones = (head_lane[:, None] == head_lane[None, :]).astype(BF16)
    prm = dict(
        mu=w["rw_mu"][j].astype(F32).reshape(1, sw), w0=w["rw_w0"][j].astype(F32).reshape(1, cw),
        w2=jnp.concatenate([w["rw_w2"][j].astype(F32), zeros_half], axis=0).astype(BF16),
        a0=w["rw_a0"][j].astype(F32).reshape(1, cw),
        a2=jnp.concatenate([zeros_half, w["rw_a2"][j].astype(F32)], axis=0).astype(BF16),
        g2=w["rw_g2"][j].astype(BF16),
        k_k=w["rw_k_k"][j].astype(F32).reshape(1, cw), k_a=w["rw_k_a"][j].astype(F32).reshape(1, cw),
        r_k=w["rw_r_k"][j].astype(F32).reshape(1, cw), ones=ones)
    if shift0 is None:
        shift0 = jnp.zeros((batch, sw), F32)
    r, dec, k2, v, kk, b, bonus, g = _rw_prep(sh, shift0.astype(F32), batch, prm)

    y, s_last = _rwkv((r, dec, k2, v, kk, b), s0, batch)
    c_out = _rw_post(y, bonus, g, w["rw_ln_g"][j].astype(F32).reshape(1, cw),
                     w["rw_ln_b"][j].astype(F32).reshape(1, cw), ones)

    if pool0 is None:
        hist = jnp.zeros((POOL_HIST * batch, dw), F32)
    else:
        hist = _to_time_major(pool0.astype(F32))
    fp = jnp.concatenate([jnp.zeros((batch, dw), F32), hist, p], axis=0)
    n_grp = len(POOL_WINDOWS)
    pg = dw // n_grp
    eye = jnp.eye(n_grp, dtype=F32)
    w_bd = (w["pool_w"][j].astype(F32)[:, :, None, :] * eye[:, None, :, None]).reshape(dw, dw).astype(BF16)
    d_out = _pool(fp, t, batch, w_bd, w["pool_scale"][j].astype(F32).reshape(1, dw), pos0)
    pool_new = _to_batch_major(fp[fp.shape[0] - POOL_HIST * batch:], POOL_HIST, batch)

    w_out = w["od_w_out"][j].astype(BF16)
    xn = _matmul([c_out, d_out], [w_out[:cw], w_out[cw:]], epilogue="ln", res=x, g=ln_g, b=ln_b, alpha=alpha)
    sh_last = sh[t - batch:]
    return xn, sh_last, s_last, pool_new


def _trunk(x3, s5_0, k_0, v_0, rw_0, shift_0, pool_0, pos0, w):
    batch, seq, _ = x3.shape
    depth = w["ln1_g"].shape[0]
    alpha = (2 * depth) ** 0.25
    x = _to_time_major(x3.astype(F32))
    prompt = s5_0 is None
    if prompt:
        bias = _bias_table(w["rel_bias"], CHUNK, (WINDOW // CHUNK + 1) * CHUNK)
    else:
        bias = _bias_table(w["rel_bias"], seq, WINDOW + seq)
    s5_new, k_new, v_new, rw_new, sh_new, pool_new = [], [], [], [], [], []
    for i in range(depth):
        j = i // 2
        if i % 2 == 0:
            x, h1, nk, nv = _even_mixer(
                x, batch, seq, None if prompt else s5_0[j], None if prompt else k_0[j],
                None if prompt else v_0[j], w, j, bias, alpha, w["ln1_g"][i], w["ln1_b"][i])
            s5_new.append(h1)
            k_new.append(nk)
            v_new.append(nv)
        else:
            x, sh1, s1, pl1 = _odd_mixer(
                x, batch, seq, None if prompt else shift_0[j], None if prompt else rw_0[j],
                None if prompt else pool_0[j], pos0, w, j, alpha, w["ln1_g"][i], w["ln1_b"][i])
            sh_new.append(sh1)
            rw_new.append(s1)
            pool_new.append(pl1)
        x = _moe_ln(x, w["moe_w_coarse"][i], w["moe_b_coarse"][i], w["moe_w_fine"][i], w["moe_b_fine"][i],
                    w["moe_w_gate"][i].astype(BF16), w["moe_w_up"][i].astype(BF16),
                    w["moe_w_down"][i].astype(BF16), w["ln2_g"][i], w["ln2_b"][i], alpha)
    y = _to_batch_major(x, seq, batch)
    return (y, jnp.stack(s5_new), jnp.stack(k_new), jnp.stack(v_new),
            jnp.stack(rw_new), jnp.stack(sh_new), jnp.stack(pool_new))


def kernel(x_prompt, x_sample, state_s5, cache_swa_k, cache_swa_v, state_rwkv, state_shift, state_pool, rel_bias, ev_w_in, ev_w_out, s5_a_re, s5_a_im, s5_log_dt, s5_b_re, s5_b_im, s5_c_re, s5_c_im, s5_d, s5_w_glu, swa_sinks, od_w_in, od_w_out, rw_mu, rw_w0, rw_w2, rw_a0, rw_a2, rw_g2, rw_k_k, rw_k_a, rw_r_k, rw_ln_g, rw_ln_b, pool_w, pool_scale, ln1_g, ln1_b, ln2_g, ln2_b, moe_w_coarse, moe_b_coarse, moe_w_fine, moe_b_fine, moe_w_gate, moe_w_up, moe_w_down):
    w = dict(rel_bias=rel_bias, ev_w_in=ev_w_in, ev_w_out=ev_w_out, s5_a_re=s5_a_re, s5_a_im=s5_a_im,
             s5_log_dt=s5_log_dt, s5_b_re=s5_b_re, s5_b_im=s5_b_im, s5_c_re=s5_c_re, s5_c_im=s5_c_im,
             s5_d=s5_d, s5_w_glu=s5_w_glu, swa_sinks=swa_sinks, od_w_in=od_w_in, od_w_out=od_w_out,
             rw_mu=rw_mu, rw_w0=rw_w0, rw_w2=rw_w2, rw_a0=rw_a0, rw_a2=rw_a2, rw_g2=rw_g2, rw_k_k=rw_k_k,
             rw_k_a=rw_k_a, rw_r_k=rw_r_k, rw_ln_g=rw_ln_g, rw_ln_b=rw_ln_b, pool_w=pool_w,
             pool_scale=pool_scale, ln1_g=ln1_g, ln1_b=ln1_b, ln2_g=ln2_g, ln2_b=ln2_b,
             moe_w_coarse=moe_w_coarse, moe_b_coarse=moe_b_coarse, moe_w_fine=moe_w_fine,
             moe_b_fine=moe_b_fine, moe_w_gate=moe_w_gate, moe_w_up=moe_w_up, moe_w_down=moe_w_down)
    y_s, s5_s, k_s, v_s, rw_s, sh_s, pool_s = _trunk(
        x_sample, state_s5, cache_swa_k, cache_swa_v, state_rwkv, state_shift, state_pool, PAST_LEN, w)
    y_p, s5_p, k_p, v_p, rw_p, sh_p, pool_p = _trunk(x_prompt, None, None, None, None, None, None, 0, w)
    return (y_p, y_s, s5_p, s5_s, k_p, v_p, k_s, v_s, rw_p, rw_s, sh_p, sh_s, pool_p, pool_s)
```

```python
import functools
import math

import jax
import jax.numpy as jnp
from jax import lax
from jax.experimental import pallas as pl
from jax.experimental.pallas import tpu as pltpu

F32 = jnp.float32
BF16 = jnp.bfloat16
I32 = jnp.int32

CHUNK = 64
WINDOW = 128
HEAD_DIM = 64
PAST_LEN = 1024
N_BUCKETS = 32
MAX_DISTANCE = 128
POOL_WINDOWS = (2, 4, 8, 16)
POOL_HIST = max(POOL_WINDOWS) - 1
E_GROUPS = 4
LN_EPS = 1e-5
GN_EPS = 64e-5
A_GROUP = 16

LANES = 128
SUBLANES = 8
VMEM_LIMIT_BYTES = 56 * 1024 * 1024

ROW_TILE = 512
MOE_BLOCK = 256
MOE_DENSE_BELOW = 8 * MOE_BLOCK


def _params(*sem):
    return pltpu.CompilerParams(dimension_semantics=sem, vmem_limit_bytes=VMEM_LIMIT_BYTES)


def _sigmoid(x):
    return 1.0 / (1.0 + jnp.exp(-x))


def _layer_norm(y, g, b):
    mu = jnp.mean(y, axis=-1, keepdims=True)
    d = y - mu
    var = jnp.mean(d * d, axis=-1, keepdims=True)
    return d * lax.rsqrt(var + LN_EPS) * g + b


def _bdot(a, b):
    return jnp.dot(a, b, preferred_element_type=F32)


def _mm_kernel(*refs, n_in, splits, epilogue, alpha):
    xs = refs[:n_in]
    ws = refs[n_in:2 * n_in]
    pos = 2 * n_in
    acc = None
    for x_ref, w_ref in zip(xs, ws):
        d = _bdot(x_ref[...].astype(BF16), w_ref[...])
        acc = d if acc is None else acc + d
    if epilogue == "ln":
        res_ref, g_ref, b_ref = refs[pos:pos + 3]
        pos += 3
        acc = _layer_norm(alpha * res_ref[...] + acc, g_ref[...], b_ref[...])
    elif epilogue == "glu":
        acc = xs[0][...] * _sigmoid(acc)
    off = 0
    for o_ref, n in zip(refs[pos:], splits):
        o_ref[...] = acc[:, off:off + n].astype(o_ref.dtype)
        off += n


def _matmul(xs, ws, splits=None, epilogue=None, res=None, g=None, b=None, alpha=None, out_dtype=F32):
    t = xs[0].shape[0]
    n = ws[0].shape[1]
    splits = tuple(splits) if splits else (n,)
    tm = min(ROW_TILE, t)
    in_specs = [pl.BlockSpec((tm, x.shape[1]), lambda i: (i, 0)) for x in xs]
    in_specs += [pl.BlockSpec(w.shape, lambda i: (0, 0)) for w in ws]
    args = list(xs) + list(ws)
    if epilogue == "ln":
        in_specs += [pl.BlockSpec((tm, n), lambda i: (i, 0)),
                     pl.BlockSpec((1, n), lambda i: (0, 0)), pl.BlockSpec((1, n), lambda i: (0, 0))]
        args += [res, g.reshape(1, n), b.reshape(1, n)]
    outs = pl.pallas_call(
        functools.partial(_mm_kernel, n_in=len(xs), splits=splits, epilogue=epilogue, alpha=alpha),
        grid=(t // tm,),
        in_specs=in_specs,
        out_specs=[pl.BlockSpec((tm, s), lambda i: (i, 0)) for s in splits],
        out_shape=[jax.ShapeDtypeStruct((t, s), out_dtype) for s in splits],
        compiler_params=_params("parallel"),
        name="matmul_" + (epilogue or "plain"),
    )(*args)
    return outs if len(splits) > 1 else outs[0]


def _s5_kernel(u_ref, bre_ref, bim_ref, cre_ref, cim_ref, are_ref, aim_ref, d_ref, h0re_ref, h0im_ref,
               g_ref, hnre_ref, hnim_ref, hre, him, car_re, car_im, *, batch, steps):
    ti = pl.program_id(1)

    @pl.when(ti == 0)
    def _():
        car_re[...] = h0re_ref[...]
        car_im[...] = h0im_ref[...]

    u = u_ref[...]
    ub = u.astype(BF16)
    hre[...] = _bdot(ub, bre_ref[...])
    him[...] = _bdot(ub, bim_ref[...])
    a_re = jnp.broadcast_to(are_ref[...], car_re.shape)
    a_im = jnp.broadcast_to(aim_ref[...], car_re.shape)

    def step(t, carry):
        h_re, h_im = carry
        r0 = pl.multiple_of(t * batch, batch)
        n_re = a_re * h_re - a_im * h_im + hre[pl.ds(r0, batch), :]
        n_im = a_re * h_im + a_im * h_re + him[pl.ds(r0, batch), :]
        hre[pl.ds(r0, batch), :] = n_re
        him[pl.ds(r0, batch), :] = n_im
        return n_re, n_im

    h_re, h_im = lax.fori_loop(0, steps, step, (car_re[...], car_im[...]))
    car_re[...] = h_re
    car_im[...] = h_im
    hnre_ref[...] = h_re
    hnim_ref[...] = h_im
    y = (_bdot(hre[...].astype(BF16), cre_ref[...]) - _bdot(him[...].astype(BF16), cim_ref[...])
         + d_ref[...] * u)
    cdf = 0.5 * (1.0 + jnp.tanh(math.sqrt(2.0 / math.pi) * (y + 0.044715 * (y * y * y))))
    g_ref[...] = y * cdf


def _s5(u, batch, h0re, h0im, prm):
    t, width = u.shape
    nt = width // LANES
    ns = prm["bre"].shape[-1]
    rows = min(1024, t)
    steps = rows // batch
    tile = lambda j, i: (j, 0, 0)
    return pl.pallas_call(
        functools.partial(_s5_kernel, batch=batch, steps=steps),
        grid=(nt, t // rows),
        in_specs=[pl.BlockSpec((rows, LANES), lambda j, i: (i, j)),
                  pl.BlockSpec((None, LANES, ns), tile), pl.BlockSpec((None, LANES, ns), tile),
                  pl.BlockSpec((None, ns, LANES), tile), pl.BlockSpec((None, ns, LANES), tile),
                  pl.BlockSpec((None, 1, ns), tile), pl.BlockSpec((None, 1, ns), tile),
                  pl.BlockSpec((None, 1, LANES), tile),
                  pl.BlockSpec((None, batch, ns), tile), pl.BlockSpec((None, batch, ns), tile)],
        out_specs=[pl.BlockSpec((rows, LANES), lambda j, i: (i, j)),
                   pl.BlockSpec((None, batch, ns), tile), pl.BlockSpec((None, batch, ns), tile)],
        out_shape=[jax.ShapeDtypeStruct((t, width), F32),
                   jax.ShapeDtypeStruct((nt, batch, ns), F32), jax.ShapeDtypeStruct((nt, batch, ns), F32)],
        scratch_shapes=[pltpu.VMEM((rows, ns), F32), pltpu.VMEM((rows, ns), F32),
                        pltpu.VMEM((batch, ns), F32), pltpu.VMEM((batch, ns), F32)],
        compiler_params=_params("parallel", "arbitrary"),
        name="s5_scan",
    )(u, prm["bre"], prm["bim"], prm["cre"], prm["cim"], prm["are"], prm["aim"], prm["d"], h0re, h0im)


def _s5_params(a_re, a_im, log_dt, b_re, b_im, c_re, c_im, d_skip):
    groups, n_state = a_re.shape
    gpt = LANES // A_GROUP
    nt = groups // gpt
    lam = lax.complex(a_re.astype(F32), a_im.astype(F32))
    dt = jnp.exp(log_dt.astype(F32))[:, None]
    a_bar = jnp.exp(lam * dt)
    b_bar = ((a_bar - 1.0) / lam)[..., None] * lax.complex(b_re.astype(F32), b_im.astype(F32))
    eye = jnp.eye(gpt, dtype=F32)

    def in_proj(m):
        m = m.reshape(nt, gpt, n_state, A_GROUP).transpose(0, 1, 3, 2)
        return (m[:, :, :, None, :] * eye[None, :, None, :, None]).reshape(nt, gpt * A_GROUP, gpt * n_state)

    def out_proj(m):
        m = m.astype(F32).reshape(nt, gpt, A_GROUP, n_state).transpose(0, 1, 3, 2)
        return (m[:, :, :, None, :] * eye[None, :, None, :, None]).reshape(nt, gpt * n_state, gpt * A_GROUP)

    return dict(bre=in_proj(b_bar.real).astype(BF16), bim=in_proj(b_bar.imag).astype(BF16),
                cre=out_proj(c_re).astype(BF16), cim=out_proj(c_im).astype(BF16),
                are=a_bar.real.reshape(nt, 1, gpt * n_state), aim=a_bar.imag.reshape(nt, 1, gpt * n_state),
                d=d_skip.astype(F32).reshape(nt, 1, LANES))


def _attn_kernel(*refs, n_kv, masked, tiles_per_kv, batch):
    n_q = 2 * tiles_per_kv
    q_refs = refs[:n_q]
    k_refs = refs[n_q:n_q + n_kv]
    v_refs = refs[n_q + n_kv:n_q + 2 * n_kv]
    bias_ref, sink_ref, o_ref, o_tiles = refs[n_q + 2 * n_kv:]
    c = pl.program_id(0)
    cq = q_refs[0].shape[0] // batch
    ck = k_refs[0].shape[0] // batch
    nk = n_kv * ck
    low_half = lax.broadcasted_iota(I32, (nk, LANES), 1) < HEAD_DIM
    if masked:
        key = lax.broadcasted_iota(I32, (nk, tiles_per_kv * cq), 0)
        valid = (c - (n_kv - 1)) * CHUNK + key >= 0

    def one_sequence(b, carry):
        kt = jnp.concatenate([r[pl.ds(b, ck, stride=batch), :] for r in k_refs], axis=0)
        vt = jnp.concatenate([r[pl.ds(b, ck, stride=batch), :] for r in v_refs], axis=0)
        k_sw = pltpu.roll(kt, HEAD_DIM, 1)
        v_sw = pltpu.roll(vt, HEAD_DIM, 1)

        def placed(t, t_sw, kvh, half):
            src = t if kvh == half else t_sw
            keep = low_half if half == 0 else jnp.logical_not(low_half)
            return jnp.where(keep, src, 0.0).astype(BF16)

        scores = []
        for kvh in range(2):
            tiles = [q_refs[kvh * tiles_per_kv + i][pl.ds(b, cq, stride=batch), :] for i in range(tiles_per_kv)]
            qs = jnp.concatenate(tiles, axis=0).astype(BF16)
            for half in range(2):
                s = lax.dot_general(placed(kt, k_sw, kvh, half), qs, (((1,), (1,)), ((), ())),
                                    preferred_element_type=F32)
                s = s * (HEAD_DIM ** -0.5) + bias_ref[2 * kvh + half]
                if masked:
                    s = jnp.where(valid, s, -jnp.inf)
                scores.append(s)
        probs = []
        for grp, s in enumerate(scores):
            sink = sink_ref[grp]
            m = jnp.maximum(jnp.max(s, axis=0, keepdims=True), sink)
            p = jnp.exp(s - m)
            denom = jnp.sum(p, axis=0, keepdims=True) + jnp.exp(sink - m)
            probs.append((p / denom).astype(BF16))
        tn = (((0,), (0,)), ((), ()))
        for kvh in range(2):
            o_t = (lax.dot_general(placed(vt, v_sw, kvh, 0), probs[2 * kvh], tn, preferred_element_type=F32)
                   + lax.dot_general(placed(vt, v_sw, kvh, 1), probs[2 * kvh + 1], tn, preferred_element_type=F32))
            o = o_t.T
            for i in range(tiles_per_kv):
                tile = kvh * tiles_per_kv + i
                o_tiles[tile, pl.ds(b, cq, stride=batch), :] = o[i * cq:(i + 1) * cq]
        return carry

    lax.fori_loop(0, batch, one_sequence, 0, unroll=2)
    for tile in range(n_q):
        o_ref[:, tile * LANES:(tile + 1) * LANES] = o_tiles[tile].astype(o_ref.dtype)


def _attention(q, k, v, bias, sinks, batch, cq, ck, n_kv, masked):
    tq, qw = q.shape
    kw = k.shape[-1]
    n_heads = qw // HEAD_DIM
    assert kw == LANES and n_heads % 4 == 0, "two kv heads in one lane tile, an even number of query tiles each"
    tiles_per_kv = n_heads // 4
    heads = [[2 * (kvh * tiles_per_kv + i) + half for i in range(tiles_per_kv)]
             for kvh in range(2) for half in range(2)]
    bias_g = jnp.stack([jnp.concatenate([bias[h].T for h in hs], axis=1) for hs in heads])
    sink_g = jnp.stack([jnp.concatenate([jnp.ones((1, cq), F32) * sinks[h] for h in hs], axis=1)
                        for hs in heads])

    def kv_spec(s):
        return pl.BlockSpec((ck * batch, kw), lambda c: (jnp.maximum(c - (n_kv - 1) + s, 0), 0))

    return pl.pallas_call(
        functools.partial(_attn_kernel, n_kv=n_kv, masked=masked, tiles_per_kv=tiles_per_kv, batch=batch),
        grid=(tq // (cq * batch),),
        in_specs=[pl.BlockSpec((cq * batch, LANES), lambda c, j=j: (c, j)) for j in range(qw // LANES)]
        + [kv_spec(s) for s in range(n_kv)] + [kv_spec(s) for s in range(n_kv)]
        + [pl.BlockSpec(bias_g.shape, lambda c: (0, 0, 0)), pl.BlockSpec(sink_g.shape, lambda c: (0, 0, 0))],
        out_specs=pl.BlockSpec((cq * batch, qw), lambda c: (c, 0)),
        out_shape=jax.ShapeDtypeStruct((tq, qw), BF16),
        scratch_shapes=[pltpu.VMEM((qw // LANES, cq * batch, LANES), F32)],
        compiler_params=_params("parallel"),
        name="swa_attention",
    )(*([q] * (qw // LANES)), *([k] * n_kv), *([v] * n_kv), bias_g, sink_g)


def _t5_bucket(rel):
    half = N_BUCKETS // 2
    max_exact = half // 2
    n = jnp.abs(rel)
    large = max_exact + (jnp.log(jnp.maximum(n, 1).astype(F32) / max_exact)
                         / math.log(MAX_DISTANCE / max_exact) * (half - max_exact)).astype(I32)
    large = jnp.minimum(large, half - 1)
    return jnp.where(rel > 0, half, 0) + jnp.where(n < max_exact, n, large)


def _bias_table(rel_bias, nq, nk):
    rel = (jnp.arange(nk) - WINDOW)[None, :] - jnp.arange(nq)[:, None]
    return jnp.transpose(rel_bias.astype(F32)[_t5_bucket(rel)], (2, 0, 1))


def _router_kernel(x_ref, w_ref, b_ref, route_ref, cnt_ref, carry, *, n_fine):
    @pl.when(pl.program_id(0) == 0)
    def _():
        carry[...] = jnp.zeros_like(carry)

    logits = _bdot(x_ref[...].astype(BF16), w_ref[...]) + b_ref[...]
    tm = logits.shape[0]
    lane = lax.broadcasted_iota(I32, logits.shape, 1)
    lanef = lane.astype(F32)
    per_group = n_fine // E_GROUPS
    big = float(LANES)

    def first_max(mask):
        mx = jnp.max(jnp.where(mask, logits, -jnp.inf), axis=-1, keepdims=True)
        idx = jnp.min(jnp.where(mask & (logits == mx), lanef, big), axis=-1, keepdims=True)
        return mx, idx

    cmask = lane < E_GROUPS
    mc, grp = first_max(cmask)
    p_grp = 1.0 / jnp.sum(jnp.where(cmask, jnp.exp(logits - mc), 0.0), axis=-1, keepdims=True)
    lo = E_GROUPS + per_group * grp
    fmask = (lanef >= lo) & (lanef < lo + per_group)
    m1, i1 = first_max(fmask)
    sel0 = lanef == i1
    m2, i2 = first_max(fmask & jnp.logical_not(sel0))
    e = jnp.exp(m2 - m1)
    g0 = (1.0 / (1.0 + e)) * p_grp
    g1 = (e / (1.0 + e)) * p_grp

    j1 = i1 - lo
    j2 = i2 - lo
    swap = j1 > j2
    j_lo = jnp.minimum(j1, j2)
    j_hi = jnp.maximum(j1, j2)
    n_pairs_group = per_group * (per_group - 1) // 2
    bucket = grp * n_pairs_group + j_lo * (2 * per_group - 1 - j_lo) * 0.5 + (j_hi - j_lo - 1.0)
    g_lo = jnp.where(swap, g1, g0)
    g_hi = jnp.where(swap, g0, g1)

    sel = lanef == bucket
    cmat = jnp.where(sel, 1.0, 0.0)
    row = lax.broadcasted_iota(I32, (tm, tm), 0)
    col = lax.broadcasted_iota(I32, (tm, tm), 1)
    tril = jnp.where(col < row, 1.0, 0.0).astype(BF16)
    before = carry[...] + _bdot(tril, cmat.astype(BF16))
    rank = jnp.sum(jnp.where(sel, before, 0.0), axis=-1, keepdims=True)
    carry[...] = carry[...] + jnp.sum(cmat, axis=0, keepdims=True)
    cnt_ref[...] = carry[...]
    vals = (bucket, rank, g_lo, g_hi)
    route = jnp.zeros_like(logits)
    for j, val in enumerate(vals):
        route = jnp.where(lane == j, val, route)
    route_ref[...] = route


def _router(x, w_route, b_route, n_fine):
    t, d = x.shape
    tm = min(ROW_TILE, t)
    return pl.pallas_call(
        functools.partial(_router_kernel, n_fine=n_fine),
        grid=(t // tm,),
        in_specs=[pl.BlockSpec((tm, d), lambda i: (i, 0)), pl.BlockSpec((d, LANES), lambda i: (0, 0)),
                  pl.BlockSpec((1, LANES), lambda i: (0, 0))],
        out_specs=[pl.BlockSpec((tm, LANES), lambda i: (i, 0)), pl.BlockSpec((1, LANES), lambda i: (0, 0))],
        out_shape=[jax.ShapeDtypeStruct((t, LANES), F32), jax.ShapeDtypeStruct((1, LANES), F32)],
        scratch_shapes=[pltpu.VMEM((1, LANES), F32)],
        compiler_params=_params("arbitrary"),
        name="moe_router",
    )(x, w_route, b_route)


def _row_copy(src_ref, src_row, dst_ref, dst_row, sem):
    return pltpu.make_async_copy(src_ref.at[pl.ds(src_row, 1)], dst_ref.at[pl.ds(dst_row, 1)], sem)


def _dispatch_kernel(dest_ref, x_ref, xs_in_ref, xs_ref, sem, *, tm):
    del xs_in_ref

    def issue(r, carry):
        _row_copy(x_ref, r, xs_ref, dest_ref[0, r], sem).start()
        return carry

    lax.fori_loop(0, tm, issue, 0, unroll=8)
    pltpu.make_async_copy(x_ref, xs_ref.at[pl.ds(0, tm)], sem).wait()


def _dispatch(dest, x, n_rows):
    t, d = x.shape
    tm = min(ROW_TILE, t)
    return pl.pallas_call(
        functools.partial(_dispatch_kernel, tm=tm),
        grid=(t // tm,),
        in_specs=[pl.BlockSpec((1, tm), lambda i: (0, i), memory_space=pltpu.SMEM),
                  pl.BlockSpec((tm, d), lambda i: (i, 0)),
                  pl.BlockSpec(memory_space=pl.ANY)],
        out_specs=pl.BlockSpec(memory_space=pl.ANY),
        out_shape=jax.ShapeDtypeStruct((n_rows, d), F32),
        scratch_shapes=[pltpu.SemaphoreType.DMA(())],
        input_output_aliases={2: 0},
        compiler_params=_params("arbitrary"),
        name="moe_dispatch",
    )(dest, x, jnp.zeros((n_rows, d), F32))


def _expert_kernel(ea_ref, eb_ref, nu_ref, xs_ref, wga_ref, wua_ref, wda_ref, wgb_ref, wub_ref, wdb_ref, ys_ref):
    del ea_ref, eb_ref
    i = pl.program_id(0)
    d = xs_ref.shape[1]

    @pl.when(i < nu_ref[0])
    def _():
        xb = xs_ref[...].astype(BF16)
        for half, (wg_ref, wu_ref, wd_ref) in enumerate(((wga_ref, wua_ref, wda_ref), (wgb_ref, wub_ref, wdb_ref))):
            a = _bdot(xb, wg_ref[...])
            h = (a * _sigmoid(a)) * _bdot(xb, wu_ref[...])
            ys_ref[:, half * d:(half + 1) * d] = _bdot(h.astype(BF16), wd_ref[...])

    @pl.when(i >= nu_ref[0])
    def _():
        ys_ref[...] = jnp.zeros_like(ys_ref)


def _experts(block_ea, block_eb, n_used, xs, wg, wu, wd, blk):
    n_rows, d = xs.shape
    de = wg.shape[-1]
    low = lambda i, ea, eb, nu: (ea[i], 0, 0)
    high = lambda i, ea, eb, nu: (eb[i], 0, 0)
    grid_spec = pltpu.PrefetchScalarGridSpec(
        num_scalar_prefetch=3,
        grid=(n_rows // blk,),
        in_specs=[pl.BlockSpec((blk, d), lambda i, ea, eb, nu: (i, 0)),
                  pl.BlockSpec((None, d, de), low), pl.BlockSpec((None, d, de), low),
                  pl.BlockSpec((None, de, d), low),
                  pl.BlockSpec((None, d, de), high), pl.BlockSpec((None, d, de), high),
                  pl.BlockSpec((None, de, d), high)],
        out_specs=pl.BlockSpec((blk, 2 * d), lambda i, ea, eb, nu: (i, 0)),
    )
    return pl.pallas_call(
        _expert_kernel, grid_spec=grid_spec,
        out_shape=jax.ShapeDtypeStruct((n_rows, 2 * d), F32),
        compiler_params=_params("arbitrary"),
        name="moe_experts",
    )(block_ea, block_eb, n_used, xs, wg, wu, wd, wg, wu, wd)


def _combine_kernel(dest_ref, x_ref, gate_ref, g_ref, b_ref, ys_ref, o_ref, ybuf, sem, *, tm, alpha):
    def issue(r, carry):
        _row_copy(ys_ref, dest_ref[0, r], ybuf, r, sem).start()
        return carry

    lax.fori_loop(0, tm, issue, 0, unroll=8)
    pltpu.make_async_copy(ys_ref.at[pl.ds(0, tm)], ybuf, sem).wait()
    gate = gate_ref[...]
    d = x_ref.shape[1]
    y = alpha * x_ref[...] + gate[:, 0:1] * ybuf[:, :d] + gate[:, 1:2] * ybuf[:, d:]
    o_ref[...] = _layer_norm(y, g_ref[...], b_ref[...])


def _combine(dest, x, gates, ys, ln_g, ln_b, alpha):
    t, d = x.shape
    tm = min(ROW_TILE, t)
    return pl.pallas_call(
        functools.partial(_combine_kernel, tm=tm, alpha=alpha),
        grid=(t // tm,),
        in_specs=[pl.BlockSpec((1, tm), lambda i: (0, i), memory_space=pltpu.SMEM),
                  pl.BlockSpec((tm, d), lambda i: (i, 0)),
                  pl.BlockSpec((tm, 2), lambda i: (i, 0)),
                  pl.BlockSpec((1, d), lambda i: (0, 0)), pl.BlockSpec((1, d), lambda i: (0, 0)),
                  pl.BlockSpec(memory_space=pl.ANY)],
        out_specs=pl.BlockSpec((tm, d), lambda i: (i, 0)),
        out_shape=jax.ShapeDtypeStruct((t, d), F32),
        scratch_shapes=[pltpu.VMEM((tm, 2 * d), F32), pltpu.SemaphoreType.DMA(())],
        compiler_params=_params("arbitrary"),
        name="moe_combine_ln",
    )(dest, x, gates, ln_g.reshape(1, d), ln_b.reshape(1, d), ys)


def _moe_dense_kernel(x_ref, gate_ref, wg_ref, wu_ref, wd_ref, g_ref, b_ref, o_ref, acc, *, alpha):
    e = pl.program_id(0)

    @pl.when(e == 0)
    def _():
        acc[...] = jnp.zeros_like(acc)

    x = x_ref[...]
    xb = x.astype(BF16)
    a = _bdot(xb, wg_ref[...])
    h = (a * _sigmoid(a)) * _bdot(xb, wu_ref[...])
    y = _bdot(h.astype(BF16), wd_ref[...])
    gates = gate_ref[...]
    lane = lax.broadcasted_iota(I32, gates.shape, 1)
    acc[...] += jnp.sum(jnp.where(lane == e, gates, 0.0), axis=-1, keepdims=True) * y

    @pl.when(e == pl.num_programs(0) - 1)
    def _():
        o_ref[...] = _layer_norm(alpha * x + acc[...], g_ref[...], b_ref[...])


def _moe_dense(x, dense_gates, wg, wu, wd, ln_g, ln_b, alpha):
    t, d = x.shape
    n_exp, _, de = wg.shape
    whole = lambda e: (0, 0)
    return pl.pallas_call(
        functools.partial(_moe_dense_kernel, alpha=alpha),
        grid=(n_exp,),
        in_specs=[pl.BlockSpec((t, d), whole), pl.BlockSpec((t, LANES), whole),
                  pl.BlockSpec((None, d, de), lambda e: (e, 0, 0)), pl.BlockSpec((None, d, de), lambda e: (e, 0, 0)),
                  pl.BlockSpec((None, de, d), lambda e: (e, 0, 0)),
                  pl.BlockSpec((1, d), whole), pl.BlockSpec((1, d), whole)],
        out_specs=pl.BlockSpec((t, d), whole),
        out_shape=jax.ShapeDtypeStruct((t, d), F32),
        scratch_shapes=[pltpu.VMEM((t, d), F32)],
        compiler_params=_params("arbitrary"),
        name="moe_dense_ln",
    )(x, dense_gates, wg, wu, wd, ln_g.reshape(1, d), ln_b.reshape(1, d))


def _moe_ln(x, w_c, b_c, w_f, b_f, wg, wu, wd, ln_g, ln_b, alpha):
    t, d = x.shape
    n_exp = w_f.shape[1]
    per_group = n_exp // E_GROUPS
    pad = LANES - E_GROUPS - n_exp
    w_route = jnp.concatenate([w_c.astype(F32), w_f.astype(F32), jnp.zeros((d, pad), F32)], axis=1)
    b_route = jnp.concatenate([b_c.astype(F32), b_f.astype(F32), jnp.zeros((pad,), F32)]).reshape(1, LANES)
    route, cnt = _router(x, w_route.astype(BF16), b_route, n_exp)
    bucket = route[:, 0].astype(I32)
    rank = route[:, 1].astype(I32)
    gates = route[:, 2:4]
    pairs = [(g * per_group + a, g * per_group + b) for g in range(E_GROUPS)
             for a in range(per_group) for b in range(a + 1, per_group)]
    n_buckets = len(pairs)
    assert n_buckets <= LANES
    pair_tab = jnp.array(pairs, dtype=I32)
    ids = jnp.arange(n_buckets, dtype=I32)

    def lookup(table, idx):
        return jnp.sum(jnp.where(idx[:, None] == ids[None, :], table[None, :], 0), axis=1)

    if t < MOE_DENSE_BELOW:
        lane = jnp.arange(LANES, dtype=I32)[None, :]
        e_lo = lookup(pair_tab[:, 0], bucket)[:, None]
        e_hi = lookup(pair_tab[:, 1], bucket)[:, None]
        dense_gates = jnp.where(lane == e_lo, gates[:, 0:1], 0.0) + jnp.where(lane == e_hi, gates[:, 1:2], 0.0)
        return _moe_dense(x, dense_gates, wg, wu, wd, ln_g, ln_b, alpha)
    blk = MOE_BLOCK
    counts = cnt[0, :n_buckets].astype(I32)
    padded = (counts + blk - 1) // blk * blk
    pend = jnp.cumsum(padded)
    poff = pend - padded
    dest =(lookup(poff, bucket) + rank).reshape(1, t)
    n_blocks = -(-t // blk) + n_buckets
    starts = jnp.arange(n_blocks, dtype=I32) * blk
    block_bucket = jnp.minimum(jnp.sum(pend[None, :] <= starts[:, None], axis=1), n_buckets - 1)
    block_ea = lookup(pair_tab[:, 0], block_bucket)
    block_eb = lookup(pair_tab[:, 1], block_bucket)
    n_used = (pend[-1] // blk).astype(I32).reshape(1)
    xs = _dispatch(dest, x, n_blocks * blk)
    ys = _experts(block_ea, block_eb, n_used, xs, wg, wu, wd, blk)
    return _combine(dest, x, gates, ys, ln_g, ln_b, alpha)


def _head_sum(x, ones):
    hi = x.astype(BF16)
    lo = (x - hi.astype(F32)).astype(BF16)
    parts = []
    for j in range(x.shape[1] // LANES):
        sl = slice(j * LANES, (j + 1) * LANES)
        parts.append(_bdot(hi[:, sl], ones) + _bdot(lo[:, sl], ones))
    return jnp.concatenate(parts, axis=-1)


def _rw_prep_kernel(sh_ref, halo_ref, sh0_ref, mu_ref, w0_ref, w2_ref, a0_ref, a2_ref, g2_ref,
                    kk_ref, ka_ref, rk_ref, ones_ref,
                    r_o, w_o, k_o, v_o, kk_o, b_o, bonus_o, g_o, *, batch, cw):
    sh = sh_ref[...]
    first = jnp.where(pl.program_id(0) == 0, sh0_ref[...], halo_ref[...])
    prev = jnp.concatenate([first, sh[:sh.shape[0] - batch]], axis=0)
    shm = sh + (prev - sh) * mu_ref[...]
    r, k, v = shm[:, :cw], shm[:, cw:2 * cw], shm[:, 2 * cw:3 * cw]
    xwa = shm[:, 3 * cw:3 * cw + LANES]
    xg = shm[:, 3 * cw + LANES:]
    ones = ones_ref[...]
    z = w0_ref[...] + _bdot(jnp.tanh(xwa).astype(BF16), w2_ref[...])
    softplus = jnp.maximum(-z, 0.0) + jnp.log1p(jnp.exp(-jnp.abs(z)))
    w_o[...] = jnp.exp(-jnp.exp(-softplus - 0.5))
    a = _sigmoid(a0_ref[...] + _bdot(xwa.astype(BF16), a2_ref[...]))
    g_o[...] = _bdot(_sigmoid(xg).astype(BF16), g2_ref[...])
    kk = k * kk_ref[...]
    kk = kk / jnp.maximum(jnp.sqrt(_head_sum(kk * kk, ones)), 1e-12)
    k2 = k * (1.0 + (a - 1.0) * ka_ref[...])
    r_o[...] = r
    k_o[...] = k2
    v_o[...] = v
    kk_o[...] = kk
    b_o[...] = kk * a
    bonus_o[...] = _head_sum(r * k2 * rk_ref[...], ones) * v


def _rw_prep(sh, shift0, batch, prm):
    t, sw = sh.shape
    cw = prm["w0"].shape[-1]
    tm = min(256, t)
    hb = tm // batch
    row = lambda i: (i, 0)
    fix = lambda i: (0, 0)
    vec = lambda n: pl.BlockSpec((1, n), fix)
    return pl.pallas_call(
        functools.partial(_rw_prep_kernel, batch=batch, cw=cw),
        grid=(t // tm,),
        in_specs=[pl.BlockSpec((tm, sw), row),
                  pl.BlockSpec((batch, sw), lambda i: (jnp.maximum(i * hb - 1, 0), 0)),
                  pl.BlockSpec((batch, sw), fix), vec(sw), vec(cw),
                  pl.BlockSpec((LANES, cw), fix), vec(cw), pl.BlockSpec((LANES, cw), fix),
                  pl.BlockSpec((LANES, cw), fix), vec(cw), vec(cw), vec(cw),
                  pl.BlockSpec((LANES, LANES), fix)],
        out_specs=[pl.BlockSpec((tm, cw), row)] * 8,
        out_shape=[jax.ShapeDtypeStruct((t, cw), F32)] * 8,
        compiler_params=_params("parallel"),
        name="rwkv_prep",
    )(sh, sh, shift0, prm["mu"], prm["w0"], prm["w2"], prm["a0"], prm["a2"], prm["g2"],
      prm["k_k"], prm["k_a"], prm["r_k"], prm["ones"])


RW_TILES_A = 4
RW_TILES_B = 2
RW_TICK = 8


def _rwkv_kernel(r_ref, w_ref, k_ref, v_ref, kk_ref, b_ref, sa0_ref, sb0_ref, y_ref, san_ref, sbn_ref,
                 st_a, st_b, xa0, xb0, xa1, xb1, *, batch, steps):
    hd = HEAD_DIM
    hv = hd // 2
    ti = pl.program_id(0)

    @pl.when(ti == 0)
    def _():
        st_a[...] = sa0_ref[...]
        st_b[...] = sb0_ref[...]

    operands = (r_ref, w_ref, k_ref, v_ref, kk_ref, b_ref)
    def lane_of(rows):
        return lax.broadcasted_iota(I32, (rows, LANES), 1)

    lane_lo = lane_of(hd) < hd
    lane_lo_out = lane_of(batch) < hd
    even_quarter = (lane_of(hd) // hv) % 2 == 0
    quarter_v = lane_of(hv) // hv
    quarter_out = lane_of(batch) // hv

    def pick_quarter(q, blocks):
        return jnp.where(q == 0, blocks[0], jnp.where(q == 1, blocks[1], jnp.where(q == 2, blocks[2], blocks[3])))

    def padded(parts, rows):
        have = len(parts) * batch
        return parts + ([jnp.zeros((rows - have, LANES), F32)] if have < rows else [])

    def run(state, x, nv, tick):
        acc = jnp.zeros((nv, LANES), F32)
        for kidx in range(hd):
            acc = acc + state[kidx] * x[4, kidx:kidx + 1, :]
            if kidx % RW_TICK == RW_TICK - 1:
                tick()
        sa = acc
        vv = x[3, 0:nv, :]
        acc = jnp.zeros((nv, LANES), F32)
        for kidx in range(hd):
            row = slice(kidx, kidx + 1)
            sn = state[kidx] * x[1, row, :] - sa * x[5, row, :] + vv * x[2, row, :]
            state[kidx] = sn
            acc = acc + sn * x[0, row, :]
            if kidx % RW_TICK == RW_TICK - 1:
                tick()
        return acc

    def transpose_operand(o, r0, xa, xb):
        slab = operands[o][pl.ds(r0, batch), :]
        tiles = [slab[:, j * LANES:(j + 1) * LANES] for j in range(RW_TILES_A + RW_TILES_B)]
        ta = jnp.concatenate(padded(tiles[:RW_TILES_A], hd) * 2, axis=0).T
        xa[o] = jnp.where(lane_lo, ta[:hd], ta[hd:])
        tb = jnp.concatenate(padded(tiles[RW_TILES_A:], hv) * 4, axis=0).T
        if o == 3:
            xb[o, pl.ds(0, hv), :] = pick_quarter(quarter_v, (tb[:hv], tb[hd:hd + hv], tb[hv:hd], tb[hd + hv:]))
        else:
            xb[o] = jnp.where(even_quarter, tb[:hd], tb[hd:])

    def time_step(t, cur, nxt):
        r0 = pl.multiple_of(t * batch, batch)
        rn = pl.multiple_of(jnp.minimum(t + 1, steps - 1) * batch, batch)
        pending = [functools.partial(transpose_operand, o, rn, *nxt) for o in range(len(operands))]

        def tick():
            if pending:
                pending.pop(0)()

        ya = run(st_a, cur[0], hd, tick)
        yb = run(st_b, cur[1], hv, tick)
        while pending:
            tick()
        mta = jnp.concatenate([ya, ya], axis=0).T
        for j in range(RW_TILES_A):
            lo = mta[j * batch:(j + 1) * batch]
            hi = mta[hd + j * batch:hd + (j + 1) * batch]
            y_ref[pl.ds(r0, batch), j * LANES:(j + 1) * LANES] = jnp.where(lane_lo_out, lo, hi)
        mtb = jnp.concatenate([yb] * 4, axis=0).T
        for j in range(RW_TILES_B):
            blocks = [mtb[base + j * batch:base + (j + 1) * batch] for base in (0, hd, hv, hd + hv)]
            tile = RW_TILES_A + j
            y_ref[pl.ds(r0, batch), tile * LANES:(tile + 1) * LANES] = pick_quarter(quarter_out, blocks)

    even_bufs, odd_bufs = (xa0, xb0), (xa1, xb1)
    for o in range(len(operands)):
        transpose_operand(o, 0, *even_bufs)

    def step_pair(p, carry):
        time_step(2 * p, even_bufs, odd_bufs)
        time_step(2 * p + 1, odd_bufs, even_bufs)
        return carry

    lax.fori_loop(0, steps // 2, step_pair, 0)

    @pl.when(ti == pl.num_programs(0) - 1)
    def _():
        san_ref[...] = st_a[...]
        sbn_ref[...] = st_b[...]


def _rwkv(ops, s0, batch):
    t, cw = ops[0].shape
    hd, hv = HEAD_DIM, HEAD_DIM // 2
    assert cw == (RW_TILES_A + RW_TILES_B) * LANES and RW_TILES_A * batch <= hd
    ca, cb = RW_TILES_A * batch, RW_TILES_B * batch
    if s0 is None:
        sa0 = jnp.zeros((hd, hd, LANES), F32)
        sb0 = jnp.zeros((hd, hv, LANES), F32)
    else:
        s0 = s0.astype(F32)
        sa0 = s0[:, :2 * RW_TILES_A].reshape(batch, RW_TILES_A, 2, hd, hd).transpose(4, 3, 2, 1, 0)
        sa0 = jnp.pad(sa0.reshape(hd, hd, 2, ca), ((0, 0),) * 3 + ((0, hd - ca),)).reshape(hd, hd, LANES)
        sb0 = s0[:, 2 * RW_TILES_A:].reshape(batch, RW_TILES_B, 2, 2, hv, hd).transpose(5, 4, 3, 2, 1, 0)
        sb0 = jnp.pad(sb0.reshape(hd, hv, 2, 2, cb), ((0, 0),) * 4 + ((0, hv - cb),)).reshape(hd, hv, LANES)
    steps = min(32, t // batch)
    assert steps % 2 == 0, "the time loop is unrolled in even/odd operand-buffer pairs"
    rows = steps * batch
    blk = pl.BlockSpec((rows, cw), lambda i: (i, 0))
    st_a = pl.BlockSpec((hd, hd, LANES), lambda i: (0, 0, 0))
    st_b = pl.BlockSpec((hd, hv, LANES), lambda i: (0, 0, 0))
    y, san, sbn = pl.pallas_call(
        functools.partial(_rwkv_kernel, batch=batch, steps=steps),
        grid=(t // rows,),
        in_specs=[blk] * 6 + [st_a, st_b],
        out_specs=[blk, st_a, st_b],
        out_shape=[jax.ShapeDtypeStruct((t, cw), F32), jax.ShapeDtypeStruct((hd, hd, LANES), F32),
                   jax.ShapeDtypeStruct((hd, hv, LANES), F32)],
        scratch_shapes=[pltpu.VMEM((hd, hd, LANES), F32), pltpu.VMEM((hd, hv, LANES), F32),
                        ] + [pltpu.VMEM((6, hd, LANES), F32)] * 4,
        compiler_params=_params("arbitrary"),
        name="rwkv_recurrence",
    )(*ops, sa0, sb0)
    sa = san.reshape(hd, hd, 2, hd)[..., :ca].reshape(hd, hd, 2, RW_TILES_A, batch).transpose(4, 3, 2, 1, 0)
    sb = sbn.reshape(hd, hv, 2, 2, hv)[..., :cb].reshape(hd, hv, 2, 2, RW_TILES_B, batch).transpose(5, 4, 3, 2, 1, 0)
    s_last = jnp.concatenate([sa.reshape(batch, 2 * RW_TILES_A, hd, hd), sb.reshape(batch, 2 * RW_TILES_B, hd, hd)],
                             axis=1)
    return y, s_last


def _rw_post_kernel(y_ref, bonus_ref, g_ref, lg_ref, lb_ref, ones_ref, o_ref):
    y = y_ref[...]
    ones = ones_ref[...]
    mu = _head_sum(y, ones) * (1.0 / HEAD_DIM)
    d = y - mu
    var = _head_sum(d * d, ones) * (1.0 / HEAD_DIM)
    yn = d * lax.rsqrt(var + GN_EPS) * lg_ref[...] + lb_ref[...]
    o_ref[...] = ((yn + bonus_ref[...]) * g_ref[...]).astype(o_ref.dtype)


def _rw_post(y, bonus, g, ln_g, ln_b, ones):
    t, cw = y.shape
    tm = min(ROW_TILE, t)
    row = pl.BlockSpec((tm, cw), lambda i: (i, 0))
    vec = pl.BlockSpec((1, cw), lambda i: (0, 0))
    return pl.pallas_call(
        _rw_post_kernel, grid=(t // tm,),
        in_specs=[row, row, row, vec, vec, pl.BlockSpec((LANES, LANES), lambda i: (0, 0))],
        out_specs=row, out_shape=jax.ShapeDtypeStruct((t, cw), BF16),
        compiler_params=_params("parallel"),
        name="rwkv_post",
    )(y, bonus, g, ln_g, ln_b, ones)


def _pool_kernel(main_ref, halo_ref, w_ref, scale_ref, o_ref, *, batch, pos0):
    tm = main_ref.shape[0]
    full = jnp.concatenate([main_ref[...], halo_ref[...]], axis=0)
    b = batch
    n = full.shape[0]
    a2 = full[b:] + full[:n - b]
    a4 = a2[2 * b:] + a2[:n - 3 * b]
    a8 = a4[4 * b:] + a4[:n - 7 * b]
    a16 = a8[8 * b:] + a8[:n - 15 * b]
    cur = full[16 * b:16 * b + tm]
    sums = (a2[15 * b:15 * b + tm], a4[13 * b:13 * b + tm], a8[9 * b:9 * b + tm], a16[b:b + tm])
    lane = lax.broadcasted_iota(I32, cur.shape, 1)
    grp = lane // (cur.shape[1] // len(POOL_WINDOWS))
    win_sum = sums[-1]
    width = jnp.full(cur.shape, float(POOL_WINDOWS[-1]), F32)
    for j in range(len(POOL_WINDOWS) - 2, -1, -1):
        win_sum = jnp.where(grp == j, sums[j], win_sum)
        width = jnp.where(grp == j, float(POOL_WINDOWS[j]), width)
    row = lax.broadcasted_iota(I32, cur.shape, 0)
    t_idx = pl.program_id(0) * (tm // b) + row // b
    cnt = jnp.minimum(width, (pos0 + t_idx + 1).astype(F32))
    pooled = win_sum / cnt - cur
    o_ref[...] = (_bdot(pooled.astype(BF16), w_ref[...]) * scale_ref[...]).astype(o_ref.dtype)


def _pool(fp, t, batch, w_bd, scale, pos0):
    dw = fp.shape[1]
    tm = min(ROW_TILE, t)
    halo = 16 * batch
    return pl.pallas_call(
        functools.partial(_pool_kernel, batch=batch, pos0=pos0),
        grid=(t // tm,),
        in_specs=[pl.BlockSpec((tm, dw), lambda i: (i, 0)),
                  pl.BlockSpec((halo, dw), lambda i: ((i + 1) * (tm // halo), 0)),
                  pl.BlockSpec((dw, dw), lambda i: (0, 0)), pl.BlockSpec((1, dw), lambda i: (0, 0))],
        out_specs=pl.BlockSpec((tm, dw), lambda i: (i, 0)),
        out_shape=jax.ShapeDtypeStruct((t, dw), BF16),
        compiler_params=_params("parallel"),
        name="causal_pool",
    )(fp, fp, w_bd, scale)


def _to_batch_major(a, seq, batch):
    return a.reshape(seq, batch, a.shape[-1]).transpose(1, 0, 2)


def _to_time_major(a):
    bsz, seq, width = a.shape
    return a.transpose(1, 0, 2).reshape(seq * bsz, width)


def _even_mixer(x, batch, seq, h0, cache_k, cache_v, w, j, bias, alpha, ln_g, ln_b):
    a_width = w["s5_w_glu"].shape[-1]
    w_in = w["ev_w_in"][j]
    qw = a_width
    kvw = (w_in.shape[1] - 2 * a_width) // 2
    u, q, k, v = _matmul([x], [w_in.astype(BF16)], splits=(a_width, qw, kvw, kvw))
    prm = _s5_params(w["s5_a_re"][j], w["s5_a_im"][j], w["s5_log_dt"][j], w["s5_b_re"][j], w["s5_b_im"][j],
                     w["s5_c_re"][j], w["s5_c_im"][j], w["s5_d"][j])
    nt = a_width // LANES
    ns = prm["bre"].shape[-1]
    if h0 is None:
        h0re = jnp.zeros((nt, batch, ns), F32)
        h0im = h0re
    else:
        h0re = h0[..., 0].astype(F32).reshape(batch, nt, ns).transpose(1, 0, 2)
        h0im = h0[..., 1].astype(F32).reshape(batch, nt, ns).transpose(1, 0, 2)
    g, hn_re, hn_im = _s5(u, batch, h0re, h0im, prm)
    groups, n_state = w["s5_a_re"][j].shape
    h_last = jnp.stack([hn_re.transpose(1, 0, 2).reshape(batch, groups, n_state),
                        hn_im.transpose(1, 0, 2).reshape(batch, groups, n_state)], axis=-1)
    a_out = _matmul([g], [w["s5_w_glu"][j].astype(BF16)], epilogue="glu", out_dtype=BF16)

    sinks = w["swa_sinks"][j].astype(F32)
    if cache_k is None:
        att = _attention(q, k, v, bias, sinks, batch, CHUNK, CHUNK, WINDOW // CHUNK + 1, True)
        k_all, v_all = k, v
    else:
        k_all = jnp.concatenate([_to_time_major(cache_k.astype(F32).reshape(batch, WINDOW, kvw)), k], axis=0)
        v_all = jnp.concatenate([_to_time_major(cache_v.astype(F32).reshape(batch, WINDOW, kvw)), v], axis=0)
        att = _attention(q, k_all, v_all, bias, sinks, batch, seq, WINDOW + seq, 1, False)
    keep = k_all.shape[0] - WINDOW * batch
    new_k = _to_batch_major(k_all[keep:], WINDOW, batch)
    new_v = _to_batch_major(v_all[keep:], WINDOW, batch)
    w_out = w["ev_w_out"][j].astype(BF16)
    xn = _matmul([a_out, att], [w_out[:a_width], w_out[a_width:]], epilogue="ln",
                 res=x, g=ln_g, b=ln_b, alpha=alpha)
    kv_shape = (batch, WINDOW, kvw // HEAD_DIM, HEAD_DIM)
    return xn, h_last, new_k.reshape(kv_shape), new_v.reshape(kv_shape)


def _odd_mixer(x, batch, seq, shift0, s0, pool0, pos0, w, j, alpha, ln_g, ln_b):
    t = x.shape[0]
    w_in = w["od_w_in"][j]
    cw = w["rw_w0"].shape[-1]
    sw = w["rw_mu"].shape[-1]
    dw = w_in.shape[1] - sw
    n_heads = cw // HEAD_DIM
    sh, p = _matmul([x], [w_in.astype(BF16)], splits=(sw, dw))
    zeros_half = jnp.zeros((LANES - w["rw_w2"].shape[1], cw), F32)
    head_lane = jnp.arange(LANES) // HEAD_DIM
    ones = (head_lane[:, None] == head_lane[None, :]).astype(BF16)
    prm = dict(
        mu=w["rw_mu"][j].astype(F32).reshape(1, sw), w0=w["rw_w0"][j].astype(F32).reshape(1, cw),
        w2=jnp.concatenate([w["rw_w2"][j].astype(F32), zeros_half], axis=0).astype(BF16),
        a0=w["rw_a0"][j].astype(F32).reshape(1, cw),
        a2=jnp.concatenate([zeros_half, w["rw_a2"][j].astype(F32)], axis=0).astype(BF16),
        g2=w["rw_g2"][j].astype(BF16),
        k_k=w["rw_k_k"][j].astype(F32).reshape(1, cw), k_a=w["rw_k_a"][j].astype(F32).reshape(1, cw),
        r_k=w["rw_r_k"][j].astype(F32).reshape(1, cw), ones=ones)
    if shift0 is None:
        shift0 = jnp.zeros((batch, sw), F32)
    r, dec, k2, v, kk, b, bonus, g = _rw_prep(sh, shift0.astype(F32), batch, prm)

    y, s_last = _rwkv((r, dec, k2, v, kk, b), s0, batch)
    c_out = _rw_post(y, bonus, g, w["rw_ln_g"][j].astype(F32).reshape(1, cw),
                     w["rw_ln_b"][j].astype(F32).reshape(1, cw), ones)

    if pool0 is None:
        hist = jnp.zeros((POOL_HIST * batch, dw), F32)
    else:
        hist = _to_time_major(pool0.astype(F32))
    fp = jnp.concatenate([jnp.zeros((batch, dw), F32), hist, p], axis=0)
    n_grp = len(POOL_WINDOWS)
    pg = dw // n_grp
    eye = jnp.eye(n_grp, dtype=F32)
    w_bd = (w["pool_w"][j].astype(F32)[:, :, None, :] * eye[:, None, :, None]).reshape(dw, dw).astype(BF16)
    d_out = _pool(fp, t, batch, w_bd, w["pool_scale"][j].astype(F32).reshape(1, dw), pos0)
    pool_new = _to_batch_major(fp[fp.shape[0] - POOL_HIST * batch:], POOL_HIST, batch)

    w_out = w["od_w_out"][j].astype(BF16)
    xn = _matmul([c_out, d_out], [w_out[:cw], w_out[cw:]], epilogue="ln", res=x, g=ln_g, b=ln_b, alpha=alpha)
    sh_last = sh[t - batch:]
    return xn, sh_last, s_last, pool_new


def _trunk(x3, s5_0, k_0, v_0, rw_0, shift_0, pool_0, pos0, w):
    batch, seq, _ = x3.shape
    depth = w["ln1_g"].shape[0]
    alpha = (2 * depth) ** 0.25
    x = _to_time_major(x3.astype(F32))
    prompt = s5_0 is None
    if prompt:
        bias = _bias_table(w["rel_bias"], CHUNK, (WINDOW // CHUNK + 1) * CHUNK)
    else:
        bias = _bias_table(w["rel_bias"], seq, WINDOW + seq)
    s5_new, k_new, v_new, rw_new, sh_new, pool_new = [], [], [], [], [], []
    for i in range(depth):
        j = i // 2
        if i % 2 == 0:
            x, h1, nk, nv = _even_mixer(
                x, batch, seq, None if prompt else s5_0[j], None if prompt else k_0[j],
                None if prompt else v_0[j], w, j, bias, alpha, w["ln1_g"][i], w["ln1_b"][i])
            s5_new.append(h1)
            k_new.append(nk)
            v_new.append(nv)
        else:
            x, sh1, s1, pl1 = _odd_mixer(
                x, batch, seq, None if prompt else shift_0[j], None if prompt else rw_0[j],
                None if prompt else pool_0[j], pos0, w, j, alpha, w["ln1_g"][i], w["ln1_b"][i])
            sh_new.append(sh1)
            rw_new.append(s1)
            pool_new.append(pl1)
        x = _moe_ln(x, w["moe_w_coarse"][i], w["moe_b_coarse"][i], w["moe_w_fine"][i], w["moe_b_fine"][i],
                    w["moe_w_gate"][i].astype(BF16), w["moe_w_up"][i].astype(BF16),
                    w["moe_w_down"][i].astype(BF16), w["ln2_g"][i], w["ln2_b"][i], alpha)
    y = _to_batch_major(x, seq, batch)
    return (y, jnp.stack(s5_new), jnp.stack(k_new), jnp.stack(v_new),
            jnp.stack(rw_new), jnp.stack(sh_new), jnp.stack(pool_new))


def kernel(x_prompt, x_sample, state_s5, cache_swa_k, cache_swa_v, state_rwkv, state_shift, state_pool, rel_bias, ev_w_in, ev_w_out, s5_a_re, s5_a_im, s5_log_dt, s5_b_re, s5_b_im, s5_c_re, s5_c_im, s5_d, s5_w_glu, swa_sinks, od_w_in, od_w_out, rw_mu, rw_w0, rw_w2, rw_a0, rw_a2, rw_g2, rw_k_k, rw_k_a, rw_r_k, rw_ln_g, rw_ln_b, pool_w, pool_scale, ln1_g, ln1_b, ln2_g, ln2_b, moe_w_coarse, moe_b_coarse, moe_w_fine, moe_b_fine, moe_w_gate, moe_w_up, moe_w_down):
    w = dict(rel_bias=rel_bias, ev_w_in=ev_w_in, ev_w_out=ev_w_out, s5_a_re=s5_a_re, s5_a_im=s5_a_im,
             s5_log_dt=s5_log_dt, s5_b_re=s5_b_re, s5_b_im=s5_b_im, s5_c_re=s5_c_re, s5_c_im=s5_c_im,
             s5_d=s5_d, s5_w_glu=s5_w_glu, swa_sinks=swa_sinks, od_w_in=od_w_in, od_w_out=od_w_out,
             rw_mu=rw_mu, rw_w0=rw_w0, rw_w2=rw_w2, rw_a0=rw_a0, rw_a2=rw_a2, rw_g2=rw_g2, rw_k_k=rw_k_k,
             rw_k_a=rw_k_a, rw_r_k=rw_r_k, rw_ln_g=rw_ln_g, rw_ln_b=rw_ln_b, pool_w=pool_w,
             pool_scale=pool_scale, ln1_g=ln1_g, ln1_b=ln1_b, ln2_g=ln2_g, ln2_b=ln2_b,
             moe_w_coarse=moe_w_coarse, moe_b_coarse=moe_b_coarse, moe_w_fine=moe_w_fine,
             moe_b_fine=moe_b_fine, moe_w_gate=moe_w_gate, moe_w_up=moe_w_up, moe_w_down=moe_w_down)
    y_s, s5_s, k_s, v_s, rw_s, sh_s, pool_s = _trunk(
        x_sample, state_s5, cache_swa_k, cache_swa_v, state_rwkv, state_shift, state_pool, PAST_LEN, w)
    y_p, s5_p, k_p, v_p, rw_p, sh_p, pool_p = _trunk(x_prompt, None, None, None, None, None, None, 0, w)
    return (y_p, y_s, s5_p, s5_s, k_p, v_p, k_s, v_s, rw_p, rw_s, sh_p, sh_s, pool_p, pool_s)
```

```python
import functools
import math

import jax
import jax.numpy as jnp
from jax import lax
from jax.experimental import pallas as pl
from jax.experimental.pallas import tpu as pltpu

F32 = jnp.float32
BF16 = jnp.bfloat16
I32 = jnp.int32

CHUNK = 64
WINDOW = 128
HEAD_DIM = 64
PAST_LEN = 1024
N_BUCKETS = 32
MAX_DISTANCE = 128
POOL_WINDOWS = (2, 4, 8, 16)
POOL_HIST = max(POOL_WINDOWS) - 1
E_GROUPS = 4
LN_EPS = 1e-5
GN_EPS = 64e-5
A_GROUP = 16

LANES = 128
SUBLANES = 8
VMEM_LIMIT_BYTES = 56 * 1024 * 1024

ROW_TILE = 512
MOE_BLOCK = 256
MOE_DENSE_BELOW = 8 * MOE_BLOCK


def _params(*sem):
    return pltpu.CompilerParams(dimension_semantics=sem, vmem_limit_bytes=VMEM_LIMIT_BYTES)


def _sigmoid(x):
    return 1.0 / (1.0 + jnp.exp(-x))


def _layer_norm(y, g, b):
    mu = jnp.mean(y, axis=-1, keepdims=True)
    d = y - mu
    var = jnp.mean(d * d, axis=-1, keepdims=True)
    return d * lax.rsqrt(var + LN_EPS) * g + b


def _bdot(a, b):
    return jnp.dot(a, b, preferred_element_type=F32)


def _mm_kernel(*refs, n_in, splits, epilogue, alpha):
    xs = refs[:n_in]
    ws = refs[n_in:2 * n_in]
    pos = 2 * n_in
    acc = None
    for x_ref, w_ref in zip(xs, ws):
        d = _bdot(x_ref[...].astype(BF16), w_ref[...])
        acc = d if acc is None else acc + d
    if epilogue == "ln":
        res_ref, g_ref, b_ref = refs[pos:pos + 3]
        pos += 3
        acc = _layer_norm(alpha * res_ref[...] + acc, g_ref[...], b_ref[...])
    elif epilogue == "glu":
        acc = xs[0][...] * _sigmoid(acc)
    off = 0
    for o_ref, n in zip(refs[pos:], splits):
        o_ref[...] = acc[:, off:off + n].astype(o_ref.dtype)
        off += n


def _matmul(xs, ws, splits=None, epilogue=None, res=None, g=None, b=None, alpha=None, out_dtype=F32):
    t = xs[0].shape[0]
    n = ws[0].shape[1]
    splits = tuple(splits) if splits else (n,)
    tm = min(ROW_TILE, t)
    in_specs = [pl.BlockSpec((tm, x.shape[1]), lambda i: (i, 0)) for x in xs]
    in_specs += [pl.BlockSpec(w.shape, lambda i: (0, 0)) for w in ws]
    args = list(xs) + list(ws)
    if epilogue == "ln":
        in_specs += [pl.BlockSpec((tm, n), lambda i: (i, 0)),
                     pl.BlockSpec((1, n), lambda i: (0, 0)), pl.BlockSpec((1, n), lambda i: (0, 0))]
        args += [res, g.reshape(1, n), b.reshape(1, n)]
    outs = pl.pallas_call(
        functools.partial(_mm_kernel, n_in=len(xs), splits=splits, epilogue=epilogue, alpha=alpha),
        grid=(t // tm,),
        in_specs=in_specs,
        out_specs=[pl.BlockSpec((tm, s), lambda i: (i, 0)) for s in splits],
        out_shape=[jax.ShapeDtypeStruct((t, s), out_dtype) for s in splits],
        compiler_params=_params("parallel"),
        name="matmul_" + (epilogue or "plain"),
    )(*args)
    return outs if len(splits) > 1 else outs[0]


def _s5_kernel(u_ref, bre_ref, bim_ref, cre_ref, cim_ref, are_ref, aim_ref, d_ref, h0re_ref, h0im_ref,
               g_ref, hnre_ref, hnim_ref, hre, him, car_re, car_im, *, batch, steps):
    ti = pl.program_id(1)

    @pl.when(ti == 0)
    def _():
        car_re[...] = h0re_ref[...]
        car_im[...] = h0im_ref[...]

    u = u_ref[...]
    ub = u.astype(BF16)
    hre[...] = _bdot(ub, bre_ref[...])
    him[...] = _bdot(ub, bim_ref[...])
    a_re = jnp.broadcast_to(are_ref[...], car_re.shape)
    a_im = jnp.broadcast_to(aim_ref[...], car_re.shape)

    def step(t, carry):
        h_re, h_im = carry
        r0 = pl.multiple_of(t * batch, batch)
        n_re = a_re * h_re - a_im * h_im + hre[pl.ds(r0, batch), :]
        n_im = a_re * h_im + a_im * h_re + him[pl.ds(r0, batch), :]
        hre[pl.ds(r0, batch), :] = n_re
        him[pl.ds(r0, batch), :] = n_im
        return n_re, n_im

    h_re, h_im = lax.fori_loop(0, steps, step, (car_re[...], car_im[...]), unroll=4)
    car_re[...] = h_re
    car_im[...] = h_im
    hnre_ref[...] = h_re
    hnim_ref[...] = h_im
    y = (_bdot(hre[...].astype(BF16), cre_ref[...]) - _bdot(him[...].astype(BF16), cim_ref[...])
         + d_ref[...] * u)
    cdf = 0.5 * (1.0 + jnp.tanh(math.sqrt(2.0 / math.pi) * (y + 0.044715 * (y * y * y))))
    g_ref[...] = y * cdf


def _s5(u, batch, h0re, h0im, prm):
    t, width = u.shape
    nt = width // LANES
    ns = prm["bre"].shape[-1]
    rows = min(1024, t)
    steps = rows // batch
    tile = lambda j, i: (j, 0, 0)
    return pl.pallas_call(
        functools.partial(_s5_kernel, batch=batch, steps=steps),
        grid=(nt, t // rows),
        in_specs=[pl.BlockSpec((rows, LANES), lambda j, i: (i, j)),
                  pl.BlockSpec((None, LANES, ns), tile), pl.BlockSpec((None, LANES, ns), tile),
                  pl.BlockSpec((None, ns, LANES), tile), pl.BlockSpec((None, ns, LANES), tile),
                  pl.BlockSpec((None, 1, ns), tile), pl.BlockSpec((None, 1, ns), tile),
                  pl.BlockSpec((None, 1, LANES), tile),
                  pl.BlockSpec((None, batch, ns), tile), pl.BlockSpec((None, batch, ns), tile)],
        out_specs=[pl.BlockSpec((rows, LANES), lambda j, i: (i, j)),
                   pl.BlockSpec((None, batch, ns), tile), pl.BlockSpec((None, batch, ns), tile)],
        out_shape=[jax.ShapeDtypeStruct((t, width), F32),
                   jax.ShapeDtypeStruct((nt, batch, ns), F32), jax.ShapeDtypeStruct((nt, batch, ns), F32)],
        scratch_shapes=[pltpu.VMEM((rows, ns), F32), pltpu.VMEM((rows, ns), F32),
                        pltpu.VMEM((batch, ns), F32), pltpu.VMEM((batch, ns), F32)],
        compiler_params=_params("parallel", "arbitrary"),
        name="s5_scan",
    )(u, prm["bre"], prm["bim"], prm["cre"], prm["cim"], prm["are"], prm["aim"], prm["d"], h0re, h0im)


def _s5_params(a_re, a_im, log_dt, b_re, b_im, c_re, c_im, d_skip):
    groups, n_state = a_re.shape
    gpt = LANES // A_GROUP
    nt = groups // gpt
    lam = lax.complex(a_re.astype(F32), a_im.astype(F32))
    dt = jnp.exp(log_dt.astype(F32))[:, None]
    a_bar = jnp.exp(lam * dt)
    b_bar = ((a_bar - 1.0) / lam)[..., None] * lax.complex(b_re.astype(F32), b_im.astype(F32))
    eye = jnp.eye(gpt, dtype=F32)

    def in_proj(m):
        m = m.reshape(nt, gpt, n_state, A_GROUP).transpose(0, 1, 3, 2)
        return (m[:, :, :, None, :] * eye[None, :, None, :, None]).reshape(nt, gpt * A_GROUP, gpt * n_state)

    def out_proj(m):
        m = m.astype(F32).reshape(nt, gpt, A_GROUP, n_state).transpose(0, 1, 3, 2)
        return (m[:, :, :, None, :] * eye[None, :, None, :, None]).reshape(nt, gpt * n_state, gpt * A_GROUP)

    return dict(bre=in_proj(b_bar.real).astype(BF16), bim=in_proj(b_bar.imag).astype(BF16),
                cre=out_proj(c_re).astype(BF16), cim=out_proj(c_im).astype(BF16),
                are=a_bar.real.reshape(nt, 1, gpt * n_state), aim=a_bar.imag.reshape(nt, 1, gpt * n_state),
                d=d_skip.astype(F32).reshape(nt, 1, LANES))


def _attn_kernel(*refs, n_kv, masked, tiles_per_kv, batch):
    n_q = 2 * tiles_per_kv
    q_refs = refs[:n_q]
    k_refs = refs[n_q:n_q + n_kv]
    v_refs = refs[n_q + n_kv:n_q + 2 * n_kv]
    bias_ref, sink_ref, o_ref, o_tiles = refs[n_q + 2 * n_kv:]
    c = pl.program_id(0)
    cq = q_refs[0].shape[0] // batch
    ck = k_refs[0].shape[0] // batch
    nk = n_kv * ck
    low_half = lax.broadcasted_iota(I32, (nk, LANES), 1) < HEAD_DIM
    if masked:
        key = lax.broadcasted_iota(I32, (nk, tiles_per_kv * cq), 0)
        valid = (c - (n_kv - 1)) * CHUNK + key >= 0

    def one_sequence(b, carry):
        kt = jnp.concatenate([r[pl.ds(b, ck, stride=batch), :] for r in k_refs], axis=0)
        vt = jnp.concatenate([r[pl.ds(b, ck, stride=batch), :] for r in v_refs], axis=0)
        k_sw = pltpu.roll(kt, HEAD_DIM, 1)
        v_sw = pltpu.roll(vt, HEAD_DIM, 1)

        def placed(t, t_sw, kvh, half):
            src = t if kvh == half else t_sw
            keep = low_half if half == 0 else jnp.logical_not(low_half)
            return jnp.where(keep, src, 0.0).astype(BF16)

        scores = []
        for kvh in range(2):
            tiles = [q_refs[kvh * tiles_per_kv + i][pl.ds(b, cq, stride=batch), :] for i in range(tiles_per_kv)]
            qs = jnp.concatenate(tiles, axis=0).astype(BF16)
            for half in range(2):
                s = lax.dot_general(placed(kt, k_sw, kvh, half), qs, (((1,), (1,)), ((), ())),
                                    preferred_element_type=F32)
                s = s * (HEAD_DIM ** -0.5) + bias_ref[2 * kvh + half]
                if masked:
                    s = jnp.where(valid, s, -jnp.inf)
                scores.append(s)
        probs = []
        for grp, s in enumerate(scores):
            sink = sink_ref[grp]
            m = jnp.maximum(jnp.max(s, axis=0, keepdims=True), sink)
            p = jnp.exp(s - m)
            denom = jnp.sum(p, axis=0, keepdims=True) + jnp.exp(sink - m)
            probs.append((p / denom).astype(BF16))
        tn = (((0,), (0,)), ((), ()))
        for kvh in range(2):
            o_t = (lax.dot_general(placed(vt, v_sw, kvh, 0), probs[2 * kvh], tn, preferred_element_type=F32)
                   + lax.dot_general(placed(vt, v_sw, kvh, 1), probs[2 * kvh + 1], tn, preferred_element_type=F32))
            o = o_t.T
            for i in range(tiles_per_kv):
                tile = kvh * tiles_per_kv + i
                o_tiles[tile, pl.ds(b, cq, stride=batch), :] = o[i * cq:(i + 1) * cq]
        return carry

    lax.fori_loop(0, batch, one_sequence, 0, unroll=2)
    for tile in range(n_q):
        o_ref[:, tile * LANES:(tile + 1) * LANES] = o_tiles[tile].astype(o_ref.dtype)


def _attention(q, k, v, bias, sinks, batch, cq, ck, n_kv, masked):
    tq, qw = q.shape
    kw = k.shape[-1]
    n_heads = qw // HEAD_DIM
    assert kw == LANES and n_heads % 4 == 0, "two kv heads in one lane tile, an even number of query tiles each"
    tiles_per_kv = n_heads // 4
    heads = [[2 * (kvh * tiles_per_kv + i) + half for i in range(tiles_per_kv)]
             for kvh in range(2) for half in range(2)]
    bias_g = jnp.stack([jnp.concatenate([bias[h].T for h in hs], axis=1) for hs in heads])
    sink_g = jnp.stack([jnp.concatenate([jnp.ones((1, cq), F32) * sinks[h] for h in hs], axis=1)
                        for hs in heads])

    def kv_spec(s):
        return pl.BlockSpec((ck * batch, kw), lambda c: (jnp.maximum(c - (n_kv - 1) + s, 0), 0))

    return pl.pallas_call(
        functools.partial(_attn_kernel, n_kv=n_kv, masked=masked, tiles_per_kv=tiles_per_kv, batch=batch),
        grid=(tq // (cq * batch),),
        in_specs=[pl.BlockSpec((cq * batch, LANES), lambda c, j=j: (c, j)) for j in range(qw // LANES)]
        + [kv_spec(s) for s in range(n_kv)] + [kv_spec(s) for s in range(n_kv)]
        + [pl.BlockSpec(bias_g.shape, lambda c: (0, 0, 0)), pl.BlockSpec(sink_g.shape, lambda c: (0, 0, 0))],
        out_specs=pl.BlockSpec((cq * batch, qw), lambda c: (c, 0)),
        out_shape=jax.ShapeDtypeStruct((tq, qw), BF16),
        scratch_shapes=[pltpu.VMEM((qw // LANES, cq * batch, LANES), F32)],
        compiler_params=_params("parallel"),
        name="swa_attention",
    )(*([q] * (qw // LANES)), *([k] * n_kv), *([v] * n_kv), bias_g, sink_g)


def _t5_bucket(rel):
    half = N_BUCKETS // 2
    max_exact = half // 2
    n = jnp.abs(rel)
    large = max_exact + (jnp.log(jnp.maximum(n, 1).astype(F32) / max_exact)
                         / math.log(MAX_DISTANCE / max_exact) * (half - max_exact)).astype(I32)
    large = jnp.minimum(large, half - 1)
    return jnp.where(rel > 0, half, 0) + jnp.where(n < max_exact, n, large)


def _bias_table(rel_bias, nq, nk):
    rel = (jnp.arange(nk) - WINDOW)[None, :] - jnp.arange(nq)[:, None]
    return jnp.transpose(rel_bias.astype(F32)[_t5_bucket(rel)], (2, 0, 1))


def _router_kernel(x_ref, w_ref, b_ref, route_ref, cnt_ref, carry, *, n_fine):
    @pl.when(pl.program_id(0) == 0)
    def _():
        carry[...] = jnp.zeros_like(carry)

    logits = _bdot(x_ref[...].astype(BF16), w_ref[...]) + b_ref[...]
    tm = logits.shape[0]
    lane = lax.broadcasted_iota(I32, logits.shape, 1)
    lanef = lane.astype(F32)
    per_group = n_fine // E_GROUPS
    big = float(LANES)

    def first_max(mask):
        mx = jnp.max(jnp.where(mask, logits, -jnp.inf), axis=-1, keepdims=True)
        idx = jnp.min(jnp.where(mask & (logits == mx), lanef, big), axis=-1, keepdims=True)
        return mx, idx

    cmask = lane < E_GROUPS
    mc, grp = first_max(cmask)
    p_grp = 1.0 / jnp.sum(jnp.where(cmask, jnp.exp(logits - mc), 0.0), axis=-1, keepdims=True)
    lo = E_GROUPS + per_group * grp
    fmask = (lanef >= lo) & (lanef < lo + per_group)
    m1, i1 = first_max(fmask)
    sel0 = lanef == i1
    m2, i2 = first_max(fmask & jnp.logical_not(sel0))
    e = jnp.exp(m2 - m1)
    g0 = (1.0 / (1.0 + e)) * p_grp
    g1 = (e / (1.0 + e)) * p_grp

    j1 = i1 - lo
    j2 = i2 - lo
    swap = j1 > j2
    j_lo = jnp.minimum(j1, j2)
    j_hi = jnp.maximum(j1, j2)
    n_pairs_group = per_group * (per_group - 1) // 2
    bucket = grp * n_pairs_group + j_lo * (2 * per_group - 1 - j_lo) * 0.5 + (j_hi - j_lo - 1.0)
    g_lo = jnp.where(swap, g1, g0)
    g_hi = jnp.where(swap, g0, g1)

    sel = lanef == bucket
    cmat = jnp.where(sel, 1.0, 0.0)
    row = lax.broadcasted_iota(I32, (tm, tm), 0)
    col = lax.broadcasted_iota(I32, (tm, tm), 1)
    tril = jnp.where(col < row, 1.0, 0.0).astype(BF16)
    before = carry[...] + _bdot(tril, cmat.astype(BF16))
    rank = jnp.sum(jnp.where(sel, before, 0.0), axis=-1, keepdims=True)
    carry[...] = carry[...] + jnp.sum(cmat, axis=0, keepdims=True)
    cnt_ref[...] = carry[...]
    vals = (bucket, rank, g_lo, g_hi)
    route = jnp.zeros_like(logits)
    for j, val in enumerate(vals):
        route = jnp.where(lane == j, val, route)
    route_ref[...] = route


def _router(x, w_route, b_route, n_fine):
    t, d = x.shape
    tm = min(ROW_TILE, t)
    return pl.pallas_call(
        functools.partial(_router_kernel, n_fine=n_fine),
        grid=(t // tm,),
        in_specs=[pl.BlockSpec((tm, d), lambda i: (i, 0)), pl.BlockSpec((d, LANES), lambda i: (0, 0)),
                  pl.BlockSpec((1, LANES), lambda i: (0, 0))],
        out_specs=[pl.BlockSpec((tm, LANES), lambda i: (i, 0)), pl.BlockSpec((1, LANES), lambda i: (0, 0))],
        out_shape=[jax.ShapeDtypeStruct((t, LANES), F32), jax.ShapeDtypeStruct((1, LANES), F32)],
        scratch_shapes=[pltpu.VMEM((1, LANES), F32)],
        compiler_params=_params("arbitrary"),
        name="moe_router",
    )(x, w_route, b_route)


def _row_copy(src_ref, src_row, dst_ref, dst_row, sem):
    return pltpu.make_async_copy(src_ref.at[pl.ds(src_row, 1)], dst_ref.at[pl.ds(dst_row, 1)], sem)


def _dispatch_kernel(dest_ref, x_ref, xs_in_ref, xs_ref, sem, *, tm):
    del xs_in_ref

    def issue(r, carry):
        _row_copy(x_ref, r, xs_ref, dest_ref[0, r], sem).start()
        return carry

    lax.fori_loop(0, tm, issue, 0, unroll=8)
    pltpu.make_async_copy(x_ref, xs_ref.at[pl.ds(0, tm)], sem).wait()


def _dispatch(dest, x, n_rows):
    t, d = x.shape
    tm = min(ROW_TILE, t)
    return pl.pallas_call(
        functools.partial(_dispatch_kernel, tm=tm),
        grid=(t // tm,),
        in_specs=[pl.BlockSpec((1, tm), lambda i: (0, i), memory_space=pltpu.SMEM),
                  pl.BlockSpec((tm, d), lambda i: (i, 0)),
                  pl.BlockSpec(memory_space=pl.ANY)],
        out_specs=pl.BlockSpec(memory_space=pl.ANY),
        out_shape=jax.ShapeDtypeStruct((n_rows, d), F32),
        scratch_shapes=[pltpu.SemaphoreType.DMA(())],
        input_output_aliases={2: 0},
        compiler_params=_params("arbitrary"),
        name="moe_dispatch",
    )(dest, x, jnp.zeros((n_rows, d), F32))


def _expert_kernel(ea_ref, eb_ref, nu_ref, xs_ref, wga_ref, wua_ref, wda_ref, wgb_ref, wub_ref, wdb_ref, ys_ref):
    del ea_ref, eb_ref
    i = pl.program_id(0)
    d = xs_ref.shape[1]

    @pl.when(i < nu_ref[0])
    def _():
        xb = xs_ref[...].astype(BF16)
        for half, (wg_ref, wu_ref, wd_ref) in enumerate(((wga_ref, wua_ref, wda_ref), (wgb_ref, wub_ref, wdb_ref))):
            a = _bdot(xb, wg_ref[...])
            h = (a * _sigmoid(a)) * _bdot(xb, wu_ref[...])
            ys_ref[:, half * d:(half + 1) * d] = _bdot(h.astype(BF16), wd_ref[...])

    @pl.when(i >= nu_ref[0])
    def _():
        ys_ref[...] = jnp.zeros_like(ys_ref)


def _experts(block_ea, block_eb, n_used, xs, wg, wu, wd, blk):
    n_rows, d = xs.shape
    de = wg.shape[-1]
    low = lambda i, ea, eb, nu: (ea[i], 0, 0)
    high = lambda i, ea, eb, nu: (eb[i], 0, 0)
    grid_spec = pltpu.PrefetchScalarGridSpec(
        num_scalar_prefetch=3,
        grid=(n_rows // blk,),
        in_specs=[pl.BlockSpec((blk, d), lambda i, ea, eb, nu: (i, 0)),
                  pl.BlockSpec((None, d, de), low), pl.BlockSpec((None, d, de), low),
                  pl.BlockSpec((None, de, d), low),
                  pl.BlockSpec((None, d, de), high), pl.BlockSpec((None, d, de), high),
                  pl.BlockSpec((None, de, d), high)],
        out_specs=pl.BlockSpec((blk, 2 * d), lambda i, ea, eb, nu: (i, 0)),
    )
    return pl.pallas_call(
        _expert_kernel, grid_spec=grid_spec,
        out_shape=jax.ShapeDtypeStruct((n_rows, 2 * d), F32),
        compiler_params=_params("arbitrary"),
        name="moe_experts",
    )(block_ea, block_eb, n_used, xs, wg, wu, wd, wg, wu, wd)


def _combine_kernel(dest_ref, next_dest_ref, x_ref, gate_ref, g_ref, b_ref, ys_ref, o_ref, ybuf, sems, *, tm, alpha):
    i = pl.program_id(0)
    slot = lax.rem(i, 2)

    def gather(dref, buf_slot, r):
        return _row_copy(ys_ref, dref[0, r], ybuf.at[buf_slot], r, sems.at[buf_slot])

    def wait_tile(buf_slot):
        pltpu.make_async_copy(ys_ref.at[pl.ds(0, tm)], ybuf.at[buf_slot], sems.at[buf_slot]).wait()

    @pl.when(i == 0)
    def _():
        def issue(r, carry):
            gather(dest_ref, 0, r).start()
            return carry

        lax.fori_loop(0, tm, issue, 0, unroll=8)

    wait_tile(slot)
    for r in range(tm):
        gather(next_dest_ref, 1 - slot, r).start()
    gate = gate_ref[...]
    d = x_ref.shape[1]
    rows = ybuf.at[slot]
    y = alpha * x_ref[...] + gate[:, 0:1] * rows[:, :d] + gate[:, 1:2] * rows[:, d:]
    o_ref[...] = _layer_norm(y, g_ref[...], b_ref[...])

    @pl.when(i == pl.num_programs(0) - 1)
    def _():
        wait_tile(1 - slot)


def _combine(dest, x, gates, ys, ln_g, ln_b, alpha):
    t, d = x.shape
    tm = min(ROW_TILE, t)
    last = t // tm - 1
    return pl.pallas_call(
        functools.partial(_combine_kernel, tm=tm, alpha=alpha),
        grid=(t // tm,),
        in_specs=[pl.BlockSpec((1, tm), lambda i: (0, i), memory_space=pltpu.SMEM),
                  pl.BlockSpec((1, tm), lambda i: (0, jnp.minimum(i + 1, last)), memory_space=pltpu.SMEM),
                  pl.BlockSpec((tm, d), lambda i: (i, 0)),
                  pl.BlockSpec((tm, 2), lambda i: (i, 0)),
                  pl.BlockSpec((1, d), lambda i: (0, 0)), pl.BlockSpec((1, d), lambda i: (0, 0)),
                  pl.BlockSpec(memory_space=pl.ANY)],
        out_specs=pl.BlockSpec((tm, d), lambda i: (i, 0)),
        out_shape=jax.ShapeDtypeStruct((t, d), F32),
        scratch_shapes=[pltpu.VMEM((2, tm, 2 * d), F32), pltpu.SemaphoreType.DMA((2,))],
        compiler_params=_params("arbitrary"),
        name="moe_combine_ln",
    )(dest, dest, x, gates, ln_g.reshape(1, d), ln_b.reshape(1, d), ys)


def _moe_dense_kernel(x_ref, gate_ref, wg_ref, wu_ref, wd_ref, g_ref, b_ref, o_ref, acc, *, alpha):
    e = pl.program_id(0)

    @pl.when(e == 0)
    def _():
        acc[...] = jnp.zeros_like(acc)

    x = x_ref[...]
    xb = x.astype(BF16)
    a = _bdot(xb, wg_ref[...])
    h = (a * _sigmoid(a)) * _bdot(xb, wu_ref[...])
    y = _bdot(h.astype(BF16), wd_ref[...])
    gates = gate_ref[...]
    lane = lax.broadcasted_iota(I32, gates.shape, 1)
    acc[...] += jnp.sum(jnp.where(lane == e, gates, 0.0), axis=-1, keepdims=True) * y

    @pl.when(e == pl.num_programs(0) - 1)
    def _():
        o_ref[...] = _layer_norm(alpha * x + acc[...], g_ref[...], b_ref[...])


def _moe_dense(x, dense_gates, wg, wu, wd, ln_g, ln_b, alpha):
    t, d = x.shape
    n_exp, _, de = wg.shape
    whole = lambda e: (0, 0)
    return pl.pallas_call(
        functools.partial(_moe_dense_kernel, alpha=alpha),
        grid=(n_exp,),
        in_specs=[pl.BlockSpec((t, d), whole), pl.BlockSpec((t, LANES), whole),
                  pl.BlockSpec((None, d, de), lambda e: (e, 0, 0)), pl.BlockSpec((None, d, de), lambda e: (e, 0, 0)),
                  pl.BlockSpec((None, de, d), lambda e: (e, 0, 0)),
                  pl.BlockSpec((1, d), whole), pl.BlockSpec((1, d), whole)],
        out_specs=pl.BlockSpec((t, d), whole),
        out_shape=jax.ShapeDtypeStruct((t, d), F32),
        scratch_shapes=[pltpu.VMEM((t, d), F32)],
        compiler_params=_params("arbitrary"),
        name="moe_dense_ln",
    )(x, dense_gates, wg, wu, wd, ln_g.reshape(1, d), ln_b.reshape(1, d))


def _moe_ln(x, w_c, b_c, w_f, b_f, wg, wu, wd, ln_g, ln_b, alpha):
    t, d = x.shape
    n_exp = w_f.shape[1]
    per_group = n_exp // E_GROUPS
    pad = LANES - E_GROUPS - n_exp
    w_route = jnp.concatenate([w_c.astype(F32), w_f.astype(F32), jnp.zeros((d, pad), F32)], axis=1)
    b_route = jnp.concatenate([b_c.astype(F32), b_f.astype(F32), jnp.zeros((pad,), F32)]).reshape(1, LANES)
    route, cnt = _router(x, w_route.astype(BF16), b_route, n_exp)
    bucket = route[:, 0].astype(I32)
    rank = route[:, 1].astype(I32)
    gates = route[:, 2:4]
    pairs = [(g * per_group + a, g * per_group + b) for g in range(E_GROUPS)
             for a in range(per_group) for b in range(a + 1, per_group)]
    n_buckets = len(pairs)
    assert n_buckets <= LANES
    pair_tab = jnp.array(pairs, dtype=I32)
    ids = jnp.arange(n_buckets, dtype=I32)

    def lookup(table, idx):
        return jnp.sum(jnp.where(idx[:, None] == ids[None, :], table[None, :], 0), axis=1)

    if t < MOE_DENSE_BELOW:
        lane = jnp.arange(LANES, dtype=I32)[None, :]
        e_lo = lookup(pair_tab[:, 0], bucket)[:, None]
        e_hi = lookup(pair_tab[:, 1], bucket)[:, None]
        dense_gates = jnp.where(lane == e_lo, gates[:, 0:1], 0.0) + jnp.where(lane == e_hi, gates[:, 1:2], 0.0)
        return _moe_dense(x, dense_gates, wg, wu, wd, ln_g, ln_b, alpha)
    blk = MOE_BLOCK
    counts = cnt[0, :n_buckets].astype(I32)
    padded = (counts + blk - 1) // blk * blk
    pend = jnp.cumsum(padded)
    poff = pend - padded
    dest =(lookup(poff, bucket) + rank).reshape(1, t)
    n_blocks = -(-t // blk) + n_buckets
    starts = jnp.arange(n_blocks, dtype=I32) * blk
    block_bucket = jnp.minimum(jnp.sum(pend[None, :] <= starts[:, None], axis=1), n_buckets - 1)
    block_ea = lookup(pair_tab[:, 0], block_bucket)
    block_eb = lookup(pair_tab[:, 1], block_bucket)
    n_used = (pend[-1] // blk).astype(I32).reshape(1)
    xs = _dispatch(dest, x, n_blocks * blk)
    ys = _experts(block_ea, block_eb, n_used, xs, wg, wu, wd, blk)
    return _combine(dest, x, gates, ys, ln_g, ln_b, alpha)


def _head_sum(x, ones):
    hi = x.astype(BF16)
    lo = (x - hi.astype(F32)).astype(BF16)
    parts = []
    for j in range(x.shape[1] // LANES):
        sl = slice(j * LANES, (j + 1) * LANES)
        parts.append(_bdot(hi[:, sl], ones) + _bdot(lo[:, sl], ones))
    return jnp.concatenate(parts, axis=-1)


def _rw_prep_kernel(sh_ref, halo_ref, sh0_ref, mu_ref, w0_ref, w2_ref, a0_ref, a2_ref, g2_ref,
                    kk_ref, ka_ref, rk_ref, ones_ref,
                    r_o, w_o, k_o, v_o, kk_o, b_o, bonus_o, g_o, *, batch, cw):
    sh = sh_ref[...]
    first = jnp.where(pl.program_id(0) == 0, sh0_ref[...], halo_ref[...])
    prev = jnp.concatenate([first, sh[:sh.shape[0] - batch]], axis=0)
    shm = sh + (prev - sh) * mu_ref[...]
    r, k, v = shm[:, :cw], shm[:, cw:2 * cw], shm[:, 2 * cw:3 * cw]
    xwa = shm[:, 3 * cw:3 * cw + LANES]
    xg = shm[:, 3 * cw + LANES:]
    ones = ones_ref[...]
    z = w0_ref[...] + _bdot(jnp.tanh(xwa).astype(BF16), w2_ref[...])
    softplus = jnp.maximum(-z, 0.0) + jnp.log1p(jnp.exp(-jnp.abs(z)))
    w_o[...] = jnp.exp(-jnp.exp(-softplus - 0.5))
    a = _sigmoid(a0_ref[...] + _bdot(xwa.astype(BF16), a2_ref[...]))
    g_o[...] = _bdot(_sigmoid(xg).astype(BF16), g2_ref[...])
    kk = k * kk_ref[...]
    kk = kk / jnp.maximum(jnp.sqrt(_head_sum(kk * kk, ones)), 1e-12)
    k2 = k * (1.0 + (a - 1.0) * ka_ref[...])
    r_o[...] = r
    k_o[...] = k2
    v_o[...] = v
    kk_o[...] = kk
    b_o[...] = kk * a
    bonus_o[...] = _head_sum(r * k2 * rk_ref[...], ones) * v


def _rw_prep(sh, shift0, batch, prm):
    t, sw = sh.shape
    cw = prm["w0"].shape[-1]
    tm = min(256, t)
    hb = tm // batch
    row = lambda i: (i, 0)
    fix = lambda i: (0, 0)
    vec = lambda n: pl.BlockSpec((1, n), fix)
    return pl.pallas_call(
        functools.partial(_rw_prep_kernel, batch=batch, cw=cw),
        grid=(t // tm,),
        in_specs=[pl.BlockSpec((tm, sw), row),
                  pl.BlockSpec((batch, sw), lambda i: (jnp.maximum(i * hb - 1, 0), 0)),
                  pl.BlockSpec((batch, sw), fix), vec(sw), vec(cw),
                  pl.BlockSpec((LANES, cw), fix), vec(cw), pl.BlockSpec((LANES, cw), fix),
                  pl.BlockSpec((LANES, cw), fix), vec(cw), vec(cw), vec(cw),
                  pl.BlockSpec((LANES, LANES), fix)],
        out_specs=[pl.BlockSpec((tm, cw), row)] * 8,
        out_shape=[jax.ShapeDtypeStruct((t, cw), F32)] * 8,
        compiler_params=_params("parallel"),
        name="rwkv_prep",
    )(sh, sh, shift0, prm["mu"], prm["w0"], prm["w2"], prm["a0"], prm["a2"], prm["g2"],
      prm["k_k"], prm["k_a"], prm["r_k"], prm["ones"])


RW_TILES_A = 4
RW_TILES_B = 2
RW_TICK = 8


def _rwkv_kernel(r_ref, w_ref, k_ref, v_ref, kk_ref, b_ref, sa0_ref, sb0_ref, y_ref, san_ref, sbn_ref,
                 st_a, st_b, xa0, xb0, xa1, xb1, *, batch, steps):
    hd = HEAD_DIM
    hv = hd // 2
    ti = pl.program_id(0)

    @pl.when(ti == 0)
    def _():
        st_a[...] = sa0_ref[...]
        st_b[...] = sb0_ref[...]

    operands = (r_ref, w_ref, k_ref, v_ref, kk_ref, b_ref)
    def lane_of(rows):
        return lax.broadcasted_iota(I32, (rows, LANES), 1)

    lane_lo = lane_of(hd) < hd
    lane_lo_out = lane_of(batch) < hd
    even_quarter = (lane_of(hd) // hv) % 2 == 0
    quarter_v = lane_of(hv) // hv
    quarter_out = lane_of(batch) // hv

    def pick_quarter(q, blocks):
        return jnp.where(q == 0, blocks[0], jnp.where(q == 1, blocks[1], jnp.where(q == 2, blocks[2], blocks[3])))

    def padded(parts, rows):
        have = len(parts) * batch
        return parts + ([jnp.zeros((rows - have, LANES), F32)] if have < rows else [])

    def run(state, x, nv, tick):
        acc = jnp.zeros((nv, LANES), F32)
        for kidx in range(hd):
            acc = acc + state[kidx] * x[4, kidx:kidx + 1, :]
            if kidx % RW_TICK == RW_TICK - 1:
                tick()
        sa = acc
        vv = x[3, 0:nv, :]
        acc = jnp.zeros((nv, LANES), F32)
        for kidx in range(hd):
            row = slice(kidx, kidx + 1)
            sn = state[kidx] * x[1, row, :] - sa * x[5, row, :] + vv * x[2, row, :]
            state[kidx] = sn
            acc = acc + sn * x[0, row, :]
            if kidx % RW_TICK == RW_TICK - 1:
                tick()
        return acc

    def transpose_operand(o, r0, xa, xb):
        slab = operands[o][pl.ds(r0, batch), :]
        tiles = [slab[:, j * LANES:(j + 1) * LANES] for j in range(RW_TILES_A + RW_TILES_B)]
        ta = jnp.concatenate(padded(tiles[:RW_TILES_A], hd) * 2, axis=0).T
        xa[o] = jnp.where(lane_lo, ta[:hd], ta[hd:])
        tb = jnp.concatenate(padded(tiles[RW_TILES_A:], hv) * 4, axis=0).T
        if o == 3:
            xb[o, pl.ds(0, hv), :] = pick_quarter(quarter_v, (tb[:hv], tb[hd:hd + hv], tb[hv:hd], tb[hd + hv:]))
        else:
            xb[o] = jnp.where(even_quarter, tb[:hd], tb[hd:])

    def time_step(t, cur, nxt):
        r0 = pl.multiple_of(t * batch, batch)
        rn = pl.multiple_of(jnp.minimum(t + 1, steps - 1) * batch, batch)
        pending = [functools.partial(transpose_operand, o, rn, *nxt) for o in range(len(operands))]

        def tick():
            if pending:
                pending.pop(0)()

        ya = run(st_a, cur[0], hd, tick)
        yb = run(st_b, cur[1], hv, tick)
        while pending:
            tick()
        mta = jnp.concatenate([ya, ya], axis=0).T
        for j in range(RW_TILES_A):
            lo = mta[j * batch:(j + 1) * batch]
            hi = mta[hd + j * batch:hd + (j + 1) * batch]
            y_ref[pl.ds(r0, batch), j * LANES:(j + 1) * LANES] = jnp.where(lane_lo_out, lo, hi)
        mtb = jnp.concatenate([yb] * 4, axis=0).T
        for j in range(RW_TILES_B):
            blocks = [mtb[base + j * batch:base + (j + 1) * batch] for base in (0, hd, hv, hd + hv)]
            tile = RW_TILES_A + j
            y_ref[pl.ds(r0, batch), tile * LANES:(tile + 1) * LANES] = pick_quarter(quarter_out, blocks)

    even_bufs, odd_bufs = (xa0, xb0), (xa1, xb1)
    for o in range(len(operands)):
        transpose_operand(o, 0, *even_bufs)

    def step_pair(p, carry):
        time_step(2 * p, even_bufs, odd_bufs)
        time_step(2 * p + 1, odd_bufs, even_bufs)
        return carry

    lax.fori_loop(0, steps // 2, step_pair, 0)

    @pl.when(ti == pl.num_programs(0) - 1)
    def _():
        san_ref[...] = st_a[...]
        sbn_ref[...] = st_b[...]


def _rwkv(ops, s0, batch):
    t, cw = ops[0].shape
    hd, hv = HEAD_DIM, HEAD_DIM // 2
    assert cw == (RW_TILES_A + RW_TILES_B) * LANES and RW_TILES_A * batch <= hd
    ca, cb = RW_TILES_A * batch, RW_TILES_B * batch
    if s0 is None:
        sa0 = jnp.zeros((hd, hd, LANES), F32)
        sb0 = jnp.zeros((hd, hv, LANES), F32)
    else:
        s0 = s0.astype(F32)
        sa0 = s0[:, :2 * RW_TILES_A].reshape(batch, RW_TILES_A, 2, hd, hd).transpose(4, 3, 2, 1, 0)
        sa0 = jnp.pad(sa0.reshape(hd, hd, 2, ca), ((0, 0),) * 3 + ((0, hd - ca),)).reshape(hd, hd, LANES)
        sb0 = s0[:, 2 * RW_TILES_A:].reshape(batch, RW_TILES_B, 2, 2, hv, hd).transpose(5, 4, 3, 2, 1, 0)
        sb0 = jnp.pad(sb0.reshape(hd, hv, 2, 2, cb), ((0, 0),) * 4 + ((0, hv - cb),)).reshape(hd, hv, LANES)
    steps = min(32, t // batch)
    assert steps % 2 == 0, "the time loop is unrolled in even/odd operand-buffer pairs"
    rows = steps * batch
    blk = pl.BlockSpec((rows, cw), lambda i: (i, 0))
    st_a = pl.BlockSpec((hd, hd, LANES), lambda i: (0, 0, 0))
    st_b = pl.BlockSpec((hd, hv, LANES), lambda i: (0, 0, 0))
    y, san, sbn = pl.pallas_call(
        functools.partial(_rwkv_kernel, batch=batch, steps=steps),
        grid=(t // rows,),
        in_specs=[blk] * 6 + [st_a, st_b],
        out_specs=[blk, st_a, st_b],
        out_shape=[jax.ShapeDtypeStruct((t, cw), F32), jax.ShapeDtypeStruct((hd, hd, LANES), F32),
                   jax.ShapeDtypeStruct((hd, hv, LANES), F32)],
        scratch_shapes=[pltpu.VMEM((hd, hd, LANES), F32), pltpu.VMEM((hd, hv, LANES), F32),
                        ] + [pltpu.VMEM((6, hd, LANES), F32)] * 4,
        compiler_params=_params("arbitrary"),
        name="rwkv_recurrence",
    )(*ops, sa0, sb0)
    sa = san.reshape(hd, hd, 2, hd)[..., :ca].reshape(hd, hd, 2, RW_TILES_A, batch).transpose(4, 3, 2, 1, 0)
    sb = sbn.reshape(hd, hv, 2, 2, hv)[..., :cb].reshape(hd, hv, 2, 2, RW_TILES_B, batch).transpose(5, 4, 3, 2, 1, 0)
    s_last = jnp.concatenate([sa.reshape(batch, 2 * RW_TILES_A, hd, hd), sb.reshape(batch, 2 * RW_TILES_B, hd, hd)],
                             axis=1)
    return y, s_last


def _rw_post_kernel(y_ref, bonus_ref, g_ref, lg_ref, lb_ref, ones_ref, o_ref):
    y = y_ref[...]
    ones = ones_ref[...]
    mu = _head_sum(y, ones) * (1.0 / HEAD_DIM)
    d = y - mu
    var = _head_sum(d * d, ones) * (1.0 / HEAD_DIM)
    yn = d * lax.rsqrt(var + GN_EPS) * lg_ref[...] + lb_ref[...]
    o_ref[...] = ((yn + bonus_ref[...]) * g_ref[...]).astype(o_ref.dtype)


def _rw_post(y, bonus, g, ln_g, ln_b, ones):
    t, cw = y.shape
    tm = min(ROW_TILE, t)
    row = pl.BlockSpec((tm, cw), lambda i: (i, 0))
    vec = pl.BlockSpec((1, cw), lambda i: (0, 0))
    return pl.pallas_call(
        _rw_post_kernel, grid=(t // tm,),
        in_specs=[row, row, row, vec, vec, pl.BlockSpec((LANES, LANES), lambda i: (0, 0))],
        out_specs=row, out_shape=jax.ShapeDtypeStruct((t, cw), BF16),
        compiler_params=_params("parallel"),
        name="rwkv_post",
    )(y, bonus, g, ln_g, ln_b, ones)


def _pool_kernel(main_ref, halo_ref, w_ref, scale_ref, o_ref, *, batch, pos0):
    tm = main_ref.shape[0]
    full = jnp.concatenate([main_ref[...], halo_ref[...]], axis=0)
    b = batch
    n = full.shape[0]
    a2 = full[b:] + full[:n - b]
    a4 = a2[2 * b:] + a2[:n - 3 * b]
    a8 = a4[4 * b:] + a4[:n - 7 * b]
    a16 = a8[8 * b:] + a8[:n - 15 * b]
    cur = full[16 * b:16 * b + tm]
    sums = (a2[15 * b:15 * b + tm], a4[13 * b:13 * b + tm], a8[9 * b:9 * b + tm], a16[b:b + tm])
    lane = lax.broadcasted_iota(I32, cur.shape, 1)
    grp = lane // (cur.shape[1] // len(POOL_WINDOWS))
    win_sum = sums[-1]
    width = jnp.full(cur.shape, float(POOL_WINDOWS[-1]), F32)
    for j in range(len(POOL_WINDOWS) - 2, -1, -1):
        win_sum = jnp.where(grp == j, sums[j], win_sum)
        width = jnp.where(grp == j, float(POOL_WINDOWS[j]), width)
    row = lax.broadcasted_iota(I32, cur.shape, 0)
    t_idx = pl.program_id(0) * (tm // b) + row // b
    cnt = jnp.minimum(width, (pos0 + t_idx + 1).astype(F32))
    pooled = win_sum / cnt - cur
    o_ref[...] = (_bdot(pooled.astype(BF16), w_ref[...]) * scale_ref[...]).astype(o_ref.dtype)


def _pool(fp, t, batch, w_bd, scale, pos0):
    dw = fp.shape[1]
    tm = min(ROW_TILE, t)
    halo = 16 * batch
    return pl.pallas_call(
        functools.partial(_pool_kernel, batch=batch, pos0=pos0),
        grid=(t // tm,),
        in_specs=[pl.BlockSpec((tm, dw), lambda i: (i, 0)),
                  pl.BlockSpec((halo, dw), lambda i: ((i + 1) * (tm // halo), 0)),
                  pl.BlockSpec((dw, dw), lambda i: (0, 0)), pl.BlockSpec((1, dw), lambda i: (0, 0))],
        out_specs=pl.BlockSpec((tm, dw), lambda i: (i, 0)),
        out_shape=jax.ShapeDtypeStruct((t, dw), BF16),
        compiler_params=_params("parallel"),
        name="causal_pool",
    )(fp, fp, w_bd, scale)


def _to_batch_major(a, seq, batch):
    return a.reshape(seq, batch, a.shape[-1]).transpose(1, 0, 2)


def _to_time_major(a):
    bsz, seq, width = a.shape
    return a.transpose(1, 0, 2).reshape(seq * bsz, width)


def _even_mixer(x, batch, seq, h0, cache_k, cache_v, w, j, bias, alpha, ln_g, ln_b):
    a_width = w["s5_w_glu"].shape[-1]
    w_in = w["ev_w_in"][j]
    qw = a_width
    kvw = (w_in.shape[1] - 2 * a_width) // 2
    u, q, k, v = _matmul([x], [w_in.astype(BF16)], splits=(a_width, qw, kvw, kvw))
    prm = _s5_params(w["s5_a_re"][j], w["s5_a_im"][j], w["s5_log_dt"][j], w["s5_b_re"][j], w["s5_b_im"][j],
                     w["s5_c_re"][j], w["s5_c_im"][j], w["s5_d"][j])
    nt = a_width // LANES
    ns = prm["bre"].shape[-1]
    if h0 is None:
        h0re = jnp.zeros((nt, batch, ns), F32)
        h0im = h0re
    else:
        h0re = h0[..., 0].astype(F32).reshape(batch, nt, ns).transpose(1, 0, 2)
        h0im = h0[..., 1].astype(F32).reshape(batch, nt, ns).transpose(1, 0, 2)
    g, hn_re, hn_im = _s5(u, batch, h0re, h0im, prm)
    groups, n_state = w["s5_a_re"][j].shape
    h_last = jnp.stack([hn_re.transpose(1, 0, 2).reshape(batch, groups, n_state),
                        hn_im.transpose(1, 0, 2).reshape(batch, groups, n_state)], axis=-1)
    a_out = _matmul([g], [w["s5_w_glu"][j].astype(BF16)], epilogue="glu", out_dtype=BF16)

    sinks = w["swa_sinks"][j].astype(F32)
    if cache_k is None:
        att = _attention(q, k, v, bias, sinks, batch, CHUNK, CHUNK, WINDOW // CHUNK + 1, True)
        k_all, v_all = k, v
    else:
        k_all = jnp.concatenate([_to_time_major(cache_k.astype(F32).reshape(batch, WINDOW, kvw)), k], axis=0)
        v_all = jnp.concatenate([_to_time_major(cache_v.astype(F32).reshape(batch, WINDOW, kvw)), v], axis=0)
        att = _attention(q, k_all, v_all, bias, sinks, batch, seq, WINDOW + seq, 1, False)
    keep = k_all.shape[0] - WINDOW * batch
    new_k = _to_batch_major(k_all[keep:], WINDOW, batch)
    new_v = _to_batch_major(v_all[keep:], WINDOW, batch)
    w_out = w["ev_w_out"][j].astype(BF16)
    xn = _matmul([a_out, att], [w_out[:a_width], w_out[a_width:]], epilogue="ln",
                 res=x, g=ln_g, b=ln_b, alpha=alpha)
    kv_shape = (batch, WINDOW, kvw // HEAD_DIM, HEAD_DIM)
    return xn, h_last, new_k.reshape(kv_shape), new_v.reshape(kv_shape)


def _odd_mixer(x, batch, seq, shift0, s0, pool0, pos0, w, j, alpha, ln_g, ln_b):
    t = x.shape[0]
    w_in = w["od_w_in"][j]
    cw = w["rw_w0"].shape[-1]
    sw = w["rw_mu"].shape[-1]
    dw = w_in.shape[1] - sw
    n_heads = cw // HEAD_DIM
    sh, p = _matmul([x], [w_in.astype(BF16)], splits=(sw, dw))
    zeros_half = jnp.zeros((LANES - w["rw_w2"].shape[1], cw), F32)
    head_lane = jnp.arange(LANES) // HEAD_DIM
    ones = (head_lane[:, None] == head_lane[None, :]).astype(BF16)
    prm = dict(
        mu=w["rw_mu"][j].astype(F32).reshape(1, sw), w0=w["rw_w0"][j].astype(F32).reshape(1, cw),
        w2=jnp.concatenate([w["rw_w2"][j].astype(F32), zeros_half], axis=0).astype(BF16),
        a0=w["rw_a0"][j].astype(F32).reshape(1, cw),
        a2=jnp.concatenate([zeros_half, w["rw_a2"][j].astype(F32)], axis=0).astype(BF16),
        g2=w["rw_g2"][j].astype(BF16),
        k_k=w["rw_k_k"][j].astype(F32).reshape(1, cw), k_a=w["rw_k_a"][j].astype(F32).reshape(1, cw),
        r_k=w["rw_r_k"][j].astype(F32).reshape(1, cw), ones=ones)
    if shift0 is None:
        shift0 = jnp.zeros((batch, sw), F32)
    r, dec, k2, v, kk, b, bonus, g = _rw_prep(sh, shift0.astype(F32), batch, prm)

    y, s_last = _rwkv((r, dec, k2, v, kk, b), s0, batch)
    c_out = _rw_post(y, bonus, g, w["rw_ln_g"][j].astype(F32).reshape(1, cw),
                     w["rw_ln_b"][j].astype(F32).reshape(1, cw), ones)

    if pool0 is None:
        hist = jnp.zeros((POOL_HIST * batch, dw), F32)
    else:
        hist = _to_time_major(pool0.astype(F32))
    fp = jnp.concatenate([jnp.zeros((batch, dw), F32), hist, p], axis=0)
    n_grp = len(POOL_WINDOWS)
    pg = dw // n_grp
    eye = jnp.eye(n_grp, dtype=F32)
    w_bd = (w["pool_w"][j].astype(F32)[:, :, None, :] * eye[:, None, :, None]).reshape(dw, dw).astype(BF16)
    d_out = _pool(fp, t, batch, w_bd, w["pool_scale"][j].astype(F32).reshape(1, dw), pos0)
    pool_new = _to_batch_major(fp[fp.shape[0] - POOL_HIST * batch:], POOL_HIST, batch)

    w_out = w["od_w_out"][j].astype(BF16)
    xn = _matmul([c_out, d_out], [w_out[:cw], w_out[cw:]], epilogue="ln", res=x, g=ln_g, b=ln_b, alpha=alpha)
    sh_last = sh[t - batch:]
    return xn, sh_last, s_last, pool_new


def _trunk(x3, s5_0, k_0, v_0, rw_0, shift_0, pool_0, pos0, w):
    batch, seq, _ = x3.shape
    depth = w["ln1_g"].shape[0]
    alpha = (2 * depth) ** 0.25
    x = _to_time_major(x3.astype(F32))
    prompt = s5_0 is None
    if prompt:
        bias = _bias_table(w["rel_bias"], CHUNK, (WINDOW // CHUNK + 1) * CHUNK)
    else:
        bias = _bias_table(w["rel_bias"], seq, WINDOW + seq)
    s5_new, k_new, v_new, rw_new, sh_new, pool_new = [], [], [], [], [], []
    for i in range(depth):
        j = i // 2
        if i % 2 == 0:
            x, h1, nk, nv = _even_mixer(
                x, batch, seq, None if prompt else s5_0[j], None if prompt else k_0[j],
                None if prompt else v_0[j], w, j, bias, alpha, w["ln1_g"][i], w["ln1_b"][i])
            s5_new.append(h1)
            k_new.append(nk)
            v_new.append(nv)
        else:
            x, sh1, s1, pl1 = _odd_mixer(
                x, batch, seq, None if prompt else shift_0[j], None if prompt else rw_0[j],
                None if prompt else pool_0[j], pos0, w, j, alpha, w["ln1_g"][i], w["ln1_b"][i])
            sh_new.append(sh1)
            rw_new.append(s1)
            pool_new.append(pl1)
        x = _moe_ln(x, w["moe_w_coarse"][i], w["moe_b_coarse"][i], w["moe_w_fine"][i], w["moe_b_fine"][i],
                    w["moe_w_gate"][i].astype(BF16), w["moe_w_up"][i].astype(BF16),
                    w["moe_w_down"][i].astype(BF16), w["ln2_g"][i], w["ln2_b"][i], alpha)
    y = _to_batch_major(x, seq, batch)
    return (y, jnp.stack(s5_new), jnp.stack(k_new), jnp.stack(v_new),
            jnp.stack(rw_new), jnp.stack(sh_new), jnp.stack(pool_new))


def kernel(x_prompt, x_sample, state_s5, cache_swa_k, cache_swa_v, state_rwkv, state_shift, state_pool, rel_bias, ev_w_in, ev_w_out, s5_a_re, s5_a_im, s5_log_dt, s5_b_re, s5_b_im, s5_c_re, s5_c_im, s5_d, s5_w_glu, swa_sinks, od_w_in, od_w_out, rw_mu, rw_w0, rw_w2, rw_a0, rw_a2, rw_g2, rw_k_k, rw_k_a, rw_r_k, rw_ln_g, rw_ln_b, pool_w, pool_scale, ln1_g, ln1_b, ln2_g, ln2_b, moe_w_coarse, moe_b_coarse, moe_w_fine, moe_b_fine, moe_w_gate, moe_w_up, moe_w_down):
    w = dict(rel_bias=rel_bias, ev_w_in=ev_w_in, ev_w_out=ev_w_out, s5_a_re=s5_a_re, s5_a_im=s5_a_im,
             s5_log_dt=s5_log_dt, s5_b_re=s5_b_re, s5_b_im=s5_b_im, s5_c_re=s5_c_re, s5_c_im=s5_c_im,
             s5_d=s5_d, s5_w_glu=s5_w_glu, swa_sinks=swa_sinks, od_w_in=od_w_in, od_w_out=od_w_out,
             rw_mu=rw_mu, rw_w0=rw_w0, rw_w2=rw_w2, rw_a0=rw_a0, rw_a2=rw_a2, rw_g2=rw_g2, rw_k_k=rw_k_k,
             rw_k_a=rw_k_a, rw_r_k=rw_r_k, rw_ln_g=rw_ln_g, rw_ln_b=rw_ln_b, pool_w=pool_w,
             pool_scale=pool_scale, ln1_g=ln1_g, ln1_b=ln1_b, ln2_g=ln2_g, ln2_b=ln2_b,
             moe_w_coarse=moe_w_coarse, moe_b_coarse=moe_b_coarse, moe_w_fine=moe_w_fine,
             moe_b_fine=moe_b_fine, moe_w_gate=moe_w_gate, moe_w_up=moe_w_up, moe_w_down=moe_w_down)
    y_s, s5_s, k_s, v_s, rw_s, sh_s, pool_s = _trunk(
        x_sample, state_s5, cache_swa_k, cache_swa_v, state_rwkv, state_shift, state_pool, PAST_LEN, w)
    y_p, s5_p, k_p, v_p, rw_p, sh_p, pool_p = _trunk(x_prompt, None, None, None, None, None, None, 0, w)
    return (y_p, y_s, s5_p, s5_s, k_p, v_p, k_s, v_s, rw_p, rw_s, sh_p, sh_s, pool_p, pool_s)
```

```python
import functools
import math

import jax
import jax.numpy as jnp
from jax import lax
from jax.experimental import pallas as pl
from jax.experimental.pallas import tpu as pltpu

F32 = jnp.float32
BF16 = jnp.bfloat16
I32 = jnp.int32

CHUNK = 64
WINDOW = 128
HEAD_DIM = 64
PAST_LEN = 1024
N_BUCKETS = 32
MAX_DISTANCE = 128
POOL_WINDOWS = (2, 4, 8, 16)
POOL_HIST = max(POOL_WINDOWS) - 1
E_GROUPS = 4
LN_EPS = 1e-5
GN_EPS = 64e-5
A_GROUP = 16

LANES = 128
SUBLANES = 8
VMEM_LIMIT_BYTES = 56 * 1024 * 1024

ROW_TILE = 512
MOE_BLOCK = 256
MOE_DENSE_BELOW = 8 * MOE_BLOCK


def _params(*sem):
    return pltpu.CompilerParams(dimension_semantics=sem, vmem_limit_bytes=VMEM_LIMIT_BYTES)


def _sigmoid(x):
    return 1.0 / (1.0 + jnp.exp(-x))


def _layer_norm(y, g, b):
    mu = jnp.mean(y, axis=-1, keepdims=True)
    d = y - mu
    var = jnp.mean(d * d, axis=-1, keepdims=True)
    return d * lax.rsqrt(var + LN_EPS) * g + b


def _bdot(a, b):
    return jnp.dot(a, b, preferred_element_type=F32)


def _mm_kernel(*refs, n_in, splits, epilogue, alpha):
    xs = refs[:n_in]
    ws = refs[n_in:2 * n_in]
    pos = 2 * n_in
    acc = None
    for x_ref, w_ref in zip(xs, ws):
        d = _bdot(x_ref[...].astype(BF16), w_ref[...])
        acc = d if acc is None else acc + d
    if epilogue == "ln":
        res_ref, g_ref, b_ref = refs[pos:pos + 3]
        pos += 3
        acc = _layer_norm(alpha * res_ref[...] + acc, g_ref[...], b_ref[...])
    elif epilogue == "glu":
        acc = xs[0][...] * _sigmoid(acc)
    off = 0
    for o_ref, n in zip(refs[pos:], splits):
        o_ref[...] = acc[:, off:off + n].astype(o_ref.dtype)
        off += n


def _matmul(xs, ws, splits=None, epilogue=None, res=None, g=None, b=None, alpha=None, out_dtype=F32):
    t = xs[0].shape[0]
    n = ws[0].shape[1]
    splits = tuple(splits) if splits else (n,)
    tm = min(ROW_TILE, t)
    in_specs = [pl.BlockSpec((tm, x.shape[1]), lambda i: (i, 0)) for x in xs]
    in_specs += [pl.BlockSpec(w.shape, lambda i: (0, 0)) for w in ws]
    args = list(xs) + list(ws)
    if epilogue == "ln":
        in_specs += [pl.BlockSpec((tm, n), lambda i: (i, 0)),
                     pl.BlockSpec((1, n), lambda i: (0, 0)), pl.BlockSpec((1, n), lambda i: (0, 0))]
        args += [res, g.reshape(1, n), b.reshape(1, n)]
    outs = pl.pallas_call(
        functools.partial(_mm_kernel, n_in=len(xs), splits=splits, epilogue=epilogue, alpha=alpha),
        grid=(t // tm,),
        in_specs=in_specs,
        out_specs=[pl.BlockSpec((tm, s), lambda i: (i, 0)) for s in splits],
        out_shape=[jax.ShapeDtypeStruct((t, s), out_dtype) for s in splits],
        compiler_params=_params("parallel"),
        name="matmul_" + (epilogue or "plain"),
    )(*args)
    return outs if len(splits) > 1 else outs[0]


def _s5_kernel(u_ref, bre_ref, bim_ref, cre_ref, cim_ref, are_ref, aim_ref, d_ref, h0re_ref, h0im_ref,
               g_ref, hnre_ref, hnim_ref, hre, him, car_re, car_im, *, batch, steps):
    ti = pl.program_id(1)

    @pl.when(ti == 0)
    def _():
        car_re[...] = h0re_ref[...]
        car_im[...] = h0im_ref[...]

    u = u_ref[...]
    ub = u.astype(BF16)
    hre[...] = _bdot(ub, bre_ref[...])
    him[...] = _bdot(ub, bim_ref[...])
    a_re = jnp.broadcast_to(are_ref[...], car_re.shape)
    a_im = jnp.broadcast_to(aim_ref[...], car_re.shape)

    def step(t, carry):
        h_re, h_im = carry
        r0 = pl.multiple_of(t * batch, batch)
        n_re = a_re * h_re - a_im * h_im + hre[pl.ds(r0, batch), :]
        n_im = a_re * h_im + a_im * h_re + him[pl.ds(r0, batch), :]
        hre[pl.ds(r0, batch), :] = n_re
        him[pl.ds(r0, batch), :] = n_im
        return n_re, n_im

    h_re, h_im = lax.fori_loop(0, steps, step, (car_re[...], car_im[...]), unroll=4)
    car_re[...] = h_re
    car_im[...] = h_im
    hnre_ref[...] = h_re
    hnim_ref[...] = h_im
    y = (_bdot(hre[...].astype(BF16), cre_ref[...]) - _bdot(him[...].astype(BF16), cim_ref[...])
         + d_ref[...] * u)
    cdf = 0.5 * (1.0 + jnp.tanh(math.sqrt(2.0 / math.pi) * (y + 0.044715 * (y * y * y))))
    g_ref[...] = y * cdf


def _s5(u, batch, h0re, h0im, prm):
    t, width = u.shape
    nt = width // LANES
    ns = prm["bre"].shape[-1]
    rows = min(1024, t)
    steps = rows // batch
    tile = lambda j, i: (j, 0, 0)
    return pl.pallas_call(
        functools.partial(_s5_kernel, batch=batch, steps=steps),
        grid=(nt, t // rows),
        in_specs=[pl.BlockSpec((rows, LANES), lambda j, i: (i, j)),
                  pl.BlockSpec((None, LANES, ns), tile), pl.BlockSpec((None, LANES, ns), tile),
                  pl.BlockSpec((None, ns, LANES), tile), pl.BlockSpec((None, ns, LANES), tile),
                  pl.BlockSpec((None, 1, ns), tile), pl.BlockSpec((None, 1, ns), tile),
                  pl.BlockSpec((None, 1, LANES), tile),
                  pl.BlockSpec((None, batch, ns), tile), pl.BlockSpec((None, batch, ns), tile)],
        out_specs=[pl.BlockSpec((rows, LANES), lambda j, i: (i, j)),
                   pl.BlockSpec((None, batch, ns), tile), pl.BlockSpec((None, batch, ns), tile)],
        out_shape=[jax.ShapeDtypeStruct((t, width), F32),
                   jax.ShapeDtypeStruct((nt, batch, ns), F32), jax.ShapeDtypeStruct((nt, batch, ns), F32)],
        scratch_shapes=[pltpu.VMEM((rows, ns), F32), pltpu.VMEM((rows, ns), F32),
                        pltpu.VMEM((batch, ns), F32), pltpu.VMEM((batch, ns), F32)],
        compiler_params=_params("parallel", "arbitrary"),
        name="s5_scan",
    )(u, prm["bre"], prm["bim"], prm["cre"], prm["cim"], prm["are"], prm["aim"], prm["d"], h0re, h0im)


def _s5_params(a_re, a_im, log_dt, b_re, b_im, c_re, c_im, d_skip):
    groups, n_state = a_re.shape
    gpt = LANES // A_GROUP
    nt = groups // gpt
    lam = lax.complex(a_re.astype(F32), a_im.astype(F32))
    dt = jnp.exp(log_dt.astype(F32))[:, None]
    a_bar = jnp.exp(lam * dt)
    b_bar = ((a_bar - 1.0) / lam)[..., None] * lax.complex(b_re.astype(F32), b_im.astype(F32))
    eye = jnp.eye(gpt, dtype=F32)

    def in_proj(m):
        m = m.reshape(nt, gpt, n_state, A_GROUP).transpose(0, 1, 3, 2)
        return (m[:, :, :, None, :] * eye[None, :, None, :, None]).reshape(nt, gpt * A_GROUP, gpt * n_state)

    def out_proj(m):
        m = m.astype(F32).reshape(nt, gpt, A_GROUP, n_state).transpose(0, 1, 3, 2)
        return (m[:, :, :, None, :] * eye[None, :, None, :, None]).reshape(nt, gpt * n_state, gpt * A_GROUP)

    return dict(bre=in_proj(b_bar.real).astype(BF16), bim=in_proj(b_bar.imag).astype(BF16),
                cre=out_proj(c_re).astype(BF16), cim=out_proj(c_im).astype(BF16),
                are=a_bar.real.reshape(nt, 1, gpt * n_state), aim=a_bar.imag.reshape(nt, 1, gpt * n_state),
                d=d_skip.astype(F32).reshape(nt, 1, LANES))


def _attn_kernel(*refs, n_kv, masked, tiles_per_kv, batch):
    n_q = 2 * tiles_per_kv
    q_refs = refs[:n_q]
    k_refs = refs[n_q:n_q + n_kv]
    v_refs = refs[n_q + n_kv:n_q + 2 * n_kv]
    bias_ref, sink_ref, o_ref, o_tiles = refs[n_q + 2 * n_kv:]
    c = pl.program_id(0)
    cq = q_refs[0].shape[0] // batch
    ck = k_refs[0].shape[0] // batch
    nk = n_kv * ck
    low_half = lax.broadcasted_iota(I32, (nk, LANES), 1) < HEAD_DIM
    if masked:
        key = lax.broadcasted_iota(I32, (nk, tiles_per_kv * cq), 0)
        valid = (c - (n_kv - 1)) * CHUNK + key >= 0

    def one_sequence(b, carry):
        kt = jnp.concatenate([r[pl.ds(b, ck, stride=batch), :] for r in k_refs], axis=0)
        vt = jnp.concatenate([r[pl.ds(b, ck, stride=batch), :] for r in v_refs], axis=0)
        k_sw = pltpu.roll(kt, HEAD_DIM, 1)
        v_sw = pltpu.roll(vt, HEAD_DIM, 1)

        def placed(t, t_sw, kvh, half):
            src = t if kvh == half else t_sw
            keep = low_half if half == 0 else jnp.logical_not(low_half)
            return jnp.where(keep, src, 0.0).astype(BF16)

        scores = []
        for kvh in range(2):
            tiles = [q_refs[kvh * tiles_per_kv + i][pl.ds(b, cq, stride=batch), :] for i in range(tiles_per_kv)]
            qs = jnp.concatenate(tiles, axis=0).astype(BF16)
            for half in range(2):
                s = lax.dot_general(placed(kt, k_sw, kvh, half), qs, (((1,), (1,)), ((), ())),
                                    preferred_element_type=F32)
                s = s * (HEAD_DIM ** -0.5) + bias_ref[2 * kvh + half]
                if masked:
                    s = jnp.where(valid, s, -jnp.inf)
                scores.append(s)
        probs = []
        for grp, s in enumerate(scores):
            sink = sink_ref[grp]
            m = jnp.maximum(jnp.max(s, axis=0, keepdims=True), sink)
            p = jnp.exp(s - m)
            denom = jnp.sum(p, axis=0, keepdims=True) + jnp.exp(sink - m)
            probs.append((p / denom).astype(BF16))
        tn = (((0,), (0,)), ((), ()))
        for kvh in range(2):
            o_t = (lax.dot_general(placed(vt, v_sw, kvh, 0), probs[2 * kvh], tn, preferred_element_type=F32)
                   + lax.dot_general(placed(vt, v_sw, kvh, 1), probs[2 * kvh + 1], tn, preferred_element_type=F32))
            o = o_t.T
            for i in range(tiles_per_kv):
                tile = kvh * tiles_per_kv + i
                o_tiles[tile, pl.ds(b, cq, stride=batch), :] = o[i * cq:(i + 1) * cq]
        return carry

    lax.fori_loop(0, batch, one_sequence, 0, unroll=2)
    for tile in range(n_q):
        o_ref[:, tile * LANES:(tile + 1) * LANES] = o_tiles[tile].astype(o_ref.dtype)


def _attention(q, k, v, bias, sinks, batch, cq, ck, n_kv, masked):
    tq, qw = q.shape
    kw = k.shape[-1]
    n_heads = qw // HEAD_DIM
    assert kw == LANES and n_heads % 4 == 0, "two kv heads in one lane tile, an even number of query tiles each"
    tiles_per_kv = n_heads // 4
    heads = [[2 * (kvh * tiles_per_kv + i) + half for i in range(tiles_per_kv)]
             for kvh in range(2) for half in range(2)]
    bias_g = jnp.stack([jnp.concatenate([bias[h].T for h in hs], axis=1) for hs in heads])
    sink_g = jnp.stack([jnp.concatenate([jnp.ones((1, cq), F32) * sinks[h] for h in hs], axis=1)
                        for hs in heads])

    def kv_spec(s):
        return pl.BlockSpec((ck * batch, kw), lambda c: (jnp.maximum(c - (n_kv - 1) + s, 0), 0))

    return pl.pallas_call(
        functools.partial(_attn_kernel, n_kv=n_kv, masked=masked, tiles_per_kv=tiles_per_kv, batch=batch),
        grid=(tq // (cq * batch),),
        in_specs=[pl.BlockSpec((cq * batch, LANES), lambda c, j=j: (c, j)) for j in range(qw // LANES)]
        + [kv_spec(s) for s in range(n_kv)] + [kv_spec(s) for s in range(n_kv)]
        + [pl.BlockSpec(bias_g.shape, lambda c: (0, 0, 0)), pl.BlockSpec(sink_g.shape, lambda c: (0, 0, 0))],
        out_specs=pl.BlockSpec((cq * batch, qw), lambda c: (c, 0)),
        out_shape=jax.ShapeDtypeStruct((tq, qw), BF16),
        scratch_shapes=[pltpu.VMEM((qw // LANES, cq * batch, LANES), F32)],
        compiler_params=_params("parallel"),
        name="swa_attention",
    )(*([q] * (qw // LANES)), *([k] * n_kv), *([v] * n_kv), bias_g, sink_g)


def _t5_bucket(rel):
    half = N_BUCKETS // 2
    max_exact = half // 2
    n = jnp.abs(rel)
    large = max_exact + (jnp.log(jnp.maximum(n, 1).astype(F32) / max_exact)
                         / math.log(MAX_DISTANCE / max_exact) * (half - max_exact)).astype(I32)
    large = jnp.minimum(large, half - 1)
    return jnp.where(rel > 0, half, 0) + jnp.where(n < max_exact, n, large)


def _bias_table(rel_bias, nq, nk):
    rel = (jnp.arange(nk) - WINDOW)[None, :] - jnp.arange(nq)[:, None]
    return jnp.transpose(rel_bias.astype(F32)[_t5_bucket(rel)], (2, 0, 1))


def _router_kernel(x_ref, w_ref, b_ref, route_ref, cnt_ref, carry, *, n_fine):
    @pl.when(pl.program_id(0) == 0)
    def _():
        carry[...] = jnp.zeros_like(carry)

    logits = _bdot(x_ref[...].astype(BF16), w_ref[...]) + b_ref[...]
    tm = logits.shape[0]
    lane = lax.broadcasted_iota(I32, logits.shape, 1)
    lanef = lane.astype(F32)
    per_group = n_fine // E_GROUPS
    big = float(LANES)

    def first_max(mask):
        mx = jnp.max(jnp.where(mask, logits, -jnp.inf), axis=-1, keepdims=True)
        idx = jnp.min(jnp.where(mask & (logits == mx), lanef, big), axis=-1, keepdims=True)
        return mx, idx

    cmask = lane < E_GROUPS
    mc, grp = first_max(cmask)
    p_grp = 1.0 / jnp.sum(jnp.where(cmask, jnp.exp(logits - mc), 0.0), axis=-1, keepdims=True)
    lo = E_GROUPS + per_group * grp
    fmask = (lanef >= lo) & (lanef < lo + per_group)
    m1, i1 = first_max(fmask)
    sel0 = lanef == i1
    m2, i2 = first_max(fmask & jnp.logical_not(sel0))
    e = jnp.exp(m2 - m1)
    g0 = (1.0 / (1.0 + e)) * p_grp
    g1 = (e / (1.0 + e)) * p_grp

    j1 = i1 - lo
    j2 = i2 - lo
    swap = j1 > j2
    j_lo = jnp.minimum(j1, j2)
    j_hi = jnp.maximum(j1, j2)
    n_pairs_group = per_group * (per_group - 1) // 2
    bucket = grp * n_pairs_group + j_lo * (2 * per_group - 1 - j_lo) * 0.5 + (j_hi - j_lo - 1.0)
    g_lo = jnp.where(swap, g1, g0)
    g_hi = jnp.where(swap, g0, g1)

    sel = lanef == bucket
    cmat = jnp.where(sel, 1.0, 0.0)
    row = lax.broadcasted_iota(I32, (tm, tm), 0)
    col = lax.broadcasted_iota(I32, (tm, tm), 1)
    tril = jnp.where(col < row, 1.0, 0.0).astype(BF16)
    before = carry[...] + _bdot(tril, cmat.astype(BF16))
    rank = jnp.sum(jnp.where(sel, before, 0.0), axis=-1, keepdims=True)
    carry[...] = carry[...] + jnp.sum(cmat, axis=0, keepdims=True)
    cnt_ref[...] = carry[...]
    vals = (bucket, rank, g_lo, g_hi)
    route = jnp.zeros_like(logits)
    for j, val in enumerate(vals):
        route = jnp.where(lane == j, val, route)
    route_ref[...] = route


def _router(x, w_route, b_route, n_fine):
    t, d = x.shape
    tm = min(ROW_TILE, t)
    return pl.pallas_call(
        functools.partial(_router_kernel, n_fine=n_fine),
        grid=(t // tm,),
        in_specs=[pl.BlockSpec((tm, d), lambda i: (i, 0)), pl.BlockSpec((d, LANES), lambda i: (0, 0)),
                  pl.BlockSpec((1, LANES), lambda i: (0, 0))],
        out_specs=[pl.BlockSpec((tm, LANES), lambda i: (i, 0)), pl.BlockSpec((1, LANES), lambda i: (0, 0))],
        out_shape=[jax.ShapeDtypeStruct((t, LANES), F32), jax.ShapeDtypeStruct((1, LANES), F32)],
        scratch_shapes=[pltpu.VMEM((1, LANES), F32)],
        compiler_params=_params("arbitrary"),
        name="moe_router",
    )(x, w_route, b_route)


def _row_copy(src_ref, src_row, dst_ref, dst_row, sem):
    return pltpu.make_async_copy(src_ref.at[pl.ds(src_row, 1)], dst_ref.at[pl.ds(dst_row, 1)], sem)


def _dispatch_kernel(dest_ref, x_ref, xs_in_ref, xs_ref, sem, *, tm):
    del xs_in_ref
    i = pl.program_id(0)
    base = i * tm

    def issue(r, carry):
        _row_copy(x_ref, base + r, xs_ref, dest_ref[0, r], sem).start()
        return carry

    def wait_tile():
        pltpu.make_async_copy(x_ref.at[pl.ds(0, tm)], xs_ref.at[pl.ds(0, tm)], sem).wait()

    lax.fori_loop(0, tm, issue, 0, unroll=8)

    @pl.when(i > 0)
    def _():
        wait_tile()

    @pl.when(i == pl.num_programs(0) - 1)
    def _():
        wait_tile()


def _dispatch(dest, x, n_rows):
    t, d = x.shape
    tm = min(ROW_TILE, t)
    return pl.pallas_call(
        functools.partial(_dispatch_kernel, tm=tm),
        grid=(t // tm,),
        in_specs=[pl.BlockSpec((1, tm), lambda i: (0, i), memory_space=pltpu.SMEM),
                  pl.BlockSpec(memory_space=pl.ANY),
                  pl.BlockSpec(memory_space=pl.ANY)],
        out_specs=pl.BlockSpec(memory_space=pl.ANY),
        out_shape=jax.ShapeDtypeStruct((n_rows, d), F32),
        scratch_shapes=[pltpu.SemaphoreType.DMA(())],
        input_output_aliases={2: 0},
        compiler_params=_params("arbitrary"),
        name="moe_dispatch",
    )(dest, x, jnp.zeros((n_rows, d), F32))


def _expert_kernel(ea_ref, eb_ref, nu_ref, xs_ref, wga_ref, wua_ref, wda_ref, wgb_ref, wub_ref, wdb_ref, ys_ref):
    del ea_ref, eb_ref
    i = pl.program_id(0)
    d = xs_ref.shape[1]

    @pl.when(i < nu_ref[0])
    def _():
        xb = xs_ref[...].astype(BF16)
        for half, (wg_ref, wu_ref, wd_ref) in enumerate(((wga_ref, wua_ref, wda_ref), (wgb_ref, wub_ref, wdb_ref))):
            a = _bdot(xb, wg_ref[...])
            h = (a * _sigmoid(a)) * _bdot(xb, wu_ref[...])
            ys_ref[:, half * d:(half + 1) * d] = _bdot(h.astype(BF16), wd_ref[...])

    @pl.when(i >= nu_ref[0])
    def _():
        ys_ref[...] = jnp.zeros_like(ys_ref)


def _experts(block_ea, block_eb, n_used, xs, wg, wu, wd, blk):
    n_rows, d = xs.shape
    de = wg.shape[-1]
    low = lambda i, ea, eb, nu: (ea[i], 0, 0)
    high = lambda i, ea, eb, nu: (eb[i], 0, 0)
    grid_spec = pltpu.PrefetchScalarGridSpec(
        num_scalar_prefetch=3,
        grid=(n_rows // blk,),
        in_specs=[pl.BlockSpec((blk, d), lambda i, ea, eb, nu: (i, 0)),
                  pl.BlockSpec((None, d, de), low), pl.BlockSpec((None, d, de), low),
                  pl.BlockSpec((None, de, d), low),
                  pl.BlockSpec((None, d, de), high), pl.BlockSpec((None, d, de), high),
                  pl.BlockSpec((None, de, d), high)],
        out_specs=pl.BlockSpec((blk, 2 * d), lambda i, ea, eb, nu: (i, 0)),
    )
    return pl.pallas_call(
        _expert_kernel, grid_spec=grid_spec,
        out_shape=jax.ShapeDtypeStruct((n_rows, 2 * d), F32),
        compiler_params=_params("arbitrary"),
        name="moe_experts",
    )(block_ea, block_eb, n_used, xs, wg, wu, wd, wg, wu, wd)


def _combine_kernel(dest_ref, next_dest_ref, x_ref, gate_ref, g_ref, b_ref, ys_ref, o_ref, ybuf, sems, *, tm, alpha):
    i = pl.program_id(0)
    slot = lax.rem(i, 2)

    def gather(dref, buf_slot, r):
        return _row_copy(ys_ref, dref[0, r], ybuf.at[buf_slot], r, sems.at[buf_slot])

    def wait_tile(buf_slot):
        pltpu.make_async_copy(ys_ref.at[pl.ds(0, tm)], ybuf.at[buf_slot], sems.at[buf_slot]).wait()

    @pl.when(i == 0)
    def _():
        def issue(r, carry):
            gather(dest_ref, 0, r).start()
            return carry

        lax.fori_loop(0, tm, issue, 0, unroll=8)

    wait_tile(slot)
    for r in range(tm):
        gather(next_dest_ref, 1 - slot, r).start()
    gate = gate_ref[...]
    d = x_ref.shape[1]
    rows = ybuf.at[slot]
    y = alpha * x_ref[...] + gate[:, 0:1] * rows[:, :d] + gate[:, 1:2] * rows[:, d:]
    o_ref[...] = _layer_norm(y, g_ref[...], b_ref[...])

    @pl.when(i == pl.num_programs(0) - 1)
    def _():
        wait_tile(1 - slot)


def _combine(dest, x, gates, ys, ln_g, ln_b, alpha):
    t, d = x.shape
    tm = min(ROW_TILE, t)
    last = t // tm - 1
    return pl.pallas_call(
        functools.partial(_combine_kernel, tm=tm, alpha=alpha),
        grid=(t // tm,),
        in_specs=[pl.BlockSpec((1, tm), lambda i: (0, i), memory_space=pltpu.SMEM),
                  pl.BlockSpec((1, tm), lambda i: (0, jnp.minimum(i + 1, last)), memory_space=pltpu.SMEM),
                  pl.BlockSpec((tm, d), lambda i: (i, 0)),
                  pl.BlockSpec((tm, 2), lambda i: (i, 0)),
                  pl.BlockSpec((1, d), lambda i: (0, 0)), pl.BlockSpec((1, d), lambda i: (0, 0)),
                  pl.BlockSpec(memory_space=pl.ANY)],
        out_specs=pl.BlockSpec((tm, d), lambda i: (i, 0)),
        out_shape=jax.ShapeDtypeStruct((t, d), F32),
        scratch_shapes=[pltpu.VMEM((2, tm, 2 * d), F32), pltpu.SemaphoreType.DMA((2,))],
        compiler_params=_params("arbitrary"),
        name="moe_combine_ln",
    )(dest, dest, x, gates, ln_g.reshape(1, d), ln_b.reshape(1, d), ys)


def _moe_dense_kernel(x_ref, gate_ref, wg_ref, wu_ref, wd_ref, g_ref, b_ref, o_ref, acc, *, alpha):
    e = pl.program_id(0)

    @pl.when(e == 0)
    def _():
        acc[...] = jnp.zeros_like(acc)

    x = x_ref[...]
    xb = x.astype(BF16)
    a = _bdot(xb, wg_ref[...])
    h = (a * _sigmoid(a)) * _bdot(xb, wu_ref[...])
    y = _bdot(h.astype(BF16), wd_ref[...])
    gates = gate_ref[...]
    lane = lax.broadcasted_iota(I32, gates.shape, 1)
    acc[...] += jnp.sum(jnp.where(lane == e, gates, 0.0), axis=-1, keepdims=True) * y

    @pl.when(e == pl.num_programs(0) - 1)
    def _():
        o_ref[...] = _layer_norm(alpha * x + acc[...], g_ref[...], b_ref[...])


def _moe_dense(x, dense_gates, wg, wu, wd, ln_g, ln_b, alpha):
    t, d = x.shape
    n_exp, _, de = wg.shape
    whole = lambda e: (0, 0)
    return pl.pallas_call(
        functools.partial(_moe_dense_kernel, alpha=alpha),
        grid=(n_exp,),
        in_specs=[pl.BlockSpec((t, d), whole), pl.BlockSpec((t, LANES), whole),
                  pl.BlockSpec((None, d, de), lambda e: (e, 0, 0)), pl.BlockSpec((None, d, de), lambda e: (e, 0, 0)),
                  pl.BlockSpec((None, de, d), lambda e: (e, 0, 0)),
                  pl.BlockSpec((1, d), whole), pl.BlockSpec((1, d), whole)],
        out_specs=pl.BlockSpec((t, d), whole),
        out_shape=jax.ShapeDtypeStruct((t, d), F32),
        scratch_shapes=[pltpu.VMEM((t, d), F32)],
        compiler_params=_params("arbitrary"),
        name="moe_dense_ln",
    )(x, dense_gates, wg, wu, wd, ln_g.reshape(1, d), ln_b.reshape(1, d))


def _moe_ln(x, w_c, b_c, w_f, b_f, wg, wu, wd, ln_g, ln_b, alpha):
    t, d = x.shape
    n_exp = w_f.shape[1]
    per_group = n_exp // E_GROUPS
    pad = LANES - E_GROUPS - n_exp
    w_route = jnp.concatenate([w_c.astype(F32), w_f.astype(F32), jnp.zeros((d, pad), F32)], axis=1)
    b_route = jnp.concatenate([b_c.astype(F32), b_f.astype(F32), jnp.zeros((pad,), F32)]).reshape(1, LANES)
    route, cnt = _router(x, w_route.astype(BF16), b_route, n_exp)
    bucket = route[:, 0].astype(I32)
    rank = route[:, 1].astype(I32)
    gates = route[:, 2:4]
    pairs = [(g * per_group + a, g * per_group + b) for g in range(E_GROUPS)
             for a in range(per_group) for b in range(a + 1, per_group)]
    n_buckets = len(pairs)
    assert n_buckets <= LANES
    pair_tab = jnp.array(pairs, dtype=I32)
    ids = jnp.arange(n_buckets, dtype=I32)

    def lookup(table, idx):
        return jnp.sum(jnp.where(idx[:, None] == ids[None, :], table[None, :], 0), axis=1)

    if t < MOE_DENSE_BELOW:
        lane = jnp.arange(LANES, dtype=I32)[None, :]
        e_lo = lookup(pair_tab[:, 0], bucket)[:, None]
        e_hi = lookup(pair_tab[:, 1], bucket)[:, None]
        dense_gates = jnp.where(lane == e_lo, gates[:, 0:1], 0.0) + jnp.where(lane == e_hi, gates[:, 1:2], 0.0)
        return _moe_dense(x, dense_gates, wg, wu, wd, ln_g, ln_b, alpha)
    blk = MOE_BLOCK
    counts = cnt[0, :n_buckets].astype(I32)
    padded = (counts + blk - 1) // blk * blk
    pend = jnp.cumsum(padded)
    poff = pend - padded
    dest =(lookup(poff, bucket) + rank).reshape(1, t)
    n_blocks = -(-t // blk) + n_buckets
    starts = jnp.arange(n_blocks, dtype=I32) * blk
    block_bucket = jnp.minimum(jnp.sum(pend[None, :] <= starts[:, None], axis=1), n_buckets - 1)
    block_ea = lookup(pair_tab[:, 0], block_bucket)
    block_eb = lookup(pair_tab[:, 1], block_bucket)
    n_used = (pend[-1] // blk).astype(I32).reshape(1)
    xs = _dispatch(dest, x, n_blocks * blk)
    ys = _experts(block_ea, block_eb, n_used, xs, wg, wu, wd, blk)
    return _combine(dest, x, gates, ys, ln_g, ln_b, alpha)


def _head_sum(x, ones):
    hi = x.astype(BF16)
    lo = (x - hi.astype(F32)).astype(BF16)
    parts = []
    for j in range(x.shape[1] // LANES):
        sl = slice(j * LANES, (j + 1) * LANES)
        parts.append(_bdot(hi[:, sl], ones) + _bdot(lo[:, sl], ones))
    return jnp.concatenate(parts, axis=-1)


def _rw_prep_kernel(sh_ref, halo_ref, sh0_ref, mu_ref, w0_ref, w2_ref, a0_ref, a2_ref, g2_ref,
                    kk_ref, ka_ref, rk_ref, ones_ref,
                    r_o, w_o, k_o, v_o, kk_o, b_o, bonus_o, g_o, *, batch, cw):
    sh = sh_ref[...]
    first = jnp.where(pl.program_id(0) == 0, sh0_ref[...], halo_ref[...])
    prev = jnp.concatenate([first, sh[:sh.shape[0] - batch]], axis=0)
    shm = sh + (prev - sh) * mu_ref[...]
    r, k, v = shm[:, :cw], shm[:, cw:2 * cw], shm[:, 2 * cw:3 * cw]
    xwa = shm[:, 3 * cw:3 * cw + LANES]
    xg = shm[:, 3 * cw + LANES:]
    ones = ones_ref[...]
    z = w0_ref[...] + _bdot(jnp.tanh(xwa).astype(BF16), w2_ref[...])
    softplus = jnp.maximum(-z, 0.0) + jnp.log1p(jnp.exp(-jnp.abs(z)))
    w_o[...] = jnp.exp(-jnp.exp(-softplus - 0.5))
    a = _sigmoid(a0_ref[...] + _bdot(xwa.astype(BF16), a2_ref[...]))
    g_o[...] = _bdot(_sigmoid(xg).astype(BF16), g2_ref[...])
    kk = k * kk_ref[...]
    kk = kk / jnp.maximum(jnp.sqrt(_head_sum(kk * kk, ones)), 1e-12)
    k2 = k * (1.0 + (a - 1.0) * ka_ref[...])
    r_o[...] = r
    k_o[...] = k2
    v_o[...] = v
    kk_o[...] = kk
    b_o[...] = kk * a
    bonus_o[...] = _head_sum(r * k2 * rk_ref[...], ones) * v


def _rw_prep(sh, shift0, batch, prm):
    t, sw = sh.shape
    cw = prm["w0"].shape[-1]
    tm = min(256, t)
    hb = tm // batch
    row = lambda i: (i, 0)
    fix = lambda i: (0, 0)
    vec = lambda n: pl.BlockSpec((1, n), fix)
    return pl.pallas_call(
        functools.partial(_rw_prep_kernel, batch=batch, cw=cw),
        grid=(t // tm,),
        in_specs=[pl.BlockSpec((tm, sw), row),
                  pl.BlockSpec((batch, sw), lambda i: (jnp.maximum(i * hb - 1, 0), 0)),
                  pl.BlockSpec((batch, sw), fix), vec(sw), vec(cw),
                  pl.BlockSpec((LANES, cw), fix), vec(cw), pl.BlockSpec((LANES, cw), fix),
                  pl.BlockSpec((LANES, cw), fix), vec(cw), vec(cw), vec(cw),
                  pl.BlockSpec((LANES, LANES), fix)],
        out_specs=[pl.BlockSpec((tm, cw), row)] * 8,
        out_shape=[jax.ShapeDtypeStruct((t, cw), F32)] * 8,
        compiler_params=_params("parallel"),
        name="rwkv_prep",
    )(sh, sh, shift0, prm["mu"], prm["w0"], prm["w2"], prm["a0"], prm["a2"], prm["g2"],
      prm["k_k"], prm["k_a"], prm["r_k"], prm["ones"])


RW_TILES_A = 4
RW_TILES_B = 2
RW_TICK = 8


def _rwkv_kernel(r_ref, w_ref, k_ref, v_ref, kk_ref, b_ref, sa0_ref, sb0_ref, y_ref, san_ref, sbn_ref,
                 st_a, st_b, xa0, xb0, xa1, xb1, *, batch, steps):
    hd = HEAD_DIM
    hv = hd // 2
    ti = pl.program_id(0)

    @pl.when(ti == 0)
    def _():
        st_a[...] = sa0_ref[...]
        st_b[...] = sb0_ref[...]

    operands = (r_ref, w_ref, k_ref, v_ref, kk_ref, b_ref)
    def lane_of(rows):
        return lax.broadcasted_iota(I32, (rows, LANES), 1)

    lane_lo = lane_of(hd) < hd
    lane_lo_out = lane_of(batch) < hd
    even_quarter = (lane_of(hd) // hv) % 2 == 0
    quarter_v = lane_of(hv) // hv
    quarter_out = lane_of(batch) // hv

    def pick_quarter(q, blocks):
        return jnp.where(q == 0, blocks[0], jnp.where(q == 1, blocks[1], jnp.where(q == 2, blocks[2], blocks[3])))

    def padded(parts, rows):
        have = len(parts) * batch
        return parts + ([jnp.zeros((rows - have, LANES), F32)] if have < rows else [])

    def run(state, x, nv, tick):
        acc = jnp.zeros((nv, LANES), F32)
        for kidx in range(hd):
            acc = acc + state[kidx] * x[4, kidx:kidx + 1, :]
            if kidx % RW_TICK == RW_TICK - 1:
                tick()
        sa = acc
        vv = x[3, 0:nv, :]
        acc = jnp.zeros((nv, LANES), F32)
        for kidx in range(hd):
            row = slice(kidx, kidx + 1)
            sn = state[kidx] * x[1, row, :] - sa * x[5, row, :] + vv * x[2, row, :]
            state[kidx] = sn
            acc = acc + sn * x[0, row, :]
            if kidx % RW_TICK == RW_TICK - 1:
                tick()
        return acc

    def transpose_operand(o, r0, xa, xb):
        slab = operands[o][pl.ds(r0, batch), :]
        tiles = [slab[:, j * LANES:(j + 1) * LANES] for j in range(RW_TILES_A + RW_TILES_B)]
        ta = jnp.concatenate(padded(tiles[:RW_TILES_A], hd) * 2, axis=0).T
        xa[o] = jnp.where(lane_lo, ta[:hd], ta[hd:])
        tb = jnp.concatenate(padded(tiles[RW_TILES_A:], hv) * 4, axis=0).T
        if o == 3:
            xb[o, pl.ds(0, hv), :] = pick_quarter(quarter_v, (tb[:hv], tb[hd:hd + hv], tb[hv:hd], tb[hd + hv:]))
        else:
            xb[o] = jnp.where(even_quarter, tb[:hd], tb[hd:])

    def time_step(t, cur, nxt):
        r0 = pl.multiple_of(t * batch, batch)
        rn = pl.multiple_of(jnp.minimum(t + 1, steps - 1) * batch, batch)
        pending = [functools.partial(transpose_operand, o, rn, *nxt) for o in range(len(operands))]

        def tick():
            if pending:
                pending.pop(0)()

        ya = run(st_a, cur[0], hd, tick)
        yb = run(st_b, cur[1], hv, tick)
        while pending:
            tick()
        mta = jnp.concatenate([ya, ya], axis=0).T
        for j in range(RW_TILES_A):
            lo = mta[j * batch:(j + 1) * batch]
            hi = mta[hd + j * batch:hd + (j + 1) * batch]
            y_ref[pl.ds(r0, batch), j * LANES:(j + 1) * LANES] = jnp.where(lane_lo_out, lo, hi)
        mtb = jnp.concatenate([yb] * 4, axis=0).T
        for j in range(RW_TILES_B):
            blocks = [mtb[base + j * batch:base + (j + 1) * batch] for base in (0, hd, hv, hd + hv)]
            tile = RW_TILES_A + j
            y_ref[pl.ds(r0, batch), tile * LANES:(tile + 1) * LANES] = pick_quarter(quarter_out, blocks)

    even_bufs, odd_bufs = (xa0, xb0), (xa1, xb1)
    for o in range(len(operands)):
        transpose_operand(o, 0, *even_bufs)

    def step_pair(p, carry):
        time_step(2 * p, even_bufs, odd_bufs)
        time_step(2 * p + 1, odd_bufs, even_bufs)
        return carry

    lax.fori_loop(0, steps // 2, step_pair, 0)

    @pl.when(ti == pl.num_programs(0) - 1)
    def _():
        san_ref[...] = st_a[...]
        sbn_ref[...] = st_b[...]


def _rwkv(ops, s0, batch):
    t, cw = ops[0].shape
    hd, hv = HEAD_DIM, HEAD_DIM // 2
    assert cw == (RW_TILES_A + RW_TILES_B) * LANES and RW_TILES_A * batch <= hd
    ca, cb = RW_TILES_A * batch, RW_TILES_B * batch
    if s0 is None:
        sa0 = jnp.zeros((hd, hd, LANES), F32)
        sb0 = jnp.zeros((hd, hv, LANES), F32)
    else:
        s0 = s0.astype(F32)
        sa0 = s0[:, :2 * RW_TILES_A].reshape(batch, RW_TILES_A, 2, hd, hd).transpose(4, 3, 2, 1, 0)
        sa0 = jnp.pad(sa0.reshape(hd, hd, 2, ca), ((0, 0),) * 3 + ((0, hd - ca),)).reshape(hd, hd, LANES)
        sb0 = s0[:, 2 * RW_TILES_A:].reshape(batch, RW_TILES_B, 2, 2, hv, hd).transpose(5, 4, 3, 2, 1, 0)
        sb0 = jnp.pad(sb0.reshape(hd, hv, 2, 2, cb), ((0, 0),) * 4 + ((0, hv - cb),)).reshape(hd, hv, LANES)
    steps = min(32, t // batch)
    assert steps % 2 == 0, "the time loop is unrolled in even/odd operand-buffer pairs"
    rows = steps * batch
    blk = pl.BlockSpec((rows, cw), lambda i: (i, 0))
    st_a = pl.BlockSpec((hd, hd, LANES), lambda i: (0, 0, 0))
    st_b = pl.BlockSpec((hd, hv, LANES), lambda i: (0, 0, 0))
    y, san, sbn = pl.pallas_call(
        functools.partial(_rwkv_kernel, batch=batch, steps=steps),
        grid=(t // rows,),
        in_specs=[blk] * 6 + [st_a, st_b],
        out_specs=[blk, st_a, st_b],
        out_shape=[jax.ShapeDtypeStruct((t, cw), F32), jax.ShapeDtypeStruct((hd, hd, LANES), F32),
                   jax.ShapeDtypeStruct((hd, hv, LANES), F32)],
        scratch_shapes=[pltpu.VMEM((hd, hd, LANES), F32), pltpu.VMEM((hd, hv, LANES), F32),
                        ] + [pltpu.VMEM((6, hd, LANES), F32)] * 4,
        compiler_params=_params("arbitrary"),
        name="rwkv_recurrence",
    )(*ops, sa0, sb0)
    sa = san.reshape(hd, hd, 2, hd)[..., :ca].reshape(hd, hd, 2, RW_TILES_A, batch).transpose(4, 3, 2, 1, 0)
    sb = sbn.reshape(hd, hv, 2, 2, hv)[..., :cb].reshape(hd, hv, 2, 2, RW_TILES_B, batch).transpose(5, 4, 3, 2, 1, 0)
    s_last = jnp.concatenate([sa.reshape(batch, 2 * RW_TILES_A, hd, hd), sb.reshape(batch, 2 * RW_TILES_B, hd, hd)],
                             axis=1)
    return y, s_last


def _rw_post_kernel(y_ref, bonus_ref, g_ref, lg_ref, lb_ref, ones_ref, o_ref):
    y = y_ref[...]
    ones = ones_ref[...]
    mu = _head_sum(y, ones) * (1.0 / HEAD_DIM)
    d = y - mu
    var = _head_sum(d * d, ones) * (1.0 / HEAD_DIM)
    yn = d * lax.rsqrt(var + GN_EPS) * lg_ref[...] + lb_ref[...]
    o_ref[...] = ((yn + bonus_ref[...]) * g_ref[...]).astype(o_ref.dtype)


def _rw_post(y, bonus, g, ln_g, ln_b, ones):
    t, cw = y.shape
    tm = min(ROW_TILE, t)
    row = pl.BlockSpec((tm, cw), lambda i: (i, 0))
    vec = pl.BlockSpec((1, cw), lambda i: (0, 0))
    return pl.pallas_call(
        _rw_post_kernel, grid=(t // tm,),
        in_specs=[row, row, row, vec, vec, pl.BlockSpec((LANES, LANES), lambda i: (0, 0))],
        out_specs=row, out_shape=jax.ShapeDtypeStruct((t, cw), BF16),
        compiler_params=_params("parallel"),
        name="rwkv_post",
    )(y, bonus, g, ln_g, ln_b, ones)


def _pool_kernel(main_ref, halo_ref, w_ref, scale_ref, o_ref, *, batch, pos0):
    tm = main_ref.shape[0]
    full = jnp.concatenate([main_ref[...], halo_ref[...]], axis=0)
    b = batch
    n = full.shape[0]
    a2 = full[b:] + full[:n - b]
    a4 = a2[2 * b:] + a2[:n - 3 * b]
    a8 = a4[4 * b:] + a4[:n - 7 * b]
    a16 = a8[8 * b:] + a8[:n - 15 * b]
    cur = full[16 * b:16 * b + tm]
    sums = (a2[15 * b:15 * b + tm], a4[13 * b:13 * b + tm], a8[9 * b:9 * b + tm], a16[b:b + tm])
    lane = lax.broadcasted_iota(I32, cur.shape, 1)
    grp = lane // (cur.shape[1] // len(POOL_WINDOWS))
    win_sum = sums[-1]
    width = jnp.full(cur.shape, float(POOL_WINDOWS[-1]), F32)
    for j in range(len(POOL_WINDOWS) - 2, -1, -1):
        win_sum = jnp.where(grp == j, sums[j], win_sum)
        width = jnp.where(grp == j, float(POOL_WINDOWS[j]), width)
    row = lax.broadcasted_iota(I32, cur.shape, 0)
    t_idx = pl.program_id(0) * (tm // b) + row // b
    cnt = jnp.minimum(width, (pos0 + t_idx + 1).astype(F32))
    pooled = win_sum / cnt - cur
    o_ref[...] = (_bdot(pooled.astype(BF16), w_ref[...]) * scale_ref[...]).astype(o_ref.dtype)


def _pool(fp, t, batch, w_bd, scale, pos0):
    dw = fp.shape[1]
    tm = min(ROW_TILE, t)
    halo = 16 * batch
    return pl.pallas_call(
        functools.partial(_pool_kernel, batch=batch, pos0=pos0),
        grid=(t // tm,),
        in_specs=[pl.BlockSpec((tm, dw), lambda i: (i, 0)),
                  pl.BlockSpec((halo, dw), lambda i: ((i + 1) * (tm // halo), 0)),
                  pl.BlockSpec((dw, dw), lambda i: (0, 0)), pl.BlockSpec((1, dw), lambda i: (0, 0))],
        out_specs=pl.BlockSpec((tm, dw), lambda i: (i, 0)),
        out_shape=jax.ShapeDtypeStruct((t, dw), BF16),
        compiler_params=_params("parallel"),
        name="causal_pool",
    )(fp, fp, w_bd, scale)


def _to_batch_major(a, seq, batch):
    return a.reshape(seq, batch, a.shape[-1]).transpose(1, 0, 2)


def _to_time_major(a):
    bsz, seq, width = a.shape
    return a.transpose(1, 0, 2).reshape(seq * bsz, width)


def _even_mixer(x, batch, seq, h0, cache_k, cache_v, w, j, bias, alpha, ln_g, ln_b):
    a_width = w["s5_w_glu"].shape[-1]
    w_in = w["ev_w_in"][j]
    qw = a_width
    kvw = (w_in.shape[1] - 2 * a_width) // 2
    u, q, k, v = _matmul([x], [w_in.astype(BF16)], splits=(a_width, qw, kvw, kvw))
    prm = _s5_params(w["s5_a_re"][j], w["s5_a_im"][j], w["s5_log_dt"][j], w["s5_b_re"][j], w["s5_b_im"][j],
                     w["s5_c_re"][j], w["s5_c_im"][j], w["s5_d"][j])
    nt = a_width // LANES
    ns = prm["bre"].shape[-1]
    if h0 is None:
        h0re = jnp.zeros((nt, batch, ns), F32)
        h0im = h0re
    else:
        h0re = h0[..., 0].astype(F32).reshape(batch, nt, ns).transpose(1, 0, 2)
        h0im = h0[..., 1].astype(F32).reshape(batch, nt, ns).transpose(1, 0, 2)
    g, hn_re, hn_im = _s5(u, batch, h0re, h0im, prm)
    groups, n_state = w["s5_a_re"][j].shape
    h_last = jnp.stack([hn_re.transpose(1, 0, 2).reshape(batch, groups, n_state),
                        hn_im.transpose(1, 0, 2).reshape(batch, groups, n_state)], axis=-1)
    a_out = _matmul([g], [w["s5_w_glu"][j].astype(BF16)], epilogue="glu", out_dtype=BF16)

    sinks = w["swa_sinks"][j].astype(F32)
    if cache_k is None:
        att = _attention(q, k, v, bias, sinks, batch, CHUNK, CHUNK, WINDOW // CHUNK + 1, True)
        k_all, v_all = k, v
    else:
        k_all = jnp.concatenate([_to_time_major(cache_k.astype(F32).reshape(batch, WINDOW, kvw)), k], axis=0)
        v_all = jnp.concatenate([_to_time_major(cache_v.astype(F32).reshape(batch, WINDOW, kvw)), v], axis=0)
        att = _attention(q, k_all, v_all, bias, sinks, batch, seq, WINDOW + seq, 1, False)
    keep = k_all.shape[0] - WINDOW * batch
    new_k = _to_batch_major(k_all[keep:], WINDOW, batch)
    new_v = _to_batch_major(v_all[keep:], WINDOW, batch)
    w_out = w["ev_w_out"][j].astype(BF16)
    xn = _matmul([a_out, att], [w_out[:a_width], w_out[a_width:]], epilogue="ln",
                 res=x, g=ln_g, b=ln_b, alpha=alpha)
    kv_shape = (batch, WINDOW, kvw // HEAD_DIM, HEAD_DIM)
    return xn, h_last, new_k.reshape(kv_shape), new_v.reshape(kv_shape)


def _odd_mixer(x, batch, seq, shift0, s0, pool0, pos0, w, j, alpha, ln_g, ln_b):
    t = x.shape[0]
    w_in = w["od_w_in"][j]
    cw = w["rw_w0"].shape[-1]
    sw = w["rw_mu"].shape[-1]
    dw = w_in.shape[1] - sw
    n_heads = cw // HEAD_DIM
    sh, p = _matmul([x], [w_in.astype(BF16)], splits=(sw, dw))
    zeros_half = jnp.zeros((LANES - w["rw_w2"].shape[1], cw), F32)
    head_lane = jnp.arange(LANES) // HEAD_DIM
    ones = (head_lane[:, None] == head_lane[None, :]).astype(BF16)
    prm = dict(
        mu=w["rw_mu"][j].astype(F32).reshape(1, sw), w0=w["rw_w0"][j].astype(F32).reshape(1, cw),
        w2=jnp.concatenate([w["rw_w2"][j].astype(F32), zeros_half], axis=0).astype(BF16),
        a0=w["rw_a0"][j].astype(F32).reshape(1, cw),
        a2=jnp.concatenate([zeros_half, w["rw_a2"][j].astype(F32)], axis=0).astype(BF16),
        g2=w["rw_g2"][j].astype(BF16),
        k_k=w["rw_k_k"][j].astype(F32).reshape(1, cw), k_a=w["rw_k_a"][j].astype(F32).reshape(1, cw),
        r_k=w["rw_r_k"][j].astype(F32).reshape(1, cw), ones=ones)
    if shift0 is None:
        shift0 = jnp.zeros((batch, sw), F32)
    r, dec, k2, v, kk, b, bonus, g = _rw_prep(sh, shift0.astype(F32), batch, prm)

    y, s_last = _rwkv((r, dec, k2, v, kk, b), s0, batch)
    c_out = _rw_post(y, bonus, g, w["rw_ln_g"][j].astype(F32).reshape(1, cw),
                     w["rw_ln_b"][j].astype(F32).reshape(1, cw), ones)

    if pool0 is None:
        hist = jnp.zeros((POOL_HIST * batch, dw), F32)
    else:
        hist = _to_time_major(pool0.astype(F32))
    fp = jnp.concatenate([jnp.zeros((batch, dw), F32), hist, p], axis=0)
    n_grp = len(POOL_WINDOWS)
    pg = dw // n_grp
    eye = jnp.eye(n_grp, dtype=F32)
    w_bd = (w["pool_w"][j].astype(F32)[:, :, None, :] * eye[:, None, :, None]).reshape(dw, dw).astype(BF16)
    d_out = _pool(fp, t, batch, w_bd, w["pool_scale"][j].astype(F32).reshape(1, dw), pos0)
    pool_new = _to_batch_major(fp[fp.shape[0] - POOL_HIST * batch:], POOL_HIST, batch)

    w_out = w["od_w_out"][j].astype(BF16)
    xn = _matmul([c_out, d_out], [w_out[:cw], w_out[cw:]], epilogue="ln", res=x, g=ln_g, b=ln_b, alpha=alpha)
    sh_last = sh[t - batch:]
    return xn, sh_last, s_last, pool_new


def _trunk(x3, s5_0, k_0, v_0, rw_0, shift_0, pool_0, pos0, w):
    batch, seq, _ = x3.shape
    depth = w["ln1_g"].shape[0]
    alpha = (2 * depth) ** 0.25
    x = _to_time_major(x3.astype(F32))
    prompt = s5_0 is None
    if prompt:
        bias = _bias_table(w["rel_bias"], CHUNK, (WINDOW // CHUNK + 1) * CHUNK)
    else:
        bias = _bias_table(w["rel_bias"], seq, WINDOW + seq)
    s5_new, k_new, v_new, rw_new, sh_new, pool_new = [], [], [], [], [], []
    for i in range(depth):
        j = i // 2
        if i % 2 == 0:
            x, h1, nk, nv = _even_mixer(
                x, batch, seq, None if prompt else s5_0[j], None if prompt else k_0[j],
                None if prompt else v_0[j], w, j, bias, alpha, w["ln1_g"][i], w["ln1_b"][i])
            s5_new.append(h1)
            k_new.append(nk)
            v_new.append(nv)
        else:
            x, sh1, s1, pl1 = _odd_mixer(
                x, batch, seq, None if prompt else shift_0[j], None if prompt else rw_0[j],
                None if prompt else pool_0[j], pos0, w, j, alpha, w["ln1_g"][i], w["ln1_b"][i])
            sh_new.append(sh1)
            rw_new.append(s1)
            pool_new.append(pl1)
        x = _moe_ln(x, w["moe_w_coarse"][i], w["moe_b_coarse"][i], w["moe_w_fine"][i], w["moe_b_fine"][i],
                    w["moe_w_gate"][i].astype(BF16), w["moe_w_up"][i].astype(BF16),
                    w["moe_w_down"][i].astype(BF16), w["ln2_g"][i], w["ln2_b"][i], alpha)
    y = _to_batch_major(x, seq, batch)
    return (y, jnp.stack(s5_new), jnp.stack(k_new), jnp.stack(v_new),
            jnp.stack(rw_new), jnp.stack(sh_new), jnp.stack(pool_new))


def kernel(x_prompt, x_sample, state_s5, cache_swa_k, cache_swa_v, state_rwkv, state_shift, state_pool, rel_bias, ev_w_in, ev_w_out, s5_a_re, s5_a_im, s5_log_dt, s5_b_re, s5_b_im, s5_c_re, s5_c_im, s5_d, s5_w_glu, swa_sinks, od_w_in, od_w_out, rw_mu, rw_w0, rw_w2, rw_a0, rw_a2, rw_g2, rw_k_k, rw_k_a, rw_r_k, rw_ln_g, rw_ln_b, pool_w, pool_scale, ln1_g, ln1_b, ln2_g, ln2_b, moe_w_coarse, moe_b_coarse, moe_w_fine, moe_b_fine, moe_w_gate, moe_w_up, moe_w_down):
    w = dict(rel_bias=rel_bias, ev_w_in=ev_w_in, ev_w_out=ev_w_out, s5_a_re=s5_a_re, s5_a_im=s5_a_im,
             s5_log_dt=s5_log_dt, s5_b_re=s5_b_re, s5_b_im=s5_b_im, s5_c_re=s5_c_re, s5_c_im=s5_c_im,
             s5_d=s5_d, s5_w_glu=s5_w_glu, swa_sinks=swa_sinks, od_w_in=od_w_in, od_w_out=od_w_out,
             rw_mu=rw_mu, rw_w0=rw_w0, rw_w2=rw_w2, rw_a0=rw_a0, rw_a2=rw_a2, rw_g2=rw_g2, rw_k_k=rw_k_k,
             rw_k_a=rw_k_a, rw_r_k=rw_r_k, rw_ln_g=rw_ln_g, rw_ln_b=rw_ln_b, pool_w=pool_w,
             pool_scale=pool_scale, ln1_g=ln1_g, ln1_b=ln1_b, ln2_g=ln2_g, ln2_b=ln2_b,
             moe_w_coarse=moe_w_coarse, moe_b_coarse=moe_b_coarse, moe_w_fine=moe_w_fine,
             moe_b_fine=moe_b_fine, moe_w_gate=moe_w_gate, moe_w_up=moe_w_up, moe_w_down=moe_w_down)
    y_s, s5_s, k_s, v_s, rw_s, sh_s, pool_s = _trunk(
        x_sample, state_s5, cache_swa_k, cache_swa_v, state_rwkv, state_shift, state_pool, PAST_LEN, w)
    y_p, s5_p, k_p, v_p, rw_p, sh_p, pool_p = _trunk(x_prompt, None, None, None, None, None, None, 0, w)
    return (y_p, y_s, s5_p, s5_s, k_p, v_p, k_s, v_s, rw_p, rw_s, sh_p, sh_s, pool_p, pool_s)
```
